```python
import math
import jax
import jax.numpy as jnp
from jax import lax
import numpy as np

D_MODEL = 2048
BATCH = 1
SEQ = 8192
DEPTH = 2
DEC_BATCH = 32
DEC_SEQ = 1
PAST_LEN = 8192
PAGE_SIZE = 128

N_A_LAYERS = DEPTH // 2
N_B_LAYERS = DEPTH - N_A_LAYERS
N_DENSE_LAYERS = (DEPTH + 1) // 2
N_MOE_LAYERS = DEPTH // 2

SSM_EXPAND = 2
SSM_D_INNER = SSM_EXPAND * D_MODEL
SSM_HEAD_DIM = 64
SSM_N_HEADS = SSM_D_INNER // SSM_HEAD_DIM
SSM_N_GROUPS = 8
SSM_D_STATE = 128
SSM_CONV_W = 4
SSM_CHUNK = 128
SSM_CONV_DIM = SSM_D_INNER + 2 * SSM_N_GROUPS * SSM_D_STATE
SSM_IN_DIM = SSM_D_INNER + SSM_CONV_DIM + SSM_N_HEADS
DT_MIN = 0.001
DT_MAX = 0.1

ATTN_HEAD_DIM = 128
ATTN_N_HEADS = D_MODEL // ATTN_HEAD_DIM
ATTN_N_KV_HEADS = 4
ATTN_GROUP = ATTN_N_HEADS // ATTN_N_KV_HEADS
ROT_DIM = ATTN_HEAD_DIM // 4
ROPE_THETA = 500000.0
MOBA_BLOCK = 256
MOBA_TOPK = 3
MOBA_QCHUNK = 32

D_FF = 11 * D_MODEL // 4
N_EXPERTS = 8
MOE_TOPK = 2
D_FF_EXPERT = 7 * D_MODEL // 2
MOE_BLOCK = 256

D_PLE = 256
DEEPNORM_ALPHA = (2 * DEPTH) ** 0.25
DEEPNORM_BETA = (8 * DEPTH) ** -0.25
LN_EPS = 1e-5
RMS_EPS = 1e-5

kernel_name = 'hybrid_ssd_moba_yoco_step'


def layernorm(x, g, b):
    xf = x.astype(jnp.float32)
    mu = jnp.mean(xf, axis=-1, keepdims=True)
    var = jnp.mean(jnp.square(xf - mu), axis=-1, keepdims=True)
    return ((xf - mu) * lax.rsqrt(var + LN_EPS) * g.astype(jnp.float32) + b.astype(jnp.float32)).astype(x.dtype)


def rope_partial(x, pos):
    half = ROT_DIM // 2
    inv = ROPE_THETA ** (-jnp.arange(half, dtype=jnp.float32) * 2.0 / ROT_DIM)
    ang = pos.astype(jnp.float32)[:, None] * inv[None, :]
    cos = jnp.cos(ang)[None, :, None, :]
    sin = jnp.sin(ang)[None, :, None, :]
    xf = x.astype(jnp.float32)
    x1 = xf[..., :half]
    x2 = xf[..., half:ROT_DIM]
    out = jnp.concatenate([x1 * cos - x2 * sin, x2 * cos + x1 * sin, xf[..., ROT_DIM:]], axis=-1)
    return out.astype(x.dtype)


def ssd_scan(x, dt, A, Bm, Cm, h0):
    b, L, H, P = x.shape
    G, N = Bm.shape[2], Bm.shape[3]
    R = H // G
    Q = SSM_CHUNK if L % SSM_CHUNK == 0 else L
    c = L // Q
    xf = x.astype(jnp.float32).reshape(b, c, Q, G, R, P)
    dtf = dt.reshape(b, c, Q, G, R)
    Bf = Bm.astype(jnp.float32).reshape(b, c, Q, G, N)
    Cf = Cm.astype(jnp.float32).reshape(b, c, Q, G, N)
    acs = jnp.cumsum(dtf * A.reshape(G, R), axis=2)
    xdt = xf * dtf[..., None]
    cb = jnp.einsum('bclgn,bcsgn->bcgls', Cf, Bf)
    at = jnp.moveaxis(acs, 2, -1)
    seg = at[..., :, None] - at[..., None, :]
    causal = jnp.tril(jnp.ones((Q, Q), dtype=bool))
    m = cb[:, :, :, None] * jnp.exp(jnp.where(causal, seg, -jnp.inf))
    y_diag = jnp.einsum('bcgrls,bcsgrp->bclgrp', m, xdt)
    last = acs[:, :, -1:]
    states = jnp.einsum('bclgn,bclgrp->bcgrpn', Bf, xdt * jnp.exp(last - acs)[..., None])
    chunk_decay = jnp.exp(last[:, :, 0])

    def carry_step(h, inp):
        st, dec = inp
        return h * dec[..., None, None] + st, h

    h_t, h_in = lax.scan(carry_step, h0.astype(jnp.float32).reshape(b, G, R, P, N),
                         (jnp.moveaxis(states, 1, 0), jnp.moveaxis(chunk_decay, 1, 0)))
    h_in = jnp.moveaxis(h_in, 0, 1)
    y_off = jnp.einsum('bclgn,bcgrpn->bclgrp', Cf, h_in) * jnp.exp(acs)[..., None]
    y = (y_diag + y_off).reshape(b, L, H, P)
    return y, h_t.reshape(b, H, P, N)


def mamba2_mixer(u, conv_state, ssm_state, w_in, conv_w, conv_b, dt_bias, a_log, d_skip, norm_w, w_out):
    b, L, _ = u.shape
    zxbcdt = u @ w_in
    z = zxbcdt[..., :SSM_D_INNER]
    xbc_raw = zxbcdt[..., SSM_D_INNER:SSM_D_INNER + SSM_CONV_DIM]
    dt = zxbcdt[..., SSM_D_INNER + SSM_CONV_DIM:]
    full = jnp.concatenate([conv_state.astype(xbc_raw.dtype), xbc_raw], axis=1)
    conv = conv_b
    for j in range(SSM_CONV_W):
        conv = conv + full[:, j:j + L] * conv_w[j]
    xbc = jax.nn.silu(conv)
    xs = xbc[..., :SSM_D_INNER].reshape(b, L, SSM_N_HEADS, SSM_HEAD_DIM)
    gn = SSM_N_GROUPS * SSM_D_STATE
    bm = xbc[..., SSM_D_INNER:SSM_D_INNER + gn].reshape(b, L, SSM_N_GROUPS, SSM_D_STATE)
    cm = xbc[..., SSM_D_INNER + gn:].reshape(b, L, SSM_N_GROUPS, SSM_D_STATE)
    dtv = jax.nn.softplus(dt.astype(jnp.float32) + dt_bias.astype(jnp.float32))
    a = -jnp.exp(a_log.astype(jnp.float32))
    y, h_t = ssd_scan(xs, dtv, a, bm, cm, ssm_state)
    y = (y + d_skip.astype(jnp.float32)[:, None] * xs.astype(jnp.float32)).reshape(b, L, SSM_D_INNER)
    g = y * jax.nn.silu(z.astype(jnp.float32))
    g = g * lax.rsqrt(jnp.mean(jnp.square(g), axis=-1, keepdims=True) + RMS_EPS) * norm_w.astype(jnp.float32)
    out = g.astype(u.dtype) @ w_out
    return out, h_t.astype(ssm_state.dtype), full[:, L:]


def moba_prepare(k_all, v_all):
    b, lk, kvh, hd = k_all.shape
    nb = -(-lk // MOBA_BLOCK)
    pad = nb * MOBA_BLOCK - lk
    kb = jnp.pad(k_all, ((0, 0), (0, pad), (0, 0), (0, 0))).reshape(b, nb, MOBA_BLOCK, kvh, hd)
    vb = jnp.pad(v_all, ((0, 0), (0, pad), (0, 0), (0, 0))).reshape(b, nb, MOBA_BLOCK, kvh, hd)
    means = jnp.mean(kb.astype(jnp.float32), axis=2)
    gidx = jnp.arange(ATTN_N_HEADS) // ATTN_GROUP
    return kb, vb, means[:, :, gidx]


def moba_attend(q, kb, vb, means_h, pos0):
    b, lq, nh, hd = q.shape
    nb = kb.shape[1]
    n_sel = min(MOBA_TOPK, nb)
    qc = MOBA_QCHUNK if lq % MOBA_QCHUNK == 0 else lq
    nq = lq // qc
    scale = ATTN_HEAD_DIM ** -0.5
    gidx = jnp.arange(nh) // ATTN_GROUP
    bidx = jnp.arange(b)[:, None, None, None]
    q_chunks = jnp.moveaxis(q.reshape(b, nq, qc, nh, hd), 1, 0)
    p_chunks = (pos0 + jnp.arange(lq, dtype=jnp.int32)).reshape(nq, qc)

    def attend_chunk(args):
        qi, pi = args
        qblk = pi // MOBA_BLOCK
        gate = jnp.einsum('bqhd,bnhd->bqhn', qi.astype(jnp.float32), means_h)
        past = jnp.arange(nb)[None, :] < qblk[:, None]
        gate = jnp.where(past[None, :, None, :], gate, -jnp.inf)
        _, sel = lax.top_k(gate, n_sel)
        valid = sel < qblk[None, :, None, None]
        k_sel = kb[bidx, sel, :, gidx[None, None, :, None]]
        v_sel = vb[bidx, sel, :, gidx[None, None, :, None]]
        k_own = kb[:, qblk]
        v_own = vb[:, qblk]
        s_sel = jnp.einsum('bqhd,bqhjsd->bqhjs', qi, k_sel).astype(jnp.float32) * scale
        s_sel = jnp.where(valid[..., None], s_sel, -jnp.inf)
        qg = qi.reshape(b, qc, ATTN_N_KV_HEADS, ATTN_GROUP, hd)
        s_own = jnp.einsum('bqkgd,bqskd->bqkgs', qg, k_own).astype(jnp.float32) * scale
        key_pos = qblk[:, None] * MOBA_BLOCK + jnp.arange(MOBA_BLOCK)[None, :]
        causal = key_pos <= pi[:, None]
        s_own = jnp.where(causal[None, :, None, None, :], s_own, -jnp.inf)
        logits = jnp.concatenate([s_sel.reshape(b, qc, nh, n_sel * MOBA_BLOCK),
                                  s_own.reshape(b, qc, nh, MOBA_BLOCK)], axis=-1)
        probs = jax.nn.softmax(logits, axis=-1).astype(vb.dtype)
        p_sel = probs[..., :n_sel * MOBA_BLOCK].reshape(b, qc, nh, n_sel, MOBA_BLOCK)
        p_own = probs[..., n_sel * MOBA_BLOCK:].reshape(b, qc, ATTN_N_KV_HEADS, ATTN_GROUP, MOBA_BLOCK)
        o_sel = jnp.einsum('bqhjs,bqhjsd->bqhd', p_sel, v_sel)
        o_own = jnp.einsum('bqkgs,bqskd->bqkgd', p_own, v_own).reshape(b, qc, nh, hd)
        return o_sel + o_own

    out = lax.map(attend_chunk, (q_chunks, p_chunks))
    return jnp.moveaxis(out, 0, 1).reshape(b, lq, nh, hd)


def swiglu(x, w_gu, w_down):
    g, u = jnp.split(x @ w_gu, 2, axis=-1)
    return (jax.nn.silu(g) * u) @ w_down


def moe_swiglu(x, w_router, b_router, w_gu, w_down):
    b, L, d = x.shape
    n_tok = b * L
    xt = x.reshape(n_tok, d)
    logits = (xt @ w_router).astype(jnp.float32) + b_router.astype(jnp.float32)
    top_val, top_idx = lax.top_k(logits, MOE_TOPK)
    gates = jax.nn.softmax(top_val, axis=-1)
    n_assign = n_tok * MOE_TOPK
    per_expert = max(1, n_assign // N_EXPERTS)
    blk = min(MOE_BLOCK, max(8, 1 << (per_expert.bit_length() - 1)))
    n_blocks = -(-n_assign // blk) + N_EXPERTS
    flat_e = top_idx.reshape(n_assign).astype(jnp.int32)
    flat_tok = jnp.repeat(jnp.arange(n_tok, dtype=jnp.int32), MOE_TOPK)
    flat_w = gates.reshape(n_assign)
    order = jnp.argsort(flat_e)
    se, st, sw = flat_e[order], flat_tok[order], flat_w[order]
    counts = jax.ops.segment_sum(jnp.ones((n_assign,), jnp.int32), flat_e, num_segments=N_EXPERTS)
    padded = (counts + blk - 1) // blk * blk
    pad_end = jnp.cumsum(padded)
    pad_start = pad_end - padded
    start = jnp.cumsum(counts) - counts
    dest = pad_start[se] + jnp.arange(n_assign, dtype=jnp.int32) - start[se]
    buf_tok = jnp.zeros((n_blocks * blk,), jnp.int32).at[dest].set(st)
    buf_w = jnp.zeros((n_blocks * blk,), jnp.float32).at[dest].set(sw)
    blk_e = jnp.minimum(jnp.searchsorted(pad_end, jnp.arange(n_blocks, dtype=jnp.int32) * blk, side='right'),
                        N_EXPERTS - 1)

    def expert_block(args):
        tok, wt, e = args
        g, u = jnp.split(xt[tok] @ w_gu[e], 2, axis=-1)
        y = (jax.nn.silu(g) * u) @ w_down[e]
        return y * wt[:, None].astype(y.dtype)

    ys = lax.map(expert_block, (buf_tok.reshape(n_blocks, blk), buf_w.reshape(n_blocks, blk), blk_e))
    out = jnp.zeros((n_tok, d), x.dtype).at[buf_tok].add(ys.reshape(n_blocks * blk, d).astype(x.dtype))
    return out.reshape(b, L, d)


def trunk(x, p, conv_st, ssm_st, k_past, v_past, w):
    b, L, _ = x.shape
    pos0 = k_past.shape[1]
    pos = pos0 + jnp.arange(L, dtype=jnp.int32)
    new_ssm, new_conv = [], []
    shared = None
    k_new = None
    v_new = None
    for i in range(DEPTH):
        if i < N_A_LAYERS:
            mix, h_t, c_t = mamba2_mixer(x, conv_st[i], ssm_st[i], w['ssm_w_in'][i], w['ssm_conv_w'][i],
                                         w['ssm_conv_b'][i], w['ssm_dt_bias'][i], w['ssm_A_log'][i],
                                         w['ssm_D'][i], w['ssm_norm_w'][i], w['ssm_w_out'][i])
            new_ssm.append(h_t)
            new_conv.append(c_t)
        else:
            if shared is None:
                kv = (x @ w['attn_w_kv']).reshape(b, L, 2, ATTN_N_KV_HEADS, ATTN_HEAD_DIM)
                k_new = rope_partial(kv[:, :, 0], pos)
                v_new = kv[:, :, 1]
                shared = moba_prepare(jnp.concatenate([k_past.astype(x.dtype), k_new], axis=1),
                                      jnp.concatenate([v_past.astype(x.dtype), v_new], axis=1))
            j = i - N_A_LAYERS
            q = rope_partial((x @ w['attn_w_q'][j]).reshape(b, L, ATTN_N_HEADS, ATTN_HEAD_DIM), pos)
            att = moba_attend(q, shared[0], shared[1], shared[2], pos0)
            mix = att.reshape(b, L, ATTN_N_HEADS * ATTN_HEAD_DIM) @ w['attn_w_o'][j]
        x = layernorm(DEEPNORM_ALPHA * x + mix, w['ln_g'][i, 0], w['ln_b'][i, 0])
        if i % 2 == 0:
            ff = swiglu(x, w['ffn_w_gu'][i // 2], w['ffn_w_down'][i // 2])
        else:
            ff = moe_swiglu(x, w['moe_w_router'][i // 2], w['moe_b_router'][i // 2],
                            w['moe_w_gu'][i // 2], w['moe_w_down'][i // 2])
        x = layernorm(DEEPNORM_ALPHA * x + ff, w['ln_g'][i, 1], w['ln_b'][i, 1])
        gate = jax.nn.sigmoid((x @ w['ple_w_gate'][i]).astype(jnp.float32) + w['ple_b_gate'][i].astype(jnp.float32))
        x = x + (gate * (p[i] @ w['ple_w_proj'][i]).astype(jnp.float32)).astype(x.dtype)
    return x, jnp.stack(new_ssm), jnp.stack(new_conv), k_new, v_new


def setup_inputs(seed: int = 0) -> dict:
    key = jax.random.key(seed)
    ks = jax.random.split(key, 32)

    def nrm(k, shape, scale=1.0):
        return scale * jax.random.normal(k, shape, jnp.float32)

    n_pages = PAST_LEN // PAGE_SIZE
    n_pool = (5 * DEC_BATCH * n_pages + 3) // 4
    perm = jax.random.permutation(ks[6], n_pool).astype(jnp.int32)
    page_table = perm[:DEC_BATCH * n_pages].reshape(DEC_BATCH, n_pages)
    dt0 = jnp.exp(jax.random.uniform(ks[13], (N_A_LAYERS, SSM_N_HEADS), jnp.float32)
                  * (math.log(DT_MAX) - math.log(DT_MIN)) + math.log(DT_MIN))
    dt_bias = dt0 + jnp.log(-jnp.expm1(-dt0))
    a_log = jnp.log(jax.random.uniform(ks[14], (N_A_LAYERS, SSM_N_HEADS), jnp.float32, 1.0, 16.0))
    return {
        'x_prompt': nrm(ks[0], (BATCH, SEQ, D_MODEL)),
        'x_sample': nrm(ks[1], (DEC_BATCH, DEC_SEQ, D_MODEL)),
        'state_ssm': nrm(ks[2], (N_A_LAYERS, DEC_BATCH, SSM_N_HEADS, SSM_HEAD_DIM, SSM_D_STATE), 0.1),
        'state_conv': nrm(ks[3], (N_A_LAYERS, DEC_BATCH, SSM_CONV_W - 1, SSM_CONV_DIM)),
        'cache_k': nrm(ks[4], (n_pool, PAGE_SIZE, ATTN_N_KV_HEADS, ATTN_HEAD_DIM)),
        'cache_v': nrm(ks[5], (n_pool, PAGE_SIZE, ATTN_N_KV_HEADS, ATTN_HEAD_DIM)),
        'page_table': page_table,
        'p_prompt': nrm(ks[7], (DEPTH, BATCH, SEQ, D_PLE)),
        'p_sample': nrm(ks[8], (DEPTH, DEC_BATCH, DEC_SEQ, D_PLE)),
        'ln_g': 1.0 + nrm(ks[9], (DEPTH, 2, D_MODEL), 0.05),
        'ln_b': nrm(ks[10], (DEPTH, 2, D_MODEL), 0.02),
        'ssm_w_in': nrm(ks[11], (N_A_LAYERS, D_MODEL, SSM_IN_DIM), D_MODEL ** -0.5),
        'ssm_conv_w': nrm(ks[12], (N_A_LAYERS, SSM_CONV_W, SSM_CONV_DIM), SSM_CONV_W ** -0.5),
        'ssm_conv_b': nrm(ks[15], (N_A_LAYERS, SSM_CONV_DIM), 0.02),
        'ssm_dt_bias': dt_bias,
        'ssm_A_log': a_log,
        'ssm_D': 1.0 + nrm(ks[16], (N_A_LAYERS, SSM_N_HEADS), 0.1),
        'ssm_norm_w': 1.0 + nrm(ks[17], (N_A_LAYERS, SSM_D_INNER), 0.1),
        'ssm_w_out': nrm(ks[18], (N_A_LAYERS, SSM_D_INNER, D_MODEL), SSM_D_INNER ** -0.5 * DEEPNORM_BETA),
        'attn_w_kv': nrm(ks[19], (D_MODEL, 2 * ATTN_N_KV_HEADS * ATTN_HEAD_DIM), D_MODEL ** -0.5),
        'attn_w_q': nrm(ks[20], (N_B_LAYERS, D_MODEL, ATTN_N_HEADS * ATTN_HEAD_DIM), D_MODEL ** -0.5),
        'attn_w_o': nrm(ks[21], (N_B_LAYERS, ATTN_N_HEADS * ATTN_HEAD_DIM, D_MODEL),
                        (ATTN_N_HEADS * ATTN_HEAD_DIM) ** -0.5 * DEEPNORM_BETA),
        'ffn_w_gu': nrm(ks[22], (N_DENSE_LAYERS, D_MODEL, 2 * D_FF), D_MODEL ** -0.5),
        'ffn_w_down': nrm(ks[23], (N_DENSE_LAYERS, D_FF, D_MODEL), D_FF ** -0.5 * DEEPNORM_BETA),
        'moe_w_router': nrm(ks[24], (N_MOE_LAYERS, D_MODEL, N_EXPERTS), D_MODEL ** -0.5),
        'moe_b_router': nrm(ks[25], (N_MOE_LAYERS, N_EXPERTS), 0.01),
        'moe_w_gu': nrm(ks[26], (N_MOE_LAYERS, N_EXPERTS, D_MODEL, 2 * D_FF_EXPERT), D_MODEL ** -0.5),
        'moe_w_down': nrm(ks[27], (N_MOE_LAYERS, N_EXPERTS, D_FF_EXPERT, D_MODEL),
                          D_FF_EXPERT ** -0.5 * DEEPNORM_BETA),
        'ple_w_gate': nrm(ks[28], (DEPTH, D_MODEL, D_MODEL), D_MODEL ** -0.5),
        'ple_b_gate': nrm(ks[29], (DEPTH, D_MODEL), 0.02),
        'ple_w_proj': nrm(ks[30], (DEPTH, D_PLE, D_MODEL), D_PLE ** -0.5),
    }


def reference(x_prompt, x_sample, state_ssm, state_conv, cache_k, cache_v, page_table, p_prompt, p_sample,
              ln_g, ln_b, ssm_w_in, ssm_conv_w, ssm_conv_b, ssm_dt_bias, ssm_A_log, ssm_D, ssm_norm_w,
              ssm_w_out, attn_w_kv, attn_w_q, attn_w_o, ffn_w_gu, ffn_w_down, moe_w_router, moe_b_router,
              moe_w_gu, moe_w_down, ple_w_gate, ple_b_gate, ple_w_proj):
    w = dict(ln_g=ln_g, ln_b=ln_b, ssm_w_in=ssm_w_in, ssm_conv_w=ssm_conv_w, ssm_conv_b=ssm_conv_b,
             ssm_dt_bias=ssm_dt_bias, ssm_A_log=ssm_A_log, ssm_D=ssm_D, ssm_norm_w=ssm_norm_w,
             ssm_w_out=ssm_w_out, attn_w_kv=attn_w_kv, attn_w_q=attn_w_q, attn_w_o=attn_w_o,
             ffn_w_gu=ffn_w_gu, ffn_w_down=ffn_w_down, moe_w_router=moe_w_router, moe_b_router=moe_b_router,
             moe_w_gu=moe_w_gu, moe_w_down=moe_w_down, ple_w_gate=ple_w_gate, ple_b_gate=ple_b_gate,
             ple_w_proj=ple_w_proj)
    bp = x_prompt.shape[0]
    dtp = x_prompt.dtype
    conv0 = jnp.zeros((N_A_LAYERS, bp, SSM_CONV_W - 1, SSM_CONV_DIM), dtp)
    ssm0 = jnp.zeros((N_A_LAYERS, bp, SSM_N_HEADS, SSM_HEAD_DIM, SSM_D_STATE), dtp)
    kv0 = jnp.zeros((bp, 0, ATTN_N_KV_HEADS, ATTN_HEAD_DIM), dtp)
    y_prompt, ssm_prompt, conv_prompt, k_prompt, v_prompt = trunk(x_prompt, p_prompt, conv0, ssm0, kv0, kv0, w)
    db, n_pages = page_table.shape
    past = n_pages * cache_k.shape[1]
    k_past = cache_k[page_table].reshape(db, past, ATTN_N_KV_HEADS, ATTN_HEAD_DIM)
    v_past = cache_v[page_table].reshape(db, past, ATTN_N_KV_HEADS, ATTN_HEAD_DIM)
    y_sample, ssm_sample, conv_sample, k_sample, v_sample = trunk(x_sample, p_sample, state_conv, state_ssm,
                                                                  k_past, v_past, w)
    return (y_prompt, y_sample, ssm_prompt, conv_prompt, k_prompt, v_prompt,
            ssm_sample, conv_sample, k_sample, v_sample)
```

```python
import functools

import jax
import jax.numpy as jnp
from jax import lax
from jax.experimental import pallas as pl
from jax.experimental.pallas import tpu as pltpu

F32 = jnp.float32
BF16 = jnp.bfloat16
I32 = jnp.int32

D_MODEL = 2048
SEQ = 8192
DEC_BATCH = 32
PAST_LEN = 8192
PAGE_SIZE = 128

SSM_D_INNER = 4096
SSM_HEAD_DIM = 64
SSM_N_HEADS = 64
SSM_N_GROUPS = 8
SSM_D_STATE = 128
SSM_CONV_W = 4
SSM_CHUNK = 128
SSM_CONV_DIM = 6144
SSM_HEADS_PER_GROUP = SSM_N_HEADS // SSM_N_GROUPS

ATTN_HEAD_DIM = 128
ATTN_N_HEADS = 16
ATTN_N_KV_HEADS = 4
ATTN_GROUP = 4
ROT_DIM = 32
ROPE_THETA = 500000.0
MOBA_BLOCK = 256
MOBA_TOPK = 3
N_PAST_BLOCKS = PAST_LEN // MOBA_BLOCK
PAGES_PER_BLOCK = MOBA_BLOCK // PAGE_SIZE
N_PAGES = PAST_LEN // PAGE_SIZE

D_FF = 5632
N_EXPERTS = 8
MOE_TOPK = 2
D_FF_EXPERT = 7168
D_PLE = 256
DEEPNORM_ALPHA = 4.0 ** 0.25
LN_EPS = 1e-5
RMS_EPS = 1e-5

LANES = 128
VMEM_LIMIT_BYTES = 56 * 1024 * 1024

N_TOK = SEQ + DEC_BATCH
TOK_PAD = 8448
MM_ROWS = 768
ROW_BLOCK = 256
MOE_ROWS = 256
N_ASSIGN = N_TOK * MOE_TOPK
MOE_BLOCKS = -(-N_ASSIGN // MOE_ROWS) + N_EXPERTS
MOE_BUF = MOE_BLOCKS * MOE_ROWS
NEG_INF = float("-inf")


def _params(*sem):
    return pltpu.CompilerParams(dimension_semantics=sem, vmem_limit_bytes=VMEM_LIMIT_BYTES)


def _split3(a):
    hi = a.astype(BF16)
    r1 = a - hi.astype(F32)
    mid = r1.astype(BF16)
    lo = (r1 - mid.astype(F32)).astype(BF16)
    return hi, mid, lo


def _dot_f32(a, b, dims):
    a0, a1, a2 = _split3(a)
    b0, b1, b2 = _split3(b)
    dn = (dims, ((), ()))
    d = lambda x, y: lax.dot_general(x, y, dn, preferred_element_type=F32)
    return (d(a0, b0) + (d(a0, b1) + d(a1, b0))) + ((d(a0, b2) + d(a2, b0)) + d(a1, b1))


def _softplus(x):
    return jnp.maximum(x, 0.0) + jnp.log1p(jnp.exp(-jnp.abs(x)))


def _silu(x):
    return x * jax.nn.sigmoid(x)


def _rope_lanes(a, cf, s1, s2):
    return a * cf + pltpu.roll(a, LANES - ROT_DIM // 2, 1) * s1 + pltpu.roll(a, ROT_DIM // 2, 1) * s2


def _mm_kernel(x_ref, w_ref, *rest, rope):
    if rope:
        cf_ref, s1_ref, s2_ref, o_ref, wb_ref = rest
    else:
        o_ref, wb_ref = rest

    @pl.when(pl.program_id(1) == 0)
    def _():
        wb_ref[...] = w_ref[...].astype(BF16)

    acc = jnp.dot(x_ref[...], wb_ref[...], preferred_element_type=F32)
    if rope:
        cf, s1, s2 = cf_ref[...], s1_ref[...], s2_ref[...]
        for c in range(acc.shape[1] // LANES):
            sl = slice(c * LANES, (c + 1) * LANES)
            o_ref[:, sl] = _rope_lanes(acc[:, sl], cf, s1, s2).astype(o_ref.dtype)
    else:
        o_ref[...] = acc.astype(o_ref.dtype)


def matmul(x, w, *, bn, col0=0, ncols=None, rope=None, out_dtype=F32, bm=MM_ROWS):
    m, k = x.shape
    ncols = w.shape[1] if ncols is None else ncols
    assert m % bm == 0 and ncols % bn == 0 and col0 % bn == 0
    j0 = col0 // bn
    in_specs = [pl.BlockSpec((bm, k), lambda j, i: (i, 0)),
                pl.BlockSpec((k, bn), lambda j, i: (0, j + j0))]
    args = [x, w]
    if rope is not None:
        in_specs += [pl.BlockSpec((bm, LANES), lambda j, i: (i, 0))] * 3
        args += list(rope)
    return pl.pallas_call(
        functools.partial(_mm_kernel, rope=rope is not None),
        grid=(ncols // bn, m // bm),
        in_specs=in_specs,
        out_specs=pl.BlockSpec((bm, bn), lambda j, i: (i, j)),
        out_shape=jax.ShapeDtypeStruct((m, ncols), out_dtype),
        scratch_shapes=[pltpu.VMEM((k, bn), BF16)],
        compiler_params=_params("arbitrary", "arbitrary"),
        name="matmul",
    )(*args)


def _mm_swiglu_kernel(x_ref, wg_ref, wu_ref, o_ref, wgb_ref, wub_ref):
    @pl.when(pl.program_id(1) == 0)
    def _():
        wgb_ref[...] = wg_ref[...].astype(BF16)
        wub_ref[...] = wu_ref[...].astype(BF16)

    x = x_ref[...]
    g = jnp.dot(x, wgb_ref[...], preferred_element_type=F32)
    u = jnp.dot(x, wub_ref[...], preferred_element_type=F32)
    o_ref[...] = (_silu(g) * u).astype(o_ref.dtype)


def matmul_swiglu(x, w_gu, *, bn, bm=MM_ROWS):
    m, k = x.shape
    f = w_gu.shape[1] // 2
    assert m % bm == 0 and f % bn == 0
    nj = f // bn
    return pl.pallas_call(
        _mm_swiglu_kernel,
        grid=(nj, m // bm),
        in_specs=[pl.BlockSpec((bm, k), lambda j, i: (i, 0)),
                  pl.BlockSpec((k, bn), lambda j, i: (0, j)),
                  pl.BlockSpec((k, bn), lambda j, i: (0, j + nj))],
        out_specs=pl.BlockSpec((bm, bn), lambda j, i: (i, j)),
        out_shape=jax.ShapeDtypeStruct((m, f), BF16),
        scratch_shapes=[pltpu.VMEM((k, bn), BF16), pltpu.VMEM((k, bn), BF16)],
        compiler_params=_params("arbitrary", "arbitrary"),
        name="matmul_swiglu",
    )(x, w_gu, w_gu)


def _ple_kernel(xb_ref, pb_ref, wg_ref, wp_ref, bg_ref, x_ref, o_ref, ob_ref, wgb_ref, wpb_ref):
    @pl.when(pl.program_id(1) == 0)
    def _():
        wgb_ref[...] = wg_ref[...].astype(BF16)
        wpb_ref[...] = wp_ref[...].astype(BF16)

    gate = jax.nn.sigmoid(jnp.dot(xb_ref[...], wgb_ref[...], preferred_element_type=F32) + bg_ref[...])
    proj = jnp.dot(pb_ref[...], wpb_ref[...], preferred_element_type=F32)
    out = x_ref[...] + gate * proj
    o_ref[...] = out
    ob_ref[...] = out.astype(BF16)


def ple_mix(x, xb, pb, w_gate, b_gate, w_proj, *, bn=512, bm=MM_ROWS):
    m, d = x.shape
    kp = pb.shape[1]
    return pl.pallas_call(
        _ple_kernel,
        grid=(d // bn, m // bm),
        in_specs=[pl.BlockSpec((bm, d), lambda j, i: (i, 0)),
                  pl.BlockSpec((bm, kp), lambda j, i: (i, 0)),
                  pl.BlockSpec((d, bn), lambda j, i: (0, j)),
                  pl.BlockSpec((kp, bn), lambda j, i: (0, j)),
                  pl.BlockSpec((1, bn), lambda j, i: (0, j)),
                  pl.BlockSpec((bm, bn), lambda j, i: (i, j))],
        out_specs=[pl.BlockSpec((bm, bn), lambda j, i: (i, j)),
                   pl.BlockSpec((bm, bn), lambda j, i: (i, j))],
        out_shape=[jax.ShapeDtypeStruct((m, d), F32), jax.ShapeDtypeStruct((m, d), BF16)],
        scratch_shapes=[pltpu.VMEM((d, bn), BF16), pltpu.VMEM((kp, bn), BF16)],
        compiler_params=_params("arbitrary", "arbitrary"),
        name="ple_mix",
    )(xb, pb, w_gate, w_proj, b_gate.reshape(1, d), x)


def _layernorm_rows(v, g, b):
    mu = jnp.mean(v, axis=-1, keepdims=True)
    c = v - mu
    var = jnp.mean(c * c, axis=-1, keepdims=True)
    return c * lax.rsqrt(var + LN_EPS) * g + b


def _add_ln_kernel(x_ref, m_ref, g_ref, b_ref, o_ref, ob_ref):
    out = _layernorm_rows(DEEPNORM_ALPHA * x_ref[...] + m_ref[...], g_ref[...], b_ref[...])
    o_ref[...] = out
    ob_ref[...] = out.astype(BF16)


def add_ln(x, mix, g, b):
    m, d = x.shape
    row = pl.BlockSpec((ROW_BLOCK, d), lambda i: (i, 0))
    vec = pl.BlockSpec((1, d), lambda i: (0, 0))
    return pl.pallas_call(
        _add_ln_kernel,
        grid=(m // ROW_BLOCK,),
        in_specs=[row, row, vec, vec],
        out_specs=[row, row],
        out_shape=[jax.ShapeDtypeStruct((m, d), F32), jax.ShapeDtypeStruct((m, d), BF16)],
        compiler_params=_params("arbitrary"),
        name="add_ln",
    )(x, mix, g.reshape(1, d), b.reshape(1, d))


def _gate_norm_kernel(y_ref, xs_ref, z_ref, d_ref, w_ref, o_ref):
    g = (y_ref[...] + d_ref[...] * xs_ref[...]) * _silu(z_ref[...])
    ms = jnp.mean(g * g, axis=-1, keepdims=True)
    o_ref[...] = (g * lax.rsqrt(ms + RMS_EPS) * w_ref[...]).astype(o_ref.dtype)


def gate_norm(y, xbc, zx, d_lanes, norm_w):
    m, di = y.shape
    row = pl.BlockSpec((ROW_BLOCK, di), lambda i: (i, 0))
    vec = pl.BlockSpec((1, di), lambda i: (0, 0))
    return pl.pallas_call(
        _gate_norm_kernel,
        grid=(m // ROW_BLOCK,),
        in_specs=[row, row, row, vec, vec],
        out_specs=row,
        out_shape=jax.ShapeDtypeStruct((m, di), BF16),
        compiler_params=_params("arbitrary"),
        name="gate_norm",
    )(y, xbc, zx, d_lanes, norm_w.reshape(1, di))


CONV_COLS = 2048


def _conv_prompt_kernel(x_ref, h_ref, w_ref, b_ref, o_ref):
    @pl.when(pl.program_id(0) < SEQ // ROW_BLOCK)
    def _():
        _conv_prompt_rows(x_ref, h_ref, w_ref, b_ref, o_ref)

    @pl.when(pl.program_id(0) >= SEQ // ROW_BLOCK)
    def _():
        o_ref[...] = jnp.zeros(o_ref.shape, o_ref.dtype)


def _conv_prompt_rows(x_ref, h_ref, w_ref, b_ref, o_ref):
    x = x_ref[...]
    halo = jnp.where(pl.program_id(0) == 0, 0.0, h_ref[...])
    w = w_ref[...]
    acc = b_ref[...] + x * w[SSM_CONV_W - 1:SSM_CONV_W, :]
    rows8 = lax.broadcasted_iota(I32, halo.shape, 0)
    for s in range(1, SSM_CONV_W):
        xr = pltpu.roll(x, s, 0)
        head = jnp.where(rows8 < s, pltpu.roll(halo, s, 0), xr[:8, :])
        shifted = jnp.concatenate([head, xr[8:, :]], axis=0)
        acc = acc + shifted * w[SSM_CONV_W - 1 - s:SSM_CONV_W - s, :]
    o_ref[...] = _silu(acc)


def conv_prompt(zx, conv_w, conv_b):
    c0 = SSM_D_INNER // CONV_COLS
    rb8 = ROW_BLOCK // 8
    return pl.pallas_call(
        _conv_prompt_kernel,
        grid=(TOK_PAD // ROW_BLOCK, SSM_CONV_DIM // CONV_COLS),
        in_specs=[pl.BlockSpec((ROW_BLOCK, CONV_COLS), lambda i, k: (i, c0 + k)),
                  pl.BlockSpec((8, CONV_COLS), lambda i, k: (jnp.maximum(i * rb8 - 1, 0), c0 + k)),
                  pl.BlockSpec((SSM_CONV_W, CONV_COLS), lambda i, k: (0, k)),
                  pl.BlockSpec((1, CONV_COLS), lambda i, k: (0, k))],
        out_specs=pl.BlockSpec((ROW_BLOCK, CONV_COLS), lambda i, k: (i, k)),
        out_shape=jax.ShapeDtypeStruct((TOK_PAD, SSM_CONV_DIM), F32),
        compiler_params=_params("arbitrary", "arbitrary"),
        name="conv_prompt",
    )(zx, zx, conv_w, conv_b.reshape(1, SSM_CONV_DIM))


def _conv_sample_kernel(st_ref, x_ref, w_ref, b_ref, o_ref):
    w = w_ref[...]
    acc = b_ref[...] + x_ref[...] * w[SSM_CONV_W - 1:SSM_CONV_W, :]
    for j in range(SSM_CONV_W - 1):
        acc = acc + st_ref[j] * w[j:j + 1, :]
    o_ref[...] = jnp.zeros(o_ref.shape, F32)
    o_ref[:DEC_BATCH, :] = _silu(acc)


def conv_sample(st, x_raw, conv_w, conv_b):
    return pl.pallas_call(
        _conv_sample_kernel,
        grid=(SSM_CONV_DIM // CONV_COLS,),
        in_specs=[pl.BlockSpec((SSM_CONV_W - 1, DEC_BATCH, CONV_COLS), lambda k: (0, 0, k)),
                  pl.BlockSpec((DEC_BATCH, CONV_COLS), lambda k: (0, k)),
                  pl.BlockSpec((SSM_CONV_W, CONV_COLS), lambda k: (0, k)),
                  pl.BlockSpec((1, CONV_COLS), lambda k: (0, k))],
        out_specs=pl.BlockSpec((TOK_PAD - SEQ, CONV_COLS), lambda k: (0, k)),
        out_shape=jax.ShapeDtypeStruct((TOK_PAD - SEQ, SSM_CONV_DIM), F32),
        compiler_params=_params("arbitrary"),
        name="conv_sample",
    )(st, x_raw, conv_w, conv_b.reshape(1, SSM_CONV_DIM))


SSD_GROUP_COLS = SSM_HEADS_PER_GROUP * SSM_HEAD_DIM
SSD_PAIR_ROWS = 2 * SSM_HEAD_DIM


def _ssd_prompt_kernel(xs_ref, b_ref, c_ref, dtc_ref, dtr_ref, bc_ref, br_ref, ac_ref, ar_ref,
                       y_ref, h_ref):
    @pl.when(pl.program_id(0) < SEQ // SSM_CHUNK)
    def _():
        _ssd_prompt_chunk(xs_ref, b_ref, c_ref, dtc_ref, dtr_ref, bc_ref, br_ref, ac_ref, ar_ref, y_ref, h_ref)

    @pl.when(pl.program_id(0) >= SEQ // SSM_CHUNK)
    def _():
        y_ref[...] = jnp.zeros(y_ref.shape, y_ref.dtype)


def _ssd_prompt_chunk(xs_ref, b_ref, c_ref, dtc_ref, dtr_ref, bc_ref, br_ref, ac_ref, ar_ref,
                      y_ref, h_ref):
    c = pl.program_id(0)
    g = pl.program_id(1)
    q = SSM_CHUNK
    dt_c = _softplus(dtc_ref[...] + bc_ref[...])
    dt_r = _softplus(dtr_ref[...] + br_ref[...])
    da_c = dt_c * -jnp.exp(ac_ref[...])
    da_r = dt_r * -jnp.exp(ar_ref[...])
    ti = lax.broadcasted_iota(I32, (q, q), 0)
    si = lax.broadcasted_iota(I32, (q, q), 1)
    causal = ti >= si
    tril = causal.astype(F32)
    triu = (ti <= si).astype(F32)
    acs_c = _dot_f32(tril, da_c, ((1,), (0,)))
    acs_r = _dot_f32(da_r, triu, ((1,), (0,)))
    last = acs_c[q - 1:q, :]

    @pl.when(c == 0)
    def _():
        grp_rows = pl.ds(pl.multiple_of(g * SSD_GROUP_COLS, SSD_GROUP_COLS), SSD_GROUP_COLS)
        h_ref[grp_rows, :] = jnp.zeros((SSD_GROUP_COLS, SSM_D_STATE), F32)

    bm = b_ref[...].astype(BF16)
    cm = c_ref[...].astype(BF16)
    cb = lax.dot_general(cm, bm, (((1,), (1,)), ((), ())), preferred_element_type=F32)

    lane = lax.broadcasted_iota(I32, (q, LANES), 1)
    lo = lane < SSM_HEAD_DIM
    sub = lax.broadcasted_iota(I32, (SSD_PAIR_ROWS, 1), 0)
    for pair in range(SSM_HEADS_PER_GROUP // 2):
        r1, r2 = 2 * pair, 2 * pair + 1
        x = xs_ref[:, pair * LANES:(pair + 1) * LANES]
        xdt = x * jnp.where(lo, dt_c[:, r1:r1 + 1], dt_c[:, r2:r2 + 1])
        ms = []
        for r in (r1, r2):
            seg = acs_c[:, r:r + 1] - acs_r[r:r + 1, :]
            ms.append((cb * jnp.exp(jnp.where(causal, seg, NEG_INF))).astype(BF16))
        y = (jnp.dot(ms[0], jnp.where(lo, xdt, 0.0).astype(BF16), preferred_element_type=F32)
             + jnp.dot(ms[1], jnp.where(lo, 0.0, xdt).astype(BF16), preferred_element_type=F32))

        rows = pl.ds(pl.multiple_of(g * SSD_GROUP_COLS + pair * SSD_PAIR_ROWS, SSD_PAIR_ROWS), SSD_PAIR_ROWS)
        h_in = h_ref[rows, :]
        y_off = lax.dot_general(cm, h_in.astype(BF16), (((1,), (1,)), ((), ())), preferred_element_type=F32)
        y = y + y_off * jnp.exp(jnp.where(lo, acs_c[:, r1:r1 + 1], acs_c[:, r2:r2 + 1]))
        y_ref[:, pair * LANES:(pair + 1) * LANES] = y

        to_end = jnp.exp(jnp.where(lo, last[:, r1:r1 + 1] - acs_c[:, r1:r1 + 1],
                                   last[:, r2:r2 + 1] - acs_c[:, r2:r2 + 1]))
        xw_t = (xdt * to_end).T.astype(BF16)
        st = jnp.dot(xw_t, bm, preferred_element_type=F32)
        decay = jnp.exp(jnp.where(sub < SSM_HEAD_DIM, last[:, r1:r1 + 1], last[:, r2:r2 + 1]))
        h_ref[rows, :] = h_in * decay + st


def ssd_prompt(xbc, dt_raw, dt_bias, a_log):
    nc = SEQ // SSM_CHUNK
    g_, r_ = SSM_N_GROUPS, SSM_HEADS_PER_GROUP
    dt3 = dt_raw.reshape(SEQ, g_, r_)
    dt_col = jnp.transpose(dt3, (1, 0, 2))
    dt_row = jnp.transpose(dt3, (1, 2, 0))
    xcol0 = 0
    bcol0 = SSM_D_INNER // SSM_D_STATE
    ccol0 = bcol0 + SSM_N_GROUPS
    col_vec = pl.BlockSpec((None, 1, r_), lambda c, g: (g, 0, 0))
    row_vec = pl.BlockSpec((None, r_, 1), lambda c, g: (g, 0, 0))
    return pl.pallas_call(
        _ssd_prompt_kernel,
        grid=(TOK_PAD // SSM_CHUNK, g_),
        in_specs=[pl.BlockSpec((SSM_CHUNK, SSD_GROUP_COLS), lambda c, g: (c, xcol0 + g)),
                  pl.BlockSpec((SSM_CHUNK, SSM_D_STATE), lambda c, g: (c, bcol0 + g)),
                  pl.BlockSpec((SSM_CHUNK, SSM_D_STATE), lambda c, g: (c, ccol0 + g)),
                  pl.BlockSpec((None, SSM_CHUNK, r_), lambda c, g: (g, jnp.minimum(c, nc - 1), 0)),
                  pl.BlockSpec((None, r_, SSM_CHUNK), lambda c, g: (g, 0, jnp.minimum(c, nc - 1))),
                  col_vec, row_vec, col_vec, row_vec],
        out_specs=[pl.BlockSpec((SSM_CHUNK, SSD_GROUP_COLS), lambda c, g: (c, g)),
                   pl.BlockSpec((SSM_N_HEADS * SSM_HEAD_DIM, SSM_D_STATE), lambda c, g: (0, 0))],
        out_shape=[jax.ShapeDtypeStruct((TOK_PAD, SSM_D_INNER), F32),
                   jax.ShapeDtypeStruct((SSM_N_HEADS * SSM_HEAD_DIM, SSM_D_STATE), F32)],
        compiler_params=_params("arbitrary", "arbitrary"),
        name="ssd_prompt",
    )(xbc, xbc, xbc, dt_col, dt_row,
      dt_bias.reshape(g_, 1, r_), dt_bias.reshape(g_, r_, 1),
      a_log.reshape(g_, 1, r_), a_log.reshape(g_, r_, 1))


def _ssd_sample_kernel(h0_ref, xt_ref, b_ref, c_ref, dt_ref, bias_ref, alog_ref, h_ref, yt_ref):
    dt = _softplus(dt_ref[...] + bias_ref[...])
    dec = jnp.exp(dt * -jnp.exp(alog_ref[...]))
    xdt_t = xt_ref[...] * dt
    bm = b_ref[...]
    cm = c_ref[...]
    for h in range(SSM_N_HEADS):
        grp = h // SSM_HEADS_PER_GROUP
        rows = slice(h * SSM_HEAD_DIM, (h + 1) * SSM_HEAD_DIM)
        hn = h0_ref[rows, :] * dec[:, h:h + 1] + xdt_t[:, h:h + 1] * bm[grp:grp + 1, :]
        h_ref[rows, :] = hn
        yt_ref[:, h:h + 1] = jnp.sum(hn * cm[grp:grp + 1, :], axis=1, keepdims=True)


def ssd_sample(h0, x_t, bm, cm, dt_raw, dt_bias, a_log):
    nb = h0.shape[0]
    hp = SSM_N_HEADS * SSM_HEAD_DIM
    per_b = lambda *shape: pl.BlockSpec((None,) + shape, lambda b: (b,) + (0,) * len(shape))
    vec = pl.BlockSpec((1, SSM_N_HEADS), lambda b: (0, 0))
    return pl.pallas_call(
        _ssd_sample_kernel,
        grid=(nb,),
        in_specs=[per_b(hp, SSM_D_STATE), per_b(SSM_HEAD_DIM, SSM_N_HEADS),
                  per_b(SSM_N_GROUPS, SSM_D_STATE), per_b(SSM_N_GROUPS, SSM_D_STATE),
                  per_b(1, SSM_N_HEADS), vec, vec],
        out_specs=[per_b(hp, SSM_D_STATE), per_b(SSM_HEAD_DIM, SSM_N_HEADS)],
        out_shape=[jax.ShapeDtypeStruct((nb, hp, SSM_D_STATE), F32),
                   jax.ShapeDtypeStruct((nb, SSM_HEAD_DIM, SSM_N_HEADS), F32)],
        compiler_params=_params("arbitrary"),
        name="ssd_sample",
    )(h0, x_t, bm, cm, dt_raw, dt_bias.reshape(1, SSM_N_HEADS), a_log.reshape(1, SSM_N_HEADS))


KV_COLS = ATTN_N_KV_HEADS * ATTN_HEAD_DIM
Q_LANES = ATTN_GROUP * MOBA_BLOCK


def _kmeans_kernel(k_ref, o_ref):
    o_ref[...] = jnp.mean(k_ref[...], axis=0, keepdims=True).reshape(o_ref.shape)


def block_means(k):
    nb = SEQ // MOBA_BLOCK
    return pl.pallas_call(
        _kmeans_kernel,
        grid=(nb,),
        in_specs=[pl.BlockSpec((MOBA_BLOCK, KV_COLS), lambda i: (i, 0))],
        out_specs=pl.BlockSpec((None, 1, KV_COLS), lambda i: (i, 0, 0)),
        out_shape=jax.ShapeDtypeStruct((nb, 1, KV_COLS), F32),
        compiler_params=_params("arbitrary"),
        name="block_means",
    )(k)


def _top3_mask_rows(gate):
    n = gate.shape[0]
    rows = lax.broadcasted_iota(I32, gate.shape, 0)
    sel = jnp.zeros(gate.shape, jnp.bool_)
    for _ in range(MOBA_TOPK):
        m = jnp.max(gate, axis=0, keepdims=True)
        first = jnp.min(jnp.where(gate == m, rows, n), axis=0, keepdims=True)
        hit = jnp.logical_and(rows == first, m > NEG_INF)
        sel = jnp.logical_or(sel, hit)
        gate = jnp.where(rows == first, NEG_INF, gate)
    return sel


def _moba_prompt_kernel(q_ref, k_ref, v_ref, mean_ref, o_ref, bias_ref, m_ref, l_ref, acc_ref):
    i = pl.program_id(0)
    j = pl.program_id(1)
    nb = mean_ref.shape[0]
    scale = ATTN_HEAD_DIM ** -0.5

    def q_group(g):
        return jnp.concatenate(
            [q_ref[:, (ATTN_GROUP * g + r) * ATTN_HEAD_DIM:(ATTN_GROUP * g + r + 1) * ATTN_HEAD_DIM]
             for r in range(ATTN_GROUP)], axis=0)

    @pl.when(jnp.logical_and(i == nb, j == 0))
    def _():
        o_ref[...] = jnp.zeros(o_ref.shape, o_ref.dtype)

    @pl.when(jnp.logical_and(i < nb, j == 0))
    def _():
        blk = lax.broadcasted_iota(I32, (nb, Q_LANES), 0)
        for g in range(ATTN_N_KV_HEADS):
            means = mean_ref[:, g * ATTN_HEAD_DIM:(g + 1) * ATTN_HEAD_DIM]
            gate = _dot_f32(means, q_group(g), ((1,), (1,)))
            sel = _top3_mask_rows(jnp.where(blk < i, gate, NEG_INF))
            bias_ref[g] = jnp.where(sel, 0.0, NEG_INF)
        m_ref[...] = jnp.full(m_ref.shape, NEG_INF, F32)
        l_ref[...] = jnp.zeros(l_ref.shape, F32)
        acc_ref[...] = jnp.zeros(acc_ref.shape, F32)

    @pl.when(jnp.logical_and(i < nb, j <= i))
    def _():
        key = lax.broadcasted_iota(I32, (MOBA_BLOCK, Q_LANES), 0)
        qry = lax.broadcasted_iota(I32, (MOBA_BLOCK, Q_LANES), 1) & (MOBA_BLOCK - 1)
        causal_bias = jnp.where(key <= qry, 0.0, NEG_INF)
        kv_blk = i - j
        for g in range(ATTN_N_KV_HEADS):
            cols = slice(g * ATTN_HEAD_DIM, (g + 1) * ATTN_HEAD_DIM)
            s_t = lax.dot_general(k_ref[:, cols].astype(BF16), q_group(g).astype(BF16),
                                  (((1,), (1,)), ((), ())), preferred_element_type=F32) * scale
            s_t = s_t + jnp.where(j == 0, causal_bias, bias_ref[g, pl.ds(kv_blk, 1), :])
            m_old = m_ref[g]
            m_new = jnp.maximum(m_old, jnp.max(s_t, axis=0, keepdims=True))
            alpha = jnp.exp(m_old - m_new)
            p = jnp.exp(s_t - m_new)
            l_ref[g] = alpha * l_ref[g] + jnp.sum(p, axis=0, keepdims=True)
            m_ref[g] = m_new
            v_t = v_ref[:, cols].T.astype(BF16)
            acc_ref[g] = alpha * acc_ref[g] + jnp.dot(v_t, p.astype(BF16), preferred_element_type=F32)

    @pl.when(j == i)
    def _():
        for g in range(ATTN_N_KV_HEADS):
            out = (acc_ref[g] / l_ref[g]).T
            for r in range(ATTN_GROUP):
                h = ATTN_GROUP * g + r
                o_ref[:, h * ATTN_HEAD_DIM:(h + 1) * ATTN_HEAD_DIM] = (
                    out[r * MOBA_BLOCK:(r + 1) * MOBA_BLOCK, :].astype(o_ref.dtype))


def moba_prompt(q, k, v, means):
    nb = SEQ // MOBA_BLOCK
    kv_spec = pl.BlockSpec((MOBA_BLOCK, KV_COLS), lambda i, j: (jnp.where(i < nb, jnp.maximum(i - j, 0), 0), 0))
    return pl.pallas_call(
        _moba_prompt_kernel,
        grid=(TOK_PAD // MOBA_BLOCK, nb),
        in_specs=[pl.BlockSpec((MOBA_BLOCK, D_MODEL), lambda i, j: (i, 0)),
                  kv_spec, kv_spec,
                  pl.BlockSpec((nb, KV_COLS), lambda i, j: (0, 0))],
        out_specs=pl.BlockSpec((MOBA_BLOCK, D_MODEL), lambda i, j: (i, 0)),
        out_shape=jax.ShapeDtypeStruct((TOK_PAD, D_MODEL), BF16),
        scratch_shapes=[pltpu.VMEM((ATTN_N_KV_HEADS, nb, Q_LANES), F32),
                        pltpu.VMEM((ATTN_N_KV_HEADS, 1, Q_LANES), F32),
                        pltpu.VMEM((ATTN_N_KV_HEADS, 1, Q_LANES), F32),
                        pltpu.VMEM((ATTN_N_KV_HEADS, ATTN_HEAD_DIM, Q_LANES), F32)],
        compiler_params=_params("arbitrary", "arbitrary"),
        name="moba_prompt",
    )(q, k, v, means)


PAGE_ROWS = PAGE_SIZE * ATTN_N_KV_HEADS
SEL_KEYS = MOBA_TOPK * MOBA_BLOCK


def _page_copy(pool_hbm, page, dst, sem):
    return pltpu.make_async_copy(pool_hbm.at[page], dst, sem)


def _sample_select_kernel(pt_ref, q_ref, kc_hbm, o_ref, kbuf, means_ref, sem):
    b = pl.program_id(0)
    for p in range(N_PAGES):
        _page_copy(kc_hbm, pt_ref[b * N_PAGES + p], kbuf.at[p], sem).start()
    for p in range(N_PAGES):
        _page_copy(kc_hbm, pt_ref[b * N_PAGES + p], kbuf.at[p], sem).wait()
    fold = 8 // ATTN_N_KV_HEADS
    for n in range(N_PAST_BLOCKS):
        s = jnp.zeros((8, ATTN_HEAD_DIM), F32)
        for p in range(PAGES_PER_BLOCK):
            s = s + jnp.sum(kbuf[n * PAGES_PER_BLOCK + p].reshape(PAGE_ROWS // 8, 8, ATTN_HEAD_DIM), axis=0)
        tot = s[:ATTN_N_KV_HEADS, :]
        for f in range(1, fold):
            tot = tot + s[f * ATTN_N_KV_HEADS:(f + 1) * ATTN_N_KV_HEADS, :]
        means_ref[n * ATTN_N_KV_HEADS:(n + 1) * ATTN_N_KV_HEADS, :] = tot * (1.0 / MOBA_BLOCK)
    gate = _dot_f32(q_ref[...], means_ref[...], ((1,), (1,)))
    shape = gate.shape
    lane = lax.broadcasted_iota(I32, shape, 1)
    head = lax.broadcasted_iota(I32, shape, 0)
    gate = jnp.where((lane & (ATTN_N_KV_HEADS - 1)) == (head >> 2), gate, NEG_INF)
    out = jnp.zeros(shape, I32)
    for t in range(MOBA_TOPK):
        m = jnp.max(gate, axis=1, keepdims=True)
        first = jnp.min(jnp.where(gate == m, lane, shape[1]), axis=1, keepdims=True)
        out = jnp.where(lane == t, first >> 2, out)
        gate = jnp.where(lane == first, NEG_INF, gate)
    o_ref[...] = out


def sample_select(page_table, q_s, kc_pages):
    nb = q_s.shape[0]
    assert N_PAST_BLOCKS * ATTN_N_KV_HEADS == LANES
    return pl.pallas_call(
        _sample_select_kernel,
        grid_spec=pltpu.PrefetchScalarGridSpec(
            num_scalar_prefetch=1,
            grid=(nb,),
            in_specs=[pl.BlockSpec((None, ATTN_N_HEADS, ATTN_HEAD_DIM), lambda b, pt: (b, 0, 0)),
                      pl.BlockSpec(memory_space=pl.ANY)],
            out_specs=pl.BlockSpec((None, ATTN_N_HEADS, LANES), lambda b, pt: (b, 0, 0)),
            scratch_shapes=[pltpu.VMEM((N_PAGES, PAGE_ROWS, ATTN_HEAD_DIM), F32),
                            pltpu.VMEM((N_PAST_BLOCKS * ATTN_N_KV_HEADS, ATTN_HEAD_DIM), F32),
                            pltpu.SemaphoreType.DMA(())]),
        out_shape=jax.ShapeDtypeStruct((nb, ATTN_N_HEADS, LANES), I32),
        compiler_params=_params("arbitrary"),
        name="sample_select",
    )(page_table.reshape(-1), q_s, kc_pages)


def _head_rows_copy(pool_hbm, page, g, dst, sem):
    return pltpu.make_async_copy(pool_hbm.at[page, :, g, :], dst, sem)


def _sample_attend_kernel(pt_ref, sel_ref, q_ref, kn_ref, vn_ref, kc_hbm, vc_hbm, o_ref, kbuf, vbuf, sems):
    b = pl.program_id(0)
    scale = ATTN_HEAD_DIM ** -0.5

    def copies(h, t, half):
        g = h // ATTN_GROUP
        blk = sel_ref[(b * ATTN_N_HEADS + h) * MOBA_TOPK + t]
        page = pt_ref[b * N_PAGES + blk * PAGES_PER_BLOCK + half]
        rows = pl.ds(t * MOBA_BLOCK + half * PAGE_SIZE, PAGE_SIZE)
        return (_head_rows_copy(kc_hbm, page, g, kbuf.at[h, rows, :], sems.at[0]),
                _head_rows_copy(vc_hbm, page, g, vbuf.at[h, rows, :], sems.at[1]))

    todo = [(h, t, half) for h in range(ATTN_N_HEADS) for t in range(MOBA_TOPK) for half in range(PAGES_PER_BLOCK)]
    for args in todo:
        for cp in copies(*args):
            cp.start()
    for args in todo:
        for cp in copies(*args):
            cp.wait()

    for h in range(ATTN_N_HEADS):
        g = h // ATTN_GROUP
        qh = q_ref[h:h + 1, :]
        q8 = jnp.broadcast_to(qh, (8, ATTN_HEAD_DIM)).astype(BF16)
        s = lax.dot_general(kbuf[h].astype(BF16), q8, (((1,), (1,)), ((), ())),
                            preferred_element_type=F32)[:, 0:1] * scale
        s_own = jnp.sum(qh * kn_ref[g:g + 1, :], axis=1, keepdims=True) * scale
        m = jnp.maximum(jnp.max(s, axis=0, keepdims=True), s_own)
        p = jnp.exp(s - m)
        p_own = jnp.exp(s_own - m)
        denom = jnp.sum(p, axis=0, keepdims=True) + p_own
        num = jnp.sum(p * vbuf[h], axis=0, keepdims=True) + p_own * vn_ref[g:g + 1, :]
        o_ref[h:h + 1, :] = num / denom


def sample_attend(page_table, sel, q_s, k_new, v_new, cache_k, cache_v):
    nb = q_s.shape[0]
    per_b = lambda rows: pl.BlockSpec((None, rows, ATTN_HEAD_DIM), lambda b, pt, sl: (b, 0, 0))
    any_spec = pl.BlockSpec(memory_space=pl.ANY)
    return pl.pallas_call(
        _sample_attend_kernel,
        grid_spec=pltpu.PrefetchScalarGridSpec(
            num_scalar_prefetch=2,
            grid=(nb,),
            in_specs=[per_b(ATTN_N_HEADS), per_b(ATTN_N_KV_HEADS), per_b(ATTN_N_KV_HEADS), any_spec, any_spec],
            out_specs=per_b(ATTN_N_HEADS),
            scratch_shapes=[pltpu.VMEM((ATTN_N_HEADS, SEL_KEYS, ATTN_HEAD_DIM), F32),
                            pltpu.VMEM((ATTN_N_HEADS, SEL_KEYS, ATTN_HEAD_DIM), F32),
                            pltpu.SemaphoreType.DMA((2,))]),
        out_shape=jax.ShapeDtypeStruct((nb, ATTN_N_HEADS, ATTN_HEAD_DIM), F32),
        compiler_params=_params("arbitrary"),
        name="sample_attend",
    )(page_table.reshape(-1), sel.reshape(-1), q_s, k_new, v_new, cache_k, cache_v)


def _router_kernel(x_ref, w_ref, b_ref, idx_ref, gate_ref):
    x0, x1, _ = _split3(x_ref[...])
    w0, w1, _ = _split3(w_ref[...])
    d = lambda a, c: jnp.dot(a, c, preferred_element_type=F32)
    logits = d(x0, w0) + (d(x0, w1) + d(x1, w0)) + b_ref[...]
    lane = lax.broadcasted_iota(I32, logits.shape, 1)
    logits = jnp.where(lane < N_EXPERTS, logits, NEG_INF)
    m1 = jnp.max(logits, axis=1, keepdims=True)
    i1 = jnp.min(jnp.where(logits == m1, lane, LANES), axis=1, keepdims=True)
    rest = jnp.where(lane == i1, NEG_INF, logits)
    m2 = jnp.max(rest, axis=1, keepdims=True)
    i2 = jnp.min(jnp.where(rest == m2, lane, LANES), axis=1, keepdims=True)
    e2 = jnp.exp(m2 - m1)
    g1 = 1.0 / (1.0 + e2)
    idx_ref[...] = jnp.where(lane == 0, i1, jnp.where(lane == 1, i2, 0))
    gate_ref[...] = jnp.where(lane == 0, g1, jnp.where(lane == 1, e2 * g1, 0.0))


def router(x, w_router, b_router):
    m, d = x.shape
    wp = jnp.zeros((d, LANES), F32).at[:, :N_EXPERTS].set(w_router)
    bp = jnp.zeros((1, LANES), F32).at[0, :N_EXPERTS].set(b_router)
    row = pl.BlockSpec((ROW_BLOCK, d), lambda i: (i, 0))
    out = pl.BlockSpec((ROW_BLOCK, LANES), lambda i: (i, 0))
    return pl.pallas_call(
        _router_kernel,
        grid=(m // ROW_BLOCK,),
        in_specs=[row, pl.BlockSpec((d, LANES), lambda i: (0, 0)), pl.BlockSpec((1, LANES), lambda i: (0, 0))],
        out_specs=[out, out],
        out_shape=[jax.ShapeDtypeStruct((m, LANES), I32), jax.ShapeDtypeStruct((m, LANES), F32)],
        compiler_params=_params("arbitrary"),
        name="router",
    )(x, wp, bp)


def _row_copy(src_hbm, row, dst, slot, sem):
    return pltpu.make_async_copy(src_hbm.at[pl.ds(row, 1), :], dst.at[pl.ds(slot, 1), :], sem)


def _moe_gather_kernel(tok_ref, x_hbm, o_ref, buf, sem):
    def start(r, carry):
        _row_copy(x_hbm, tok_ref[0, 0, r], buf, r, sem).start()
        return carry

    def wait(r, carry):
        _row_copy(x_hbm, 0, buf, r, sem).wait()
        return carry

    lax.fori_loop(0, MOE_ROWS, start, 0)
    lax.fori_loop(0, MOE_ROWS, wait, 0)
    o_ref[...] = buf[...].astype(o_ref.dtype)


def moe_gather(x, buf_tok):
    d = x.shape[1]
    return pl.pallas_call(
        _moe_gather_kernel,
        grid=(MOE_BLOCKS,),
        in_specs=[pl.BlockSpec((1, 1, MOE_ROWS), lambda i: (i, 0, 0), memory_space=pltpu.SMEM),
                  pl.BlockSpec(memory_space=pl.ANY)],
        out_specs=pl.BlockSpec((MOE_ROWS, d), lambda i: (i, 0)),
        out_shape=jax.ShapeDtypeStruct((MOE_BUF, d), BF16),
        scratch_shapes=[pltpu.VMEM((MOE_ROWS, d), F32), pltpu.SemaphoreType.DMA(())],
        compiler_params=_params("arbitrary"),
        name="moe_gather",
    )(buf_tok.reshape(MOE_BLOCKS, 1, MOE_ROWS), x)


def _new_expert(be_ref, i):
    return jnp.logical_or(i == 0, be_ref[i] != be_ref[jnp.maximum(i - 1, 0)])


def _moe_gu_kernel(be_ref, nu_ref, x_ref, wg_ref, wu_ref, o_ref, wgb_ref, wub_ref):
    i = pl.program_id(1)

    @pl.when(_new_expert(be_ref, i))
    def _():
        wgb_ref[...] = wg_ref[...].astype(BF16)
        wub_ref[...] = wu_ref[...].astype(BF16)

    @pl.when(i < nu_ref[0])
    def _():
        x = x_ref[...]
        g = jnp.dot(x, wgb_ref[...], preferred_element_type=F32)
        u = jnp.dot(x, wub_ref[...], preferred_element_type=F32)
        o_ref[...] = (_silu(g) * u).astype(o_ref.dtype)

    @pl.when(i >= nu_ref[0])
    def _():
        o_ref[...] = jnp.zeros(o_ref.shape, o_ref.dtype)


def moe_gu(xs, w_gu, blk_e, n_used, *, bn=1024):
    r, d = xs.shape
    f = w_gu.shape[2] // 2
    nj = f // bn
    return pl.pallas_call(
        _moe_gu_kernel,
        grid_spec=pltpu.PrefetchScalarGridSpec(
            num_scalar_prefetch=2,
            grid=(nj, r // MOE_ROWS),
            in_specs=[pl.BlockSpec((MOE_ROWS, d), lambda j, i, be, nu: (i, 0)),
                      pl.BlockSpec((None, d, bn), lambda j, i, be, nu: (be[i], 0, j)),
                      pl.BlockSpec((None, d, bn), lambda j, i, be, nu: (be[i], 0, j + nj))],
            out_specs=pl.BlockSpec((MOE_ROWS, bn), lambda j, i, be, nu: (i, j)),
            scratch_shapes=[pltpu.VMEM((d, bn), BF16), pltpu.VMEM((d, bn), BF16)]),
        out_shape=jax.ShapeDtypeStruct((r, f), BF16),
        compiler_params=_params("arbitrary", "arbitrary"),
        name="moe_gu",
    )(blk_e, n_used, xs, w_gu, w_gu)


def _moe_down_kernel(be_ref, nu_ref, x_ref, w_ref, o_ref, wb_ref):
    i = pl.program_id(1)

    @pl.when(_new_expert(be_ref, i))
    def _():
        wb_ref[...] = w_ref[...].astype(BF16)

    @pl.when(i < nu_ref[0])
    def _():
        o_ref[...] = jnp.dot(x_ref[...], wb_ref[...], preferred_element_type=F32)

    @pl.when(i >= nu_ref[0])
    def _():
        o_ref[...] = jnp.zeros(o_ref.shape, o_ref.dtype)


def moe_down(hs, w_down, blk_e, n_used, *, bn=512):
    r, f = hs.shape
    d = w_down.shape[2]
    return pl.pallas_call(
        _moe_down_kernel,
        grid_spec=pltpu.PrefetchScalarGridSpec(
            num_scalar_prefetch=2,
            grid=(d // bn, r // MOE_ROWS),
            in_specs=[pl.BlockSpec((MOE_ROWS, f), lambda j, i, be, nu: (i, 0)),
                      pl.BlockSpec((None, f, bn), lambda j, i, be, nu: (be[i], 0, j))],
            out_specs=pl.BlockSpec((MOE_ROWS, bn), lambda j, i, be, nu: (i, j)),
            scratch_shapes=[pltpu.VMEM((f, bn), BF16)]),
        out_shape=jax.ShapeDtypeStruct((r, d), F32),
        compiler_params=_params("arbitrary", "arbitrary"),
        name="moe_down",
    )(blk_e, n_used, hs, w_down)


def _moe_combine_ln_kernel(pos_ref, y_hbm, x_ref, gate_ref, g_ref, b_ref, o_ref, ob_ref, buf, sem):
    def start(r, carry):
        for t in range(MOE_TOPK):
            _row_copy(y_hbm, pos_ref[0, 0, MOE_TOPK * r + t], buf.at[t], r, sem).start()
        return carry

    def wait(r, carry):
        for t in range(MOE_TOPK):
            _row_copy(y_hbm, 0, buf.at[t], r, sem).wait()
        return carry

    lax.fori_loop(0, ROW_BLOCK, start, 0)
    lax.fori_loop(0, ROW_BLOCK, wait, 0)
    gates = gate_ref[...]
    ff = buf[0] * gates[:, 0:1]
    for t in range(1, MOE_TOPK):
        ff = ff + buf[t] * gates[:, t:t + 1]
    out = _layernorm_rows(DEEPNORM_ALPHA * x_ref[...] + ff, g_ref[...], b_ref[...])
    o_ref[...] = out
    ob_ref[...] = out.astype(BF16)


def moe_combine_ln(x, ys, pos, gates, g, b):
    m, d = x.shape
    nblk = m // ROW_BLOCK
    row = pl.BlockSpec((ROW_BLOCK, d), lambda i: (i, 0))
    vec = pl.BlockSpec((1, d), lambda i: (0, 0))
    return pl.pallas_call(
        _moe_combine_ln_kernel,
        grid=(nblk,),
        in_specs=[pl.BlockSpec((1, 1, MOE_TOPK * ROW_BLOCK), lambda i: (i, 0, 0), memory_space=pltpu.SMEM),
                  pl.BlockSpec(memory_space=pl.ANY),
                  row, pl.BlockSpec((ROW_BLOCK, LANES), lambda i: (i, 0)), vec, vec],
        out_specs=[row, row],
        out_shape=[jax.ShapeDtypeStruct((m, d), F32), jax.ShapeDtypeStruct((m, d), BF16)],
        scratch_shapes=[pltpu.VMEM((MOE_TOPK, ROW_BLOCK, d), F32), pltpu.SemaphoreType.DMA(())],
        compiler_params=_params("arbitrary"),
        name="moe_combine_ln",
    )(pos.reshape(nblk, 1, MOE_TOPK * ROW_BLOCK), ys, x, gates, g.reshape(1, d), b.reshape(1, d))


def moe_plan(idx):
    e = idx[:N_TOK, :MOE_TOPK].reshape(N_ASSIGN)
    onehot = (e[:, None] == jnp.arange(N_EXPERTS, dtype=I32)[None, :]).astype(I32)
    before = jnp.cumsum(onehot, axis=0) - onehot
    rank = jnp.sum(before * onehot, axis=1)
    counts = jnp.sum(onehot, axis=0)
    padded = (counts + MOE_ROWS - 1) // MOE_ROWS * MOE_ROWS
    pad_end = jnp.cumsum(padded)
    pad_start = pad_end - padded
    dest = pad_start[e] + rank
    tok = jnp.arange(N_ASSIGN, dtype=I32) // MOE_TOPK
    buf_tok = jnp.zeros((MOE_BUF,), I32).at[dest].set(tok)
    starts = jnp.arange(MOE_BLOCKS, dtype=I32) * MOE_ROWS
    blk_e = jnp.minimum(jnp.searchsorted(pad_end, starts, side="right"), N_EXPERTS - 1).astype(I32)
    n_used = (pad_end[-1:] // MOE_ROWS).astype(I32)
    pos = jnp.zeros((TOK_PAD * MOE_TOPK,), I32).at[:N_ASSIGN].set(dest)
    return buf_tok, blk_e, n_used, pos


def _rope_tables():
    half = ROT_DIM // 2
    inv = ROPE_THETA ** (-jnp.arange(half, dtype=F32) * 2.0 / ROT_DIM)
    pos = jnp.concatenate([jnp.arange(SEQ, dtype=I32), jnp.full((TOK_PAD - SEQ,), PAST_LEN, I32)])
    ang = pos.astype(F32)[:, None] * inv[None, :]
    cos, sin = jnp.cos(ang), jnp.sin(ang)
    rest = LANES - ROT_DIM
    cf = jnp.concatenate([cos, cos, jnp.ones((TOK_PAD, rest), F32)], axis=1)
    s1 = jnp.concatenate([-sin, jnp.zeros((TOK_PAD, LANES - half), F32)], axis=1)
    s2 = jnp.concatenate([jnp.zeros((TOK_PAD, half), F32), sin, jnp.zeros((TOK_PAD, rest), F32)], axis=1)
    return cf, s1, s2


def _tokens(prompt, sample):
    pad = jnp.zeros((TOK_PAD - N_TOK, prompt.shape[1]), prompt.dtype)
    return jnp.concatenate([prompt, sample, pad], axis=0)


def _put_sample(buf, rows):
    tail = jnp.zeros((TOK_PAD - SEQ, buf.shape[1]), buf.dtype).at[:rows.shape[0]].set(rows.astype(buf.dtype))
    return lax.dynamic_update_slice(buf, tail, (SEQ, 0))


def kernel(x_prompt, x_sample, state_ssm, state_conv, cache_k, cache_v, page_table, p_prompt, p_sample, ln_g, ln_b, ssm_w_in, ssm_conv_w, ssm_conv_b, ssm_dt_bias, ssm_A_log, ssm_D, ssm_norm_w, ssm_w_out, attn_w_kv, attn_w_q, attn_w_o, ffn_w_gu, ffn_w_down, moe_w_router, moe_b_router, moe_w_gu, moe_w_down, ple_w_gate, ple_b_gate, ple_w_proj):
    x0 = _tokens(x_prompt[0], x_sample[:, 0])
    x0b = x0.astype(BF16)
    p_tok = [_tokens(p_prompt[i, 0], p_sample[i, :, 0]).astype(BF16) for i in range(2)]

    w_in = ssm_w_in[0]
    zx_cols = SSM_D_INNER + SSM_CONV_DIM
    zx = matmul(x0b, w_in, bn=1024, ncols=zx_cols)
    w_dt = jnp.zeros((D_MODEL, LANES), F32).at[:, :SSM_N_HEADS].set(w_in[:, zx_cols:])
    dt_raw = matmul(x0b, w_dt, bn=LANES)[:, :SSM_N_HEADS]

    xbc = conv_prompt(zx, ssm_conv_w[0], ssm_conv_b[0])
    st_conv = state_conv[0]
    x_raw_s = zx[SEQ:N_TOK, SSM_D_INNER:]
    xbc_s = conv_sample(jnp.transpose(st_conv, (1, 0, 2)), x_raw_s, ssm_conv_w[0], ssm_conv_b[0])
    xbc = lax.dynamic_update_slice(xbc, xbc_s, (SEQ, 0))
    conv_prompt_out = zx[SEQ - (SSM_CONV_W - 1):SEQ, SSM_D_INNER:].reshape(1, 1, SSM_CONV_W - 1, SSM_CONV_DIM)
    conv_sample_out = jnp.concatenate([st_conv[:, 1:], x_raw_s[:, None, :]], axis=1)[None]

    y, h_prompt = ssd_prompt(xbc, dt_raw[:SEQ], ssm_dt_bias[0], ssm_A_log[0])
    xs_s = xbc_s[:DEC_BATCH]
    gn = SSM_N_GROUPS * SSM_D_STATE
    h_sample, y_t = ssd_sample(
        state_ssm[0].reshape(DEC_BATCH, SSM_N_HEADS * SSM_HEAD_DIM, SSM_D_STATE),
        jnp.transpose(xs_s[:, :SSM_D_INNER].reshape(DEC_BATCH, SSM_N_HEADS, SSM_HEAD_DIM), (0, 2, 1)),
        xs_s[:, SSM_D_INNER:SSM_D_INNER + gn].reshape(DEC_BATCH, SSM_N_GROUPS, SSM_D_STATE),
        xs_s[:, SSM_D_INNER + gn:].reshape(DEC_BATCH, SSM_N_GROUPS, SSM_D_STATE),
        dt_raw[SEQ:N_TOK].reshape(DEC_BATCH, 1, SSM_N_HEADS), ssm_dt_bias[0], ssm_A_log[0])
    y = _put_sample(y, jnp.transpose(y_t, (0, 2, 1)).reshape(DEC_BATCH, SSM_D_INNER))
    d_lanes = jnp.repeat(ssm_D[0], SSM_HEAD_DIM).reshape(1, SSM_D_INNER)
    gated = gate_norm(y, xbc, zx, d_lanes, ssm_norm_w[0])
    mix = matmul(gated, ssm_w_out[0], bn=512)
    x1, x1b = add_ln(x0, mix, ln_g[0, 0], ln_b[0, 0])

    hid = matmul_swiglu(x1b, ffn_w_gu[0], bn=512)
    ff = matmul(hid, ffn_w_down[0], bn=256)
    x2, x2b = add_ln(x1, ff, ln_g[0, 1], ln_b[0, 1])
    x3, x3b = ple_mix(x2, x2b, p_tok[0], ple_w_gate[0], ple_b_gate[0], ple_w_proj[0])

    rope = _rope_tables()
    k_all = matmul(x3b, attn_w_kv, bn=KV_COLS, ncols=KV_COLS, rope=rope)
    v_all = matmul(x3b, attn_w_kv, bn=KV_COLS, col0=KV_COLS, ncols=KV_COLS)
    q_all = matmul(x3b, attn_w_q[0], bn=1024, rope=rope)
    att = moba_prompt(q_all, k_all, v_all, block_means(k_all).reshape(SEQ // MOBA_BLOCK, KV_COLS))

    q_s = q_all[SEQ:N_TOK].reshape(DEC_BATCH, ATTN_N_HEADS, ATTN_HEAD_DIM)
    k_s = k_all[SEQ:N_TOK].reshape(DEC_BATCH, ATTN_N_KV_HEADS, ATTN_HEAD_DIM)
    v_s = v_all[SEQ:N_TOK].reshape(DEC_BATCH, ATTN_N_KV_HEADS, ATTN_HEAD_DIM)
    n_pool = cache_k.shape[0]
    sel = sample_select(page_table, q_s, cache_k.reshape(n_pool, PAGE_ROWS, ATTN_HEAD_DIM))[:, :, :MOBA_TOPK]
    att_s = sample_attend(page_table, sel, q_s, k_s, v_s, cache_k, cache_v)
    att = _put_sample(att, att_s.reshape(DEC_BATCH, D_MODEL))
    mix = matmul(att, attn_w_o[0], bn=1024)
    x4, x4b = add_ln(x3, mix, ln_g[1, 0], ln_b[1, 0])

    idx, gates = router(x4, moe_w_router[0], moe_b_router[0])
    buf_tok, blk_e, n_used, pos = moe_plan(idx)
    xs = moe_gather(x4, buf_tok)
    hs = moe_gu(xs, moe_w_gu[0], blk_e, n_used)
    ys = moe_down(hs, moe_w_down[0], blk_e, n_used)
    x5, x5b = moe_combine_ln(x4, ys, pos, gates, ln_g[1, 1], ln_b[1, 1])
    x6, _ = ple_mix(x5, x5b, p_tok[1], ple_w_gate[1], ple_b_gate[1], ple_w_proj[1])

    state_shape = (1, -1, SSM_N_HEADS, SSM_HEAD_DIM, SSM_D_STATE)
    kv_shape = (-1, ATTN_N_KV_HEADS, ATTN_HEAD_DIM)
    return (x6[:SEQ].reshape(1, SEQ, D_MODEL),
            x6[SEQ:N_TOK].reshape(DEC_BATCH, 1, D_MODEL),
            h_prompt.reshape(state_shape),
            conv_prompt_out,
            k_all[:SEQ].reshape((1,) + (SEQ,) + kv_shape[1:]),
            v_all[:SEQ].reshape((1,) + (SEQ,) + kv_shape[1:]),
            h_sample.reshape(state_shape),
            conv_sample_out,
            k_all[SEQ:N_TOK].reshape((DEC_BATCH, 1) + kv_shape[1:]),
            v_all[SEQ:N_TOK].reshape((DEC_BATCH, 1) + kv_shape[1:]))
```

```python
import functools

import jax
import jax.numpy as jnp
import numpy as np
from jax import lax
from jax.experimental import pallas as pl
from jax.experimental.pallas import tpu as pltpu

F32 = jnp.float32
BF16 = jnp.bfloat16
I32 = jnp.int32

D_MODEL = 2048
SEQ = 8192
DEC_BATCH = 32
PAST_LEN = 8192
PAGE_SIZE = 128

SSM_D_INNER = 4096
SSM_HEAD_DIM = 64
SSM_N_HEADS = 64
SSM_N_GROUPS = 8
SSM_D_STATE = 128
SSM_CONV_W = 4
SSM_CHUNK = 128
SSM_CONV_DIM = 6144
SSM_HEADS_PER_GROUP = SSM_N_HEADS // SSM_N_GROUPS

ATTN_HEAD_DIM = 128
ATTN_N_HEADS = 16
ATTN_N_KV_HEADS = 4
ATTN_GROUP = 4
ROT_DIM = 32
ROPE_THETA = 500000.0
MOBA_BLOCK = 256
MOBA_TOPK = 3
N_PAST_BLOCKS = PAST_LEN // MOBA_BLOCK
PAGES_PER_BLOCK = MOBA_BLOCK // PAGE_SIZE
N_PAGES = PAST_LEN // PAGE_SIZE

D_FF = 5632
N_EXPERTS = 8
MOE_TOPK = 2
D_FF_EXPERT = 7168
D_PLE = 256
DEEPNORM_ALPHA = 4.0 ** 0.25
LN_EPS = 1e-5
RMS_EPS = 1e-5

LANES = 128
VMEM_LIMIT_BYTES = 56 * 1024 * 1024

N_TOK = SEQ + DEC_BATCH
TOK_PAD = 8448
MM_ROWS = 768
ROW_BLOCK = 256
MOE_ROWS = 256
N_ASSIGN = N_TOK * MOE_TOPK
MOE_BLOCKS = -(-N_ASSIGN // MOE_ROWS) + N_EXPERTS
MOE_BUF = MOE_BLOCKS * MOE_ROWS
NEG_INF = float("-inf")


def _params(*sem):
    return pltpu.CompilerParams(dimension_semantics=sem, vmem_limit_bytes=VMEM_LIMIT_BYTES)


def _split3(a):
    hi = a.astype(BF16)
    r1 = a - hi.astype(F32)
    mid = r1.astype(BF16)
    lo = (r1 - mid.astype(F32)).astype(BF16)
    return hi, mid, lo


def _dot_f32(a, b, dims):
    a0, a1, a2 = _split3(a)
    b0, b1, b2 = _split3(b)
    dn = (dims, ((), ()))
    d = lambda x, y: lax.dot_general(x, y, dn, preferred_element_type=F32)
    return (d(a0, b0) + (d(a0, b1) + d(a1, b0))) + ((d(a0, b2) + d(a2, b0)) + d(a1, b1))


def _softplus(x):
    return jnp.maximum(x, 0.0) + jnp.log1p(jnp.exp(-jnp.abs(x)))


def _silu(x):
    return x * jax.nn.sigmoid(x)


def _rope_lanes(a, cf, s1, s2):
    return a * cf + pltpu.roll(a, LANES - ROT_DIM // 2, 1) * s1 + pltpu.roll(a, ROT_DIM // 2, 1) * s2


def _mm_kernel(x_ref, w_ref, *rest, rope, valid_cols):
    if rope:
        cf_ref, s1_ref, s2_ref, o_ref, wb_ref = rest
    else:
        o_ref, wb_ref = rest

    @pl.when(pl.program_id(1) == 0)
    def _():
        w = w_ref[...]
        if valid_cols is not None:
            w = jnp.where(lax.broadcasted_iota(I32, w.shape, 1) < valid_cols, w, 0.0)
        wb_ref[...] = w.astype(BF16)

    acc = jnp.dot(x_ref[...], wb_ref[...], preferred_element_type=F32)
    if rope:
        cf, s1, s2 = cf_ref[...], s1_ref[...], s2_ref[...]
        for c in range(acc.shape[1] // LANES):
            sl = slice(c * LANES, (c + 1) * LANES)
            o_ref[:, sl] = _rope_lanes(acc[:, sl], cf, s1, s2).astype(o_ref.dtype)
    else:
        o_ref[...] = acc.astype(o_ref.dtype)


def matmul(x, w, *, bn, col0=0, ncols=None, rope=None, out_dtype=F32, bm=MM_ROWS):
    m, k = x.shape
    ncols = w.shape[1] if ncols is None else ncols
    assert m % bm == 0 and ncols % bn == 0 and col0 % bn == 0
    j0 = col0 // bn
    valid_cols = None
    if col0 + ncols > w.shape[1]:
        assert ncols == bn
        valid_cols = w.shape[1] - col0
    in_specs = [pl.BlockSpec((bm, k), lambda j, i: (i, 0)),
                pl.BlockSpec((k, bn), lambda j, i: (0, j + j0))]
    args = [x, w]
    if rope is not None:
        in_specs += [pl.BlockSpec((bm, LANES), lambda j, i: (i, 0))] * 3
        args += list(rope)
    return pl.pallas_call(
        functools.partial(_mm_kernel, rope=rope is not None, valid_cols=valid_cols),
        grid=(ncols // bn, m // bm),
        in_specs=in_specs,
        out_specs=pl.BlockSpec((bm, bn), lambda j, i: (i, j)),
        out_shape=jax.ShapeDtypeStruct((m, ncols), out_dtype),
        scratch_shapes=[pltpu.VMEM((k, bn), BF16)],
        compiler_params=_params("arbitrary", "arbitrary"),
        name="matmul",
    )(*args)


def _mm_swiglu_kernel(x_ref, wg_ref, wu_ref, o_ref, wgb_ref, wub_ref):
    @pl.when(pl.program_id(1) == 0)
    def _():
        wgb_ref[...] = wg_ref[...].astype(BF16)
        wub_ref[...] = wu_ref[...].astype(BF16)

    x = x_ref[...]
    g = jnp.dot(x, wgb_ref[...], preferred_element_type=F32)
    u = jnp.dot(x, wub_ref[...], preferred_element_type=F32)
    o_ref[...] = (_silu(g) * u).astype(o_ref.dtype)


def matmul_swiglu(x, w_gu, *, bn, bm=MM_ROWS):
    m, k = x.shape
    f = w_gu.shape[1] // 2
    assert m % bm == 0 and f % bn == 0
    nj = f // bn
    return pl.pallas_call(
        _mm_swiglu_kernel,
        grid=(nj, m // bm),
        in_specs=[pl.BlockSpec((bm, k), lambda j, i: (i, 0)),
                  pl.BlockSpec((k, bn), lambda j, i: (0, j)),
                  pl.BlockSpec((k, bn), lambda j, i: (0, j + nj))],
        out_specs=pl.BlockSpec((bm, bn), lambda j, i: (i, j)),
        out_shape=jax.ShapeDtypeStruct((m, f), BF16),
        scratch_shapes=[pltpu.VMEM((k, bn), BF16), pltpu.VMEM((k, bn), BF16)],
        compiler_params=_params("arbitrary", "arbitrary"),
        name="matmul_swiglu",
    )(x, w_gu, w_gu)


def _ple_kernel(xb_ref, pb_ref, wg_ref, wp_ref, bg_ref, x_ref, o_ref, o2_ref, wgb_ref, wpb_ref, *, final):
    @pl.when(pl.program_id(1) == 0)
    def _():
        wgb_ref[...] = wg_ref[...].astype(BF16)
        wpb_ref[...] = wp_ref[...].astype(BF16)

    gate = jax.nn.sigmoid(jnp.dot(xb_ref[...], wgb_ref[...], preferred_element_type=F32) + bg_ref[...])
    proj = jnp.dot(pb_ref[...], wpb_ref[...], preferred_element_type=F32)
    out = x_ref[...] + gate * proj
    o_ref[...] = out
    if final:
        tail0 = SEQ - (TOK_PAD - MM_ROWS)

        @pl.when(pl.program_id(1) == pl.num_programs(1) - 1)
        def _():
            o2_ref[...] = out[tail0:tail0 + DEC_BATCH, :]
    else:
        o2_ref[...] = out.astype(BF16)


def ple_mix(x, xb, pb, w_gate, b_gate, w_proj, layer, *, final=False, bn=512, bm=MM_ROWS):
    m, d = x.shape
    kp = pb.shape[1]
    full = pl.BlockSpec((bm, bn), lambda j, i: (i, j))
    if final:
        assert m == TOK_PAD and bm == MM_ROWS and 0 <= SEQ - (TOK_PAD - MM_ROWS) <= MM_ROWS - DEC_BATCH
        out_specs = [full, pl.BlockSpec((DEC_BATCH, bn), lambda j, i: (0, j))]
        out_shape = [jax.ShapeDtypeStruct((SEQ, d), F32), jax.ShapeDtypeStruct((DEC_BATCH, d), F32)]
    else:
        out_specs = [full, full]
        out_shape = [jax.ShapeDtypeStruct((m, d), F32), jax.ShapeDtypeStruct((m, d), BF16)]
    return pl.pallas_call(
        functools.partial(_ple_kernel, final=final),
        grid=(d // bn, m // bm),
        in_specs=[pl.BlockSpec((bm, d), lambda j, i: (i, 0)),
                  pl.BlockSpec((bm, kp), lambda j, i: (i, 0)),
                  pl.BlockSpec((None, d, bn), lambda j, i: (layer, 0, j)),
                  pl.BlockSpec((None, kp, bn), lambda j, i: (layer, 0, j)),
                  pl.BlockSpec((None, 1, bn), lambda j, i: (layer, 0, j)),
                  full],
        out_specs=out_specs,
        out_shape=out_shape,
        scratch_shapes=[pltpu.VMEM((d, bn), BF16), pltpu.VMEM((kp, bn), BF16)],
        compiler_params=_params("arbitrary", "arbitrary"),
        name="ple_mix",
    )(xb, pb, w_gate, w_proj, b_gate.reshape(b_gate.shape[0], 1, d), x)


def _layernorm_rows(v, g, b):
    mu = jnp.mean(v, axis=-1, keepdims=True)
    c = v - mu
    var = jnp.mean(c * c, axis=-1, keepdims=True)
    return c * lax.rsqrt(var + LN_EPS) * g + b


def _add_ln_kernel(x_ref, m_ref, g_ref, b_ref, o_ref, ob_ref):
    out = _layernorm_rows(DEEPNORM_ALPHA * x_ref[...] + m_ref[...], g_ref[...], b_ref[...])
    o_ref[...] = out
    ob_ref[...] = out.astype(BF16)


def add_ln(x, mix, g, b):
    m, d = x.shape
    row = pl.BlockSpec((ROW_BLOCK, d), lambda i: (i, 0))
    vec = pl.BlockSpec((1, d), lambda i: (0, 0))
    return pl.pallas_call(
        _add_ln_kernel,
        grid=(m // ROW_BLOCK,),
        in_specs=[row, row, vec, vec],
        out_specs=[row, row],
        out_shape=[jax.ShapeDtypeStruct((m, d), F32), jax.ShapeDtypeStruct((m, d), BF16)],
        compiler_params=_params("arbitrary"),
        name="add_ln",
    )(x, mix, g.reshape(1, d), b.reshape(1, d))


def _gate_norm_kernel(y_ref, xs_ref, z_ref, d_ref, w_ref, o_ref):
    g = (y_ref[...] + d_ref[...] * xs_ref[...]) * _silu(z_ref[...])
    ms = jnp.mean(g * g, axis=-1, keepdims=True)
    o_ref[...] = (g * lax.rsqrt(ms + RMS_EPS) * w_ref[...]).astype(o_ref.dtype)


def gate_norm(y, xbc, zx, d_lanes, norm_w):
    m, di = y.shape
    row = pl.BlockSpec((ROW_BLOCK, di), lambda i: (i, 0))
    vec = pl.BlockSpec((1, di), lambda i: (0, 0))
    return pl.pallas_call(
        _gate_norm_kernel,
        grid=(m // ROW_BLOCK,),
        in_specs=[row, row, row, vec, vec],
        out_specs=row,
        out_shape=jax.ShapeDtypeStruct((m, di), BF16),
        compiler_params=_params("arbitrary"),
        name="gate_norm",
    )(y, xbc, zx, d_lanes, norm_w.reshape(1, di))


CONV_COLS = 2048


def _conv_prompt_kernel(x_ref, h_ref, w_ref, b_ref, o_ref):
    @pl.when(pl.program_id(0) < SEQ // ROW_BLOCK)
    def _():
        _conv_prompt_rows(x_ref, h_ref, w_ref, b_ref, o_ref)

    @pl.when(pl.program_id(0) >= SEQ // ROW_BLOCK)
    def _():
        o_ref[...] = jnp.zeros(o_ref.shape, o_ref.dtype)


def _conv_prompt_rows(x_ref, h_ref, w_ref, b_ref, o_ref):
    x = x_ref[...]
    halo = jnp.where(pl.program_id(0) == 0, 0.0, h_ref[...])
    w = w_ref[...]
    acc = b_ref[...] + x * w[SSM_CONV_W - 1:SSM_CONV_W, :]
    rows8 = lax.broadcasted_iota(I32, halo.shape, 0)
    for s in range(1, SSM_CONV_W):
        xr = pltpu.roll(x, s, 0)
        head = jnp.where(rows8 < s, pltpu.roll(halo, s, 0), xr[:8, :])
        shifted = jnp.concatenate([head, xr[8:, :]], axis=0)
        acc = acc + shifted * w[SSM_CONV_W - 1 - s:SSM_CONV_W - s, :]
    o_ref[...] = _silu(acc)


def conv_prompt(zx, conv_w, conv_b):
    c0 = SSM_D_INNER // CONV_COLS
    rb8 = ROW_BLOCK // 8
    return pl.pallas_call(
        _conv_prompt_kernel,
        grid=(TOK_PAD // ROW_BLOCK, SSM_CONV_DIM // CONV_COLS),
        in_specs=[pl.BlockSpec((ROW_BLOCK, CONV_COLS), lambda i, k: (i, c0 + k)),
                  pl.BlockSpec((8, CONV_COLS), lambda i, k: (jnp.maximum(i * rb8 - 1, 0), c0 + k)),
                  pl.BlockSpec((SSM_CONV_W, CONV_COLS), lambda i, k: (0, k)),
                  pl.BlockSpec((1, CONV_COLS), lambda i, k: (0, k))],
        out_specs=pl.BlockSpec((ROW_BLOCK, CONV_COLS), lambda i, k: (i, k)),
        out_shape=jax.ShapeDtypeStruct((TOK_PAD, SSM_CONV_DIM), F32),
        compiler_params=_params("arbitrary", "arbitrary"),
        name="conv_prompt",
    )(zx, zx, conv_w, conv_b.reshape(1, SSM_CONV_DIM))


def _conv_sample_kernel(st_ref, x_ref, w_ref, b_ref, o_ref):
    w = w_ref[...]
    acc = b_ref[...] + x_ref[...] * w[SSM_CONV_W - 1:SSM_CONV_W, :]
    for j in range(SSM_CONV_W - 1):
        acc = acc + st_ref[j] * w[j:j + 1, :]
    o_ref[...] = jnp.zeros(o_ref.shape, F32)
    o_ref[:DEC_BATCH, :] = _silu(acc)


def conv_sample(st, x_raw, conv_w, conv_b):
    return pl.pallas_call(
        _conv_sample_kernel,
        grid=(SSM_CONV_DIM // CONV_COLS,),
        in_specs=[pl.BlockSpec((SSM_CONV_W - 1, DEC_BATCH, CONV_COLS), lambda k: (0, 0, k)),
                  pl.BlockSpec((DEC_BATCH, CONV_COLS), lambda k: (0, k)),
                  pl.BlockSpec((SSM_CONV_W, CONV_COLS), lambda k: (0, k)),
                  pl.BlockSpec((1, CONV_COLS), lambda k: (0, k))],
        out_specs=pl.BlockSpec((TOK_PAD - SEQ, CONV_COLS), lambda k: (0, k)),
        out_shape=jax.ShapeDtypeStruct((TOK_PAD - SEQ, SSM_CONV_DIM), F32),
        compiler_params=_params("arbitrary"),
        name="conv_sample",
    )(st, x_raw, conv_w, conv_b.reshape(1, SSM_CONV_DIM))


SSD_GROUP_COLS = SSM_HEADS_PER_GROUP * SSM_HEAD_DIM
SSD_PAIR_ROWS = 2 * SSM_HEAD_DIM


def _ssd_prompt_kernel(xs_ref, b_ref, c_ref, dtc_ref, dtr_ref, bc_ref, br_ref, ac_ref, ar_ref,
                       y_ref, h_ref):
    @pl.when(pl.program_id(0) < SEQ // SSM_CHUNK)
    def _():
        _ssd_prompt_chunk(xs_ref, b_ref, c_ref, dtc_ref, dtr_ref, bc_ref, br_ref, ac_ref, ar_ref, y_ref, h_ref)

    @pl.when(pl.program_id(0) >= SEQ // SSM_CHUNK)
    def _():
        y_ref[...] = jnp.zeros(y_ref.shape, y_ref.dtype)


def _ssd_prompt_chunk(xs_ref, b_ref, c_ref, dtc_ref, dtr_ref, bc_ref, br_ref, ac_ref, ar_ref,
                      y_ref, h_ref):
    q = SSM_CHUNK
    dt_c = _softplus(dtc_ref[...] + bc_ref[...])
    dt_r = _softplus(dtr_ref[...] + br_ref[...])
    da_c = dt_c * -jnp.exp(ac_ref[...])
    da_r = dt_r * -jnp.exp(ar_ref[...])
    ti = lax.broadcasted_iota(I32, (q, q), 0)
    si = lax.broadcasted_iota(I32, (q, q), 1)
    causal = ti >= si
    tril = causal.astype(F32)
    triu = (ti <= si).astype(F32)
    acs_c = _dot_f32(tril, da_c, ((1,), (0,)))
    acs_r = _dot_f32(da_r, triu, ((1,), (0,)))
    last = acs_c[q - 1:q, :]

    @pl.when(pl.program_id(0) == 0)
    def _():
        h_ref[...] = jnp.zeros(h_ref.shape, F32)

    lane = lax.broadcasted_iota(I32, (q, LANES), 1)
    lo = lane < SSM_HEAD_DIM
    sub = lax.broadcasted_iota(I32, (SSD_PAIR_ROWS, 1), 0)
    for g in range(SSM_N_GROUPS):
        n_cols = slice(g * SSM_D_STATE, (g + 1) * SSM_D_STATE)
        bm = b_ref[:, n_cols].astype(BF16)
        cm = c_ref[:, n_cols].astype(BF16)
        cb = lax.dot_general(cm, bm, (((1,), (1,)), ((), ())), preferred_element_type=F32)
        for pair in range(SSM_HEADS_PER_GROUP // 2):
            r1 = g * SSM_HEADS_PER_GROUP + 2 * pair
            r2 = r1 + 1
            cols = slice(r1 * SSM_HEAD_DIM, (r2 + 1) * SSM_HEAD_DIM)
            xdt = xs_ref[:, cols] * jnp.where(lo, dt_c[:, r1:r1 + 1], dt_c[:, r2:r2 + 1])
            ms = []
            for r in (r1, r2):
                seg = acs_c[:, r:r + 1] - acs_r[r:r + 1, :]
                ms.append((cb * jnp.exp(jnp.where(causal, seg, NEG_INF))).astype(BF16))
            y = (jnp.dot(ms[0], jnp.where(lo, xdt, 0.0).astype(BF16), preferred_element_type=F32)
                 + jnp.dot(ms[1], jnp.where(lo, 0.0, xdt).astype(BF16), preferred_element_type=F32))

            h_in = h_ref[cols, :]
            y_off = lax.dot_general(cm, h_in.astype(BF16), (((1,), (1,)), ((), ())),
                                    preferred_element_type=F32)
            y = y + y_off * jnp.exp(jnp.where(lo, acs_c[:, r1:r1 + 1], acs_c[:, r2:r2 + 1]))
            y_ref[:, cols] = y

            to_end = jnp.exp(jnp.where(lo, last[:, r1:r1 + 1] - acs_c[:, r1:r1 + 1],
                                       last[:, r2:r2 + 1] - acs_c[:, r2:r2 + 1]))
            xw_t = (xdt * to_end).T.astype(BF16)
            st = jnp.dot(xw_t, bm, preferred_element_type=F32)
            decay = jnp.exp(jnp.where(sub < SSM_HEAD_DIM, last[:, r1:r1 + 1], last[:, r2:r2 + 1]))
            h_ref[cols, :] = h_in * decay + st


def ssd_prompt(xbc, dt_raw, dt_bias, a_log):
    nc = SEQ // SSM_CHUNK
    nh = SSM_N_HEADS
    gn = SSM_N_GROUPS * SSM_D_STATE
    clamp = lambda c: jnp.minimum(c, nc - 1)
    col_vec = pl.BlockSpec((1, nh), lambda c: (0, 0))
    row_vec = pl.BlockSpec((nh, 1), lambda c: (0, 0))
    return pl.pallas_call(
        _ssd_prompt_kernel,
        grid=(TOK_PAD // SSM_CHUNK,),
        in_specs=[pl.BlockSpec((SSM_CHUNK, SSM_D_INNER), lambda c: (c, 0)),
                  pl.BlockSpec((SSM_CHUNK, gn), lambda c: (c, SSM_D_INNER // gn)),
                  pl.BlockSpec((SSM_CHUNK, gn), lambda c: (c, SSM_D_INNER // gn + 1)),
                  pl.BlockSpec((SSM_CHUNK, nh), lambda c: (clamp(c), 0)),
                  pl.BlockSpec((nh, SSM_CHUNK), lambda c: (0, clamp(c))),
                  col_vec, row_vec, col_vec, row_vec],
        out_specs=[pl.BlockSpec((SSM_CHUNK, SSM_D_INNER), lambda c: (c, 0)),
                   pl.BlockSpec((nh * SSM_HEAD_DIM, SSM_D_STATE), lambda c: (0, 0))],
        out_shape=[jax.ShapeDtypeStruct((TOK_PAD, SSM_D_INNER), F32),
                   jax.ShapeDtypeStruct((nh * SSM_HEAD_DIM, SSM_D_STATE), F32)],
        compiler_params=_params("arbitrary"),
        name="ssd_prompt",
    )(xbc, xbc, xbc, dt_raw, dt_raw.T,
      dt_bias.reshape(1, nh), dt_bias.reshape(nh, 1), a_log.reshape(1, nh), a_log.reshape(nh, 1))


def _mxu_round(x):
    return x.astype(BF16).astype(F32)


def _ssd_sample_kernel(h0_ref, xt_ref, xl_ref, b_ref, c_ref, dt_ref, dtl_ref, bias_ref, biasl_ref,
                       alog_ref, alogl_ref, h_ref, y_ref):
    dt = _softplus(dt_ref[...] + bias_ref[...])
    dec = jnp.exp(dt * -jnp.exp(alog_ref[...]))
    dt_l = _softplus(dtl_ref[...] + biasl_ref[...])
    dec_l = jnp.exp(dt_l * -jnp.exp(alogl_ref[...]))
    xdt_t = _mxu_round(xt_ref[...] * dt)
    xdt_l = _mxu_round(xl_ref[...] * dt_l)
    bm = _mxu_round(b_ref[...])
    cm = _mxu_round(c_ref[...])
    for grp in range(SSM_N_GROUPS):
        b_row = bm[grp:grp + 1, :]
        c_row = cm[grp:grp + 1, :]
        cb = _mxu_round(jnp.sum(c_row * b_row, axis=1, keepdims=True))
        old = []
        for r in range(SSM_HEADS_PER_GROUP):
            h = grp * SSM_HEADS_PER_GROUP + r
            rows = slice(h * SSM_HEAD_DIM, (h + 1) * SSM_HEAD_DIM)
            h0 = h0_ref[rows, :]
            h_ref[rows, :] = h0 * dec[:, h:h + 1] + xdt_t[:, h:h + 1] * b_row
            old.append(h0)
        h0_grp = jnp.concatenate(old, axis=0).astype(BF16)
        c_rows = jnp.broadcast_to(c_row, (8, SSM_D_STATE)).astype(BF16)
        ch0 = lax.dot_general(c_rows, h0_grp, (((1,), (1,)), ((), ())), preferred_element_type=F32)[0:1, :]
        cols = slice(grp * SSD_GROUP_COLS, (grp + 1) * SSD_GROUP_COLS)
        y_ref[:, cols] = cb * xdt_l[:, cols] + ch0 * dec_l[:, cols]


def ssd_sample(h0, x, bm, cm, dt_raw, dt_bias, a_log):
    nb = h0.shape[0]
    nh, p = SSM_N_HEADS, SSM_HEAD_DIM
    hp = nh * p
    lanes = lambda v: jnp.repeat(v, p, axis=-1)
    x_t = jnp.transpose(x.reshape(nb, nh, p), (0, 2, 1))
    per_b = lambda *shape: pl.BlockSpec((None,) + shape, lambda b: (b,) + (0,) * len(shape))
    vec = lambda n: pl.BlockSpec((1, n), lambda b: (0, 0))
    return pl.pallas_call(
        _ssd_sample_kernel,
        grid=(nb,),
        in_specs=[per_b(hp, SSM_D_STATE), per_b(p, nh), per_b(1, hp),
                  per_b(SSM_N_GROUPS, SSM_D_STATE), per_b(SSM_N_GROUPS, SSM_D_STATE),
                  per_b(1, nh), per_b(1, hp), vec(nh), vec(hp), vec(nh), vec(hp)],
        out_specs=[per_b(hp, SSM_D_STATE), per_b(1, hp)],
        out_shape=[jax.ShapeDtypeStruct((nb, hp, SSM_D_STATE), F32),
                   jax.ShapeDtypeStruct((nb, 1, hp), F32)],
        compiler_params=_params("arbitrary"),
        name="ssd_sample",
    )(h0, x_t, x.reshape(nb, 1, hp), bm, cm, dt_raw.reshape(nb, 1, nh), lanes(dt_raw).reshape(nb, 1, hp),
      dt_bias.reshape(1, nh), lanes(dt_bias).reshape(1, hp), a_log.reshape(1, nh), lanes(a_log).reshape(1, hp))


KV_COLS = ATTN_N_KV_HEADS * ATTN_HEAD_DIM
Q_LANES = ATTN_GROUP * MOBA_BLOCK


def _kmeans_kernel(k_ref, o_ref):
    o_ref[...] = jnp.mean(k_ref[...], axis=0, keepdims=True).reshape(o_ref.shape)


def block_means(k):
    nb = SEQ // MOBA_BLOCK
    return pl.pallas_call(
        _kmeans_kernel,
        grid=(nb,),
        in_specs=[pl.BlockSpec((MOBA_BLOCK, KV_COLS), lambda i: (i, 0))],
        out_specs=pl.BlockSpec((None, 1, KV_COLS), lambda i: (i, 0, 0)),
        out_shape=jax.ShapeDtypeStruct((nb, 1, KV_COLS), F32),
        compiler_params=_params("arbitrary"),
        name="block_means",
    )(k)


def _top3_mask_rows(gate):
    n = gate.shape[0]
    rows = lax.broadcasted_iota(I32, gate.shape, 0)
    sel = jnp.zeros(gate.shape, jnp.bool_)
    for _ in range(MOBA_TOPK):
        m = jnp.max(gate, axis=0, keepdims=True)
        first = jnp.min(jnp.where(gate == m, rows, n), axis=0, keepdims=True)
        hit = jnp.logical_and(rows == first, m > NEG_INF)
        sel = jnp.logical_or(sel, hit)
        gate = jnp.where(rows == first, NEG_INF, gate)
    return sel


MASKED_SCORE = -1e30
LOG2_E = 1.4426950408889634


def _moba_prompt_kernel(qi_ref, jj_ref, q_ref, k_ref, v_ref, mean_ref, o_ref,
                        qaug_ref, s_ref, m_ref, l_ref, acc_ref):
    t = pl.program_id(0)
    i = qi_ref[t]
    j = jj_ref[t]
    nb = mean_ref.shape[0]
    q_scale = ATTN_HEAD_DIM ** -0.5 * LOG2_E
    hd = ATTN_HEAD_DIM

    @pl.when(i == nb)
    def _():
        o_ref[...] = jnp.zeros(o_ref.shape, o_ref.dtype)

    @pl.when(jnp.logical_and(i < nb, j == 0))
    def _():
        blk = lax.broadcasted_iota(I32, (nb, Q_LANES), 0)
        pad = jnp.zeros((LANES - nb, MOBA_BLOCK), F32)
        for g in range(ATTN_N_KV_HEADS):
            q_g = jnp.concatenate([q_ref[:, (ATTN_GROUP * g + r) * hd:(ATTN_GROUP * g + r + 1) * hd]
                                   for r in range(ATTN_GROUP)], axis=0)
            gate = lax.dot_general(mean_ref[:, g * hd:(g + 1) * hd].astype(BF16), q_g.astype(BF16),
                                   (((1,), (1,)), ((), ())), preferred_element_type=F32)
            sel = _top3_mask_rows(jnp.where(blk < i, gate, NEG_INF))
            bias = jnp.where(jnp.logical_or(sel, blk == i), 0.0, MASKED_SCORE)
            bias_q = jnp.concatenate(
                [jnp.concatenate([bias[:, r * MOBA_BLOCK:(r + 1) * MOBA_BLOCK], pad], axis=0).T
                 for r in range(ATTN_GROUP)], axis=0)
            qaug_ref[g] = jnp.concatenate([(q_g * q_scale).astype(BF16), bias_q.astype(BF16)], axis=1)
        m_ref[...] = jnp.full(m_ref.shape, NEG_INF, F32)
        l_ref[...] = jnp.zeros(l_ref.shape, F32)
        acc_ref[...] = jnp.zeros(acc_ref.shape, F32)

    def attend(own):
        blk_lane = lax.broadcasted_iota(I32, (MOBA_BLOCK, LANES), 1)
        onehot = jnp.where(blk_lane == i - j, 1.0, 0.0).astype(BF16)
        for g in range(ATTN_N_KV_HEADS):
            k_aug = jnp.concatenate([k_ref[:, g * hd:(g + 1) * hd].astype(BF16), onehot], axis=1)
            s_ref[g] = lax.dot_general(k_aug, qaug_ref[g], (((1,), (1,)), ((), ())), preferred_element_type=F32)
        if own:
            key = lax.broadcasted_iota(I32, (MOBA_BLOCK, Q_LANES), 0)
            qry = lax.broadcasted_iota(I32, (MOBA_BLOCK, Q_LANES), 1) & (MOBA_BLOCK - 1)
            causal = key <= qry
        for g in range(ATTN_N_KV_HEADS):
            s = s_ref[g]
            if own:
                s = jnp.where(causal, s, NEG_INF)
            m_old = m_ref[g]
            m_new = jnp.maximum(m_old, jnp.max(s, axis=0, keepdims=True))
            alpha = jnp.exp2(m_old - m_new)
            p = jnp.exp2(s - m_new)
            l_ref[g] = alpha * l_ref[g] + jnp.sum(p, axis=0, keepdims=True)
            m_ref[g] = m_new
            v_t = v_ref[:, g * hd:(g + 1) * hd].T.astype(BF16)
            acc_ref[g] = alpha * acc_ref[g] + jnp.dot(v_t, p.astype(BF16), preferred_element_type=F32)

    @pl.when(jnp.logical_and(i < nb, j == 0))
    def _():
        attend(True)

    @pl.when(jnp.logical_and(i < nb, j > 0))
    def _():
        attend(False)

    @pl.when(jnp.logical_and(i < nb, j == i))
    def _():
        for g in range(ATTN_N_KV_HEADS):
            out = (acc_ref[g] / l_ref[g]).T
            for r in range(ATTN_GROUP):
                h = ATTN_GROUP * g + r
                o_ref[:, h * hd:(h + 1) * hd] = out[r * MOBA_BLOCK:(r + 1) * MOBA_BLOCK, :].astype(o_ref.dtype)


def moba_prompt(q, k, v, means):
    nb = SEQ // MOBA_BLOCK
    assert nb <= LANES
    steps = [(i, j) for i in range(nb) for j in range(i + 1)] + [(nb, 0)]
    qi = jnp.asarray(np.array([s[0] for s in steps], np.int32))
    jj = jnp.asarray(np.array([s[1] for s in steps], np.int32))
    kv_spec = pl.BlockSpec((MOBA_BLOCK, KV_COLS), lambda t, qi, jj: (jnp.minimum(qi[t] - jj[t], nb - 1), 0))
    row_spec = pl.BlockSpec((MOBA_BLOCK, D_MODEL), lambda t, qi, jj: (qi[t], 0))
    return pl.pallas_call(
        _moba_prompt_kernel,
        grid_spec=pltpu.PrefetchScalarGridSpec(
            num_scalar_prefetch=2,
            grid=(len(steps),),
            in_specs=[row_spec, kv_spec, kv_spec, pl.BlockSpec((nb, KV_COLS), lambda t, qi, jj: (0, 0))],
            out_specs=row_spec,
            scratch_shapes=[pltpu.VMEM((ATTN_N_KV_HEADS, Q_LANES, ATTN_HEAD_DIM + LANES), BF16),
                            pltpu.VMEM((ATTN_N_KV_HEADS, MOBA_BLOCK, Q_LANES), F32),
                            pltpu.VMEM((ATTN_N_KV_HEADS, 1, Q_LANES), F32),
                            pltpu.VMEM((ATTN_N_KV_HEADS, 1, Q_LANES), F32),
                            pltpu.VMEM((ATTN_N_KV_HEADS, ATTN_HEAD_DIM, Q_LANES), F32)]),
        out_shape=jax.ShapeDtypeStruct((TOK_PAD, D_MODEL), BF16),
        compiler_params=_params("arbitrary"),
        name="moba_prompt",
    )(qi, jj, q, k, v, means)


PAGE_ROWS = PAGE_SIZE * ATTN_N_KV_HEADS
SEL_KEYS = MOBA_TOPK * MOBA_BLOCK


def _page_copy(pool_hbm, page, dst, sem):
    return pltpu.make_async_copy(pool_hbm.at[page], dst, sem)


def _sample_select_kernel(pt_ref, q_ref, kc_hbm, o_ref, kbuf, means_ref, sems):
    b = pl.program_id(0)
    slot = b % 2

    def fetch(seq, slot_, start):
        for p in range(N_PAGES):
            cp = _page_copy(kc_hbm, pt_ref[seq * N_PAGES + p], kbuf.at[slot_, p], sems.at[slot_])
            cp.start() if start else cp.wait()

    @pl.when(b == 0)
    def _():
        fetch(b, slot, True)

    @pl.when(b + 1 < pl.num_programs(0))
    def _():
        fetch(b + 1, 1 - slot, True)

    fetch(b, slot, False)
    fold = 8 // ATTN_N_KV_HEADS
    for n in range(N_PAST_BLOCKS):
        s = jnp.zeros((8, ATTN_HEAD_DIM), F32)
        for p in range(PAGES_PER_BLOCK):
            s = s + jnp.sum(kbuf[slot, n * PAGES_PER_BLOCK + p].reshape(PAGE_ROWS // 8, 8, ATTN_HEAD_DIM), axis=0)
        tot = s[:ATTN_N_KV_HEADS, :]
        for f in range(1, fold):
            tot = tot + s[f * ATTN_N_KV_HEADS:(f + 1) * ATTN_N_KV_HEADS, :]
        means_ref[n * ATTN_N_KV_HEADS:(n + 1) * ATTN_N_KV_HEADS, :] = tot * (1.0 / MOBA_BLOCK)
    gate = lax.dot_general(q_ref[...].astype(BF16), means_ref[...].astype(BF16), (((1,), (1,)), ((), ())),
                           preferred_element_type=F32)
    shape = gate.shape
    lane = lax.broadcasted_iota(I32, shape, 1)
    head = lax.broadcasted_iota(I32, shape, 0)
    gate = jnp.where((lane & (ATTN_N_KV_HEADS - 1)) == (head >> 2), gate, NEG_INF)
    out = jnp.zeros(shape, I32)
    for t in range(MOBA_TOPK):
        m = jnp.max(gate, axis=1, keepdims=True)
        first = jnp.min(jnp.where(gate == m, lane, shape[1]), axis=1, keepdims=True)
        out = jnp.where(lane == t, first >> 2, out)
        gate = jnp.where(lane == first, NEG_INF, gate)
    o_ref[...] = out


def sample_select(page_table, q_s, kc_pages):
    nb = q_s.shape[0]
    assert N_PAST_BLOCKS * ATTN_N_KV_HEADS == LANES
    return pl.pallas_call(
        _sample_select_kernel,
        grid_spec=pltpu.PrefetchScalarGridSpec(
            num_scalar_prefetch=1,
            grid=(nb,),
            in_specs=[pl.BlockSpec((None, ATTN_N_HEADS, ATTN_HEAD_DIM), lambda b, pt: (b, 0, 0)),
                      pl.BlockSpec(memory_space=pl.ANY)],
            out_specs=pl.BlockSpec((None, ATTN_N_HEADS, LANES), lambda b, pt: (b, 0, 0)),
            scratch_shapes=[pltpu.VMEM((2, N_PAGES, PAGE_ROWS, ATTN_HEAD_DIM), F32),
                            pltpu.VMEM((N_PAST_BLOCKS * ATTN_N_KV_HEADS, ATTN_HEAD_DIM), F32),
                            pltpu.SemaphoreType.DMA((2,))]),
        out_shape=jax.ShapeDtypeStruct((nb, ATTN_N_HEADS, LANES), I32),
        compiler_params=_params("arbitrary"),
        name="sample_select",
    )(page_table.reshape(-1), q_s, kc_pages)


def _head_rows_copy(pool_hbm, page, g, dst, sem):
    return pltpu.make_async_copy(pool_hbm.at[page, :, g, :], dst, sem)


def _sample_attend_kernel(pt_ref, sel_ref, q_ref, kn_ref, vn_ref, kc_hbm, vc_hbm, o_ref, kbuf, vbuf, sems):
    b = pl.program_id(0)
    slot = b % 2
    scale = ATTN_HEAD_DIM ** -0.5

    def fetch(seq, slot_, start):
        for h in range(ATTN_N_HEADS):
            g = h // ATTN_GROUP
            for t in range(MOBA_TOPK):
                blk = sel_ref[(seq * ATTN_N_HEADS + h) * MOBA_TOPK + t]
                for half in range(PAGES_PER_BLOCK):
                    page = pt_ref[seq * N_PAGES + blk * PAGES_PER_BLOCK + half]
                    rows = pl.ds(t * MOBA_BLOCK + half * PAGE_SIZE, PAGE_SIZE)
                    for cp in (_head_rows_copy(kc_hbm, page, g, kbuf.at[slot_, h, rows, :], sems.at[0, slot_]),
                               _head_rows_copy(vc_hbm, page, g, vbuf.at[slot_, h, rows, :], sems.at[1, slot_])):
                        cp.start() if start else cp.wait()

    @pl.when(b == 0)
    def _():
        fetch(b, slot, True)

    @pl.when(b + 1 < pl.num_programs(0))
    def _():
        fetch(b + 1, 1 - slot, True)

    fetch(b, slot, False)

    for h in range(ATTN_N_HEADS):
        g = h // ATTN_GROUP
        qh = q_ref[h:h + 1, :]
        q8 = jnp.broadcast_to(qh, (8, ATTN_HEAD_DIM)).astype(BF16)
        s = lax.dot_general(kbuf[slot, h].astype(BF16), q8, (((1,), (1,)), ((), ())),
                            preferred_element_type=F32)[:, 0:1] * scale
        s_own = jnp.sum(_mxu_round(qh) * _mxu_round(kn_ref[g:g + 1, :]), axis=1, keepdims=True) * scale
        m = jnp.maximum(jnp.max(s, axis=0, keepdims=True), s_own)
        p = jnp.exp(s - m)
        p_own = jnp.exp(s_own - m)
        denom = jnp.sum(p, axis=0, keepdims=True) + p_own
        o_sel = jnp.sum(_mxu_round(p / denom) * _mxu_round(vbuf[slot, h]), axis=0, keepdims=True)
        o_ref[h:h + 1, :] = o_sel + _mxu_round(p_own / denom) * _mxu_round(vn_ref[g:g + 1, :])


def sample_attend(page_table, sel, q_s, k_new, v_new, cache_k, cache_v):
    nb = q_s.shape[0]
    per_b = lambda rows: pl.BlockSpec((None, rows, ATTN_HEAD_DIM), lambda b, pt, sl: (b, 0, 0))
    any_spec = pl.BlockSpec(memory_space=pl.ANY)
    return pl.pallas_call(
        _sample_attend_kernel,
        grid_spec=pltpu.PrefetchScalarGridSpec(
            num_scalar_prefetch=2,
            grid=(nb,),
            in_specs=[per_b(ATTN_N_HEADS), per_b(ATTN_N_KV_HEADS), per_b(ATTN_N_KV_HEADS), any_spec, any_spec],
            out_specs=per_b(ATTN_N_HEADS),
            scratch_shapes=[pltpu.VMEM((2, ATTN_N_HEADS, SEL_KEYS, ATTN_HEAD_DIM), F32),
                            pltpu.VMEM((2, ATTN_N_HEADS, SEL_KEYS, ATTN_HEAD_DIM), F32),
                            pltpu.SemaphoreType.DMA((2, 2))]),
        out_shape=jax.ShapeDtypeStruct((nb, ATTN_N_HEADS, ATTN_HEAD_DIM), F32),
        compiler_params=_params("arbitrary"),
        name="sample_attend",
    )(page_table.reshape(-1), sel.reshape(-1), q_s, k_new, v_new, cache_k, cache_v)


def _router_kernel(x_ref, w_ref, b_ref, idx_ref, gate_ref):
    logits = jnp.dot(x_ref[...], w_ref[...].astype(BF16), preferred_element_type=F32) + b_ref[...]
    lane = lax.broadcasted_iota(I32, logits.shape, 1)
    logits = jnp.where(lane < N_EXPERTS, logits, NEG_INF)
    m1 = jnp.max(logits, axis=1, keepdims=True)
    i1 = jnp.min(jnp.where(logits == m1, lane, LANES), axis=1, keepdims=True)
    rest = jnp.where(lane == i1, NEG_INF, logits)
    m2 = jnp.max(rest, axis=1, keepdims=True)
    i2 = jnp.min(jnp.where(rest == m2, lane, LANES), axis=1, keepdims=True)
    e2 = jnp.exp(m2 - m1)
    g1 = 1.0 / (1.0 + e2)
    idx_ref[...] = jnp.where(lane == 0, i1, jnp.where(lane == 1, i2, 0))
    gate_ref[...] = jnp.where(lane == 0, g1, jnp.where(lane == 1, e2 * g1, 0.0))


def router(x, w_router, b_router):
    m, d = x.shape
    wp = jnp.zeros((d, LANES), F32).at[:, :N_EXPERTS].set(w_router)
    bp = jnp.zeros((1, LANES), F32).at[0, :N_EXPERTS].set(b_router)
    row = pl.BlockSpec((ROW_BLOCK, d), lambda i: (i, 0))
    out = pl.BlockSpec((ROW_BLOCK, LANES), lambda i: (i, 0))
    return pl.pallas_call(
        _router_kernel,
        grid=(m // ROW_BLOCK,),
        in_specs=[row, pl.BlockSpec((d, LANES), lambda i: (0, 0)), pl.BlockSpec((1, LANES), lambda i: (0, 0))],
        out_specs=[out, out],
        out_shape=[jax.ShapeDtypeStruct((m, LANES), I32), jax.ShapeDtypeStruct((m, LANES), F32)],
        compiler_params=_params("arbitrary"),
        name="router",
    )(x, wp, bp)


def _row_copy(src_hbm, row, dst, slot, sem):
    return pltpu.make_async_copy(src_hbm.at[pl.ds(row, 1), :], dst.at[pl.ds(slot, 1), :], sem)


GATHER_UNROLL = 8


def _moe_gather_kernel(tok_ref, nxt_ref, x_hbm, o_ref, buf, sems):
    i = pl.program_id(0)
    slot = i % 2

    def issue(idx_ref, slot_):
        def body(r, carry):
            _row_copy(x_hbm, idx_ref[0, 0, r], buf.at[slot_], r, sems.at[slot_]).start()
            return carry
        lax.fori_loop(0, MOE_ROWS, body, 0, unroll=GATHER_UNROLL)

    @pl.when(i == 0)
    def _():
        issue(tok_ref, slot)

    @pl.when(i + 1 < pl.num_programs(0))
    def _():
        issue(nxt_ref, 1 - slot)

    def wait(r, carry):
        _row_copy(x_hbm, 0, buf.at[slot], r, sems.at[slot]).wait()
        return carry

    lax.fori_loop(0, MOE_ROWS, wait, 0, unroll=GATHER_UNROLL)
    o_ref[...] = buf[slot].astype(o_ref.dtype)


def moe_gather(x, buf_tok):
    d = x.shape[1]
    tok = buf_tok.reshape(MOE_BLOCKS, 1, MOE_ROWS)
    idx_spec = lambda step: pl.BlockSpec((1, 1, MOE_ROWS), lambda i: (jnp.minimum(i + step, MOE_BLOCKS - 1), 0, 0),
                                         memory_space=pltpu.SMEM)
    return pl.pallas_call(
        _moe_gather_kernel,
        grid=(MOE_BLOCKS,),
        in_specs=[idx_spec(0), idx_spec(1), pl.BlockSpec(memory_space=pl.ANY)],
        out_specs=pl.BlockSpec((MOE_ROWS, d), lambda i: (i, 0)),
        out_shape=jax.ShapeDtypeStruct((MOE_BUF, d), BF16),
        scratch_shapes=[pltpu.VMEM((2, MOE_ROWS, d), F32), pltpu.SemaphoreType.DMA((2,))],
        compiler_params=_params("arbitrary"),
        name="moe_gather",
    )(tok, tok, x)


def _new_expert(be_ref, i):
    return jnp.logical_or(i == 0, be_ref[i] != be_ref[jnp.maximum(i - 1, 0)])


def _moe_gu_kernel(be_ref, nu_ref, x_ref, wg_ref, wu_ref, o_ref, wgb_ref, wub_ref):
    i = pl.program_id(1)

    @pl.when(_new_expert(be_ref, i))
    def _():
        wgb_ref[...] = wg_ref[...].astype(BF16)
        wub_ref[...] = wu_ref[...].astype(BF16)

    @pl.when(i < nu_ref[0])
    def _():
        x = x_ref[...]
        g = jnp.dot(x, wgb_ref[...], preferred_element_type=F32)
        u = jnp.dot(x, wub_ref[...], preferred_element_type=F32)
        o_ref[...] = (_silu(g) * u).astype(o_ref.dtype)

    @pl.when(i >= nu_ref[0])
    def _():
        o_ref[...] = jnp.zeros(o_ref.shape, o_ref.dtype)


def moe_gu(xs, w_gu, blk_e, n_used, *, bn=1024):
    r, d = xs.shape
    f = w_gu.shape[2] // 2
    nj = f // bn
    return pl.pallas_call(
        _moe_gu_kernel,
        grid_spec=pltpu.PrefetchScalarGridSpec(
            num_scalar_prefetch=2,
            grid=(nj, r // MOE_ROWS),
            in_specs=[pl.BlockSpec((MOE_ROWS, d), lambda j, i, be, nu: (i, 0)),
                      pl.BlockSpec((None, d, bn), lambda j, i, be, nu: (be[i], 0, j)),
                      pl.BlockSpec((None, d, bn), lambda j, i, be, nu: (be[i], 0, j + nj))],
            out_specs=pl.BlockSpec((MOE_ROWS, bn), lambda j, i, be, nu: (i, j)),
            scratch_shapes=[pltpu.VMEM((d, bn), BF16), pltpu.VMEM((d, bn), BF16)]),
        out_shape=jax.ShapeDtypeStruct((r, f), BF16),
        compiler_params=_params("arbitrary", "arbitrary"),
        name="moe_gu",
    )(blk_e, n_used, xs, w_gu, w_gu)


def _moe_down_kernel(be_ref, nu_ref, x_ref, w_ref, o_ref, wb_ref):
    i = pl.program_id(1)

    @pl.when(_new_expert(be_ref, i))
    def _():
        wb_ref[...] = w_ref[...].astype(BF16)

    @pl.when(i < nu_ref[0])
    def _():
        o_ref[...] = jnp.dot(x_ref[...], wb_ref[...], preferred_element_type=F32)

    @pl.when(i >= nu_ref[0])
    def _():
        o_ref[...] = jnp.zeros(o_ref.shape, o_ref.dtype)


def moe_down(hs, w_down, blk_e, n_used, *, bn=512):
    r, f = hs.shape
    d = w_down.shape[2]
    return pl.pallas_call(
        _moe_down_kernel,
        grid_spec=pltpu.PrefetchScalarGridSpec(
            num_scalar_prefetch=2,
            grid=(d // bn, r // MOE_ROWS),
            in_specs=[pl.BlockSpec((MOE_ROWS, f), lambda j, i, be, nu: (i, 0)),
                      pl.BlockSpec((None, f, bn), lambda j, i, be, nu: (be[i], 0, j))],
            out_specs=pl.BlockSpec((MOE_ROWS, bn), lambda j, i, be, nu: (i, j)),
            scratch_shapes=[pltpu.VMEM((f, bn), BF16)]),
        out_shape=jax.ShapeDtypeStruct((r, d), F32),
        compiler_params=_params("arbitrary", "arbitrary"),
        name="moe_down",
    )(blk_e, n_used, hs, w_down)


def _moe_combine_ln_kernel(pos_ref, nxt_ref, y_hbm, x_ref, gate_ref, g_ref, b_ref, o_ref, ob_ref, buf, sems):
    i = pl.program_id(0)
    slot = i % 2

    def issue(idx_ref, slot_):
        def body(r, carry):
            for t in range(MOE_TOPK):
                _row_copy(y_hbm, idx_ref[0, 0, MOE_TOPK * r + t], buf.at[slot_, t], r, sems.at[slot_]).start()
            return carry
        lax.fori_loop(0, ROW_BLOCK, body, 0, unroll=GATHER_UNROLL)

    @pl.when(i == 0)
    def _():
        issue(pos_ref, slot)

    @pl.when(i + 1 < pl.num_programs(0))
    def _():
        issue(nxt_ref, 1 - slot)

    def wait(r, carry):
        for t in range(MOE_TOPK):
            _row_copy(y_hbm, 0, buf.at[slot, t], r, sems.at[slot]).wait()
        return carry

    lax.fori_loop(0, ROW_BLOCK, wait, 0, unroll=GATHER_UNROLL)
    gates = gate_ref[...]
    ff = buf[slot, 0] * gates[:, 0:1]
    for t in range(1, MOE_TOPK):
        ff = ff + buf[slot, t] * gates[:, t:t + 1]
    out = _layernorm_rows(DEEPNORM_ALPHA * x_ref[...] + ff, g_ref[...], b_ref[...])
    o_ref[...] = out
    ob_ref[...] = out.astype(BF16)


def moe_combine_ln(x, ys, pos, gates, g, b):
    m, d = x.shape
    nblk = m // ROW_BLOCK
    row = pl.BlockSpec((ROW_BLOCK, d), lambda i: (i, 0))
    vec = pl.BlockSpec((1, d), lambda i: (0, 0))
    pos3 = pos.reshape(nblk, 1, MOE_TOPK * ROW_BLOCK)
    idx_spec = lambda step: pl.BlockSpec((1, 1, MOE_TOPK * ROW_BLOCK),
                                         lambda i: (jnp.minimum(i + step, nblk - 1), 0, 0), memory_space=pltpu.SMEM)
    return pl.pallas_call(
        _moe_combine_ln_kernel,
        grid=(nblk,),
        in_specs=[idx_spec(0), idx_spec(1), pl.BlockSpec(memory_space=pl.ANY),
                  row, pl.BlockSpec((ROW_BLOCK, LANES), lambda i: (i, 0)), vec, vec],
        out_specs=[row, row],
        out_shape=[jax.ShapeDtypeStruct((m, d), F32), jax.ShapeDtypeStruct((m, d), BF16)],
        scratch_shapes=[pltpu.VMEM((2, MOE_TOPK, ROW_BLOCK, d), F32), pltpu.SemaphoreType.DMA((2,))],
        compiler_params=_params("arbitrary"),
        name="moe_combine_ln",
    )(pos3, pos3, ys, x, gates, g.reshape(1, d), b.reshape(1, d))


def moe_plan(idx):
    e = idx[:N_TOK, :MOE_TOPK].reshape(N_ASSIGN)
    onehot = (e[:, None] == jnp.arange(N_EXPERTS, dtype=I32)[None, :]).astype(I32)
    before = jnp.cumsum(onehot, axis=0) - onehot
    rank = jnp.sum(before * onehot, axis=1)
    counts = jnp.sum(onehot, axis=0)
    padded = (counts + MOE_ROWS - 1) // MOE_ROWS * MOE_ROWS
    pad_end = jnp.cumsum(padded)
    pad_start = pad_end - padded
    dest = pad_start[e] + rank
    tok = jnp.arange(N_ASSIGN, dtype=I32) // MOE_TOPK
    buf_tok = jnp.zeros((MOE_BUF,), I32).at[dest].set(tok)
    starts = jnp.arange(MOE_BLOCKS, dtype=I32) * MOE_ROWS
    blk_e = jnp.minimum(jnp.searchsorted(pad_end, starts, side="right"), N_EXPERTS - 1).astype(I32)
    n_used = (pad_end[-1:] // MOE_ROWS).astype(I32)
    pos = jnp.zeros((TOK_PAD * MOE_TOPK,), I32).at[:N_ASSIGN].set(dest)
    return buf_tok, blk_e, n_used, pos


def _rope_tables():
    half = ROT_DIM // 2
    inv = ROPE_THETA ** (-jnp.arange(half, dtype=F32) * 2.0 / ROT_DIM)
    pos = jnp.concatenate([jnp.arange(SEQ, dtype=I32), jnp.full((TOK_PAD - SEQ,), PAST_LEN, I32)])
    ang = pos.astype(F32)[:, None] * inv[None, :]
    cos, sin = jnp.cos(ang), jnp.sin(ang)
    rest = LANES - ROT_DIM
    cf = jnp.concatenate([cos, cos, jnp.ones((TOK_PAD, rest), F32)], axis=1)
    s1 = jnp.concatenate([-sin, jnp.zeros((TOK_PAD, LANES - half), F32)], axis=1)
    s2 = jnp.concatenate([jnp.zeros((TOK_PAD, half), F32), sin, jnp.zeros((TOK_PAD, rest), F32)], axis=1)
    return cf, s1, s2


def _tokens(prompt, sample):
    pad = jnp.zeros((TOK_PAD - N_TOK, prompt.shape[1]), prompt.dtype)
    return jnp.concatenate([prompt, sample, pad], axis=0)


def _put_sample(buf, rows):
    tail = jnp.zeros((TOK_PAD - SEQ, buf.shape[1]), buf.dtype).at[:rows.shape[0]].set(rows.astype(buf.dtype))
    return lax.dynamic_update_slice(buf, tail, (SEQ, 0))


def kernel(x_prompt, x_sample, state_ssm, state_conv, cache_k, cache_v, page_table, p_prompt, p_sample, ln_g, ln_b, ssm_w_in, ssm_conv_w, ssm_conv_b, ssm_dt_bias, ssm_A_log, ssm_D, ssm_norm_w, ssm_w_out, attn_w_kv, attn_w_q, attn_w_o, ffn_w_gu, ffn_w_down, moe_w_router, moe_b_router, moe_w_gu, moe_w_down, ple_w_gate, ple_b_gate, ple_w_proj):
    x0 = _tokens(x_prompt[0], x_sample[:, 0])
    x0b = x0.astype(BF16)
    p_tok = [_tokens(p_prompt[i, 0], p_sample[i, :, 0]).astype(BF16) for i in range(2)]

    w_in = ssm_w_in[0]
    zx_cols = SSM_D_INNER + SSM_CONV_DIM
    zx = matmul(x0b, w_in, bn=1024, ncols=zx_cols)
    dt_raw = matmul(x0b, w_in, bn=LANES, col0=zx_cols, ncols=LANES)[:, :SSM_N_HEADS]

    xbc = conv_prompt(zx, ssm_conv_w[0], ssm_conv_b[0])
    st_conv = state_conv[0]
    x_raw_s = zx[SEQ:N_TOK, SSM_D_INNER:]
    xbc_s = conv_sample(jnp.transpose(st_conv, (1, 0, 2)), x_raw_s, ssm_conv_w[0], ssm_conv_b[0])
    xbc = lax.dynamic_update_slice(xbc, xbc_s, (SEQ, 0))
    conv_prompt_out = zx[SEQ - (SSM_CONV_W - 1):SEQ, SSM_D_INNER:].reshape(1, 1, SSM_CONV_W - 1, SSM_CONV_DIM)
    conv_sample_out = jnp.concatenate([st_conv[:, 1:], x_raw_s[:, None, :]], axis=1)[None]

    y, h_prompt = ssd_prompt(xbc, dt_raw[:SEQ], ssm_dt_bias[0], ssm_A_log[0])
    xs_s = xbc_s[:DEC_BATCH]
    gn = SSM_N_GROUPS * SSM_D_STATE
    h_sample, y_s = ssd_sample(
        state_ssm[0].reshape(DEC_BATCH, SSM_N_HEADS * SSM_HEAD_DIM, SSM_D_STATE),
        xs_s[:, :SSM_D_INNER],
        xs_s[:, SSM_D_INNER:SSM_D_INNER + gn].reshape(DEC_BATCH, SSM_N_GROUPS, SSM_D_STATE),
        xs_s[:, SSM_D_INNER + gn:].reshape(DEC_BATCH, SSM_N_GROUPS, SSM_D_STATE),
        dt_raw[SEQ:N_TOK], ssm_dt_bias[0], ssm_A_log[0])
    y = _put_sample(y, y_s.reshape(DEC_BATCH, SSM_D_INNER))
    d_lanes = jnp.repeat(ssm_D[0], SSM_HEAD_DIM).reshape(1, SSM_D_INNER)
    gated = gate_norm(y, xbc, zx, d_lanes, ssm_norm_w[0])
    mix = matmul(gated, ssm_w_out[0], bn=512)
    x1, x1b = add_ln(x0, mix, ln_g[0, 0], ln_b[0, 0])

    hid = matmul_swiglu(x1b, ffn_w_gu[0], bn=512)
    ff = matmul(hid, ffn_w_down[0], bn=256)
    x2, x2b = add_ln(x1, ff, ln_g[0, 1], ln_b[0, 1])
    x3, x3b = ple_mix(x2, x2b, p_tok[0], ple_w_gate, ple_b_gate, ple_w_proj, 0)

    rope = _rope_tables()
    k_all = matmul(x3b, attn_w_kv, bn=KV_COLS, ncols=KV_COLS, rope=rope)
    v_all = matmul(x3b, attn_w_kv, bn=KV_COLS, col0=KV_COLS, ncols=KV_COLS)
    q_all = matmul(x3b, attn_w_q[0], bn=1024, rope=rope)
    att = moba_prompt(q_all, k_all, v_all, block_means(k_all).reshape(SEQ // MOBA_BLOCK, KV_COLS))

    q_s = q_all[SEQ:N_TOK].reshape(DEC_BATCH, ATTN_N_HEADS, ATTN_HEAD_DIM)
    k_s = k_all[SEQ:N_TOK].reshape(DEC_BATCH, ATTN_N_KV_HEADS, ATTN_HEAD_DIM)
    v_s = v_all[SEQ:N_TOK].reshape(DEC_BATCH, ATTN_N_KV_HEADS, ATTN_HEAD_DIM)
    n_pool = cache_k.shape[0]
    sel = sample_select(page_table, q_s, cache_k.reshape(n_pool, PAGE_ROWS, ATTN_HEAD_DIM))[:, :, :MOBA_TOPK]
    att_s = sample_attend(page_table, sel, q_s, k_s, v_s, cache_k, cache_v)
    att = _put_sample(att, att_s.reshape(DEC_BATCH, D_MODEL))
    mix = matmul(att, attn_w_o[0], bn=1024)
    x4, x4b = add_ln(x3, mix, ln_g[1, 0], ln_b[1, 0])

    idx, gates = router(x4b, moe_w_router[0], moe_b_router[0])
    buf_tok, blk_e, n_used, pos = moe_plan(idx)
    xs = moe_gather(x4, buf_tok)
    hs = moe_gu(xs, moe_w_gu[0], blk_e, n_used)
    ys = moe_down(hs, moe_w_down[0], blk_e, n_used)
    x5, x5b = moe_combine_ln(x4, ys, pos, gates, ln_g[1, 1], ln_b[1, 1])
    y_prompt, y_sample = ple_mix(x5, x5b, p_tok[1], ple_w_gate, ple_b_gate, ple_w_proj, 1, final=True)

    state_shape = (1, -1, SSM_N_HEADS, SSM_HEAD_DIM, SSM_D_STATE)
    kv_shape = (-1, ATTN_N_KV_HEADS, ATTN_HEAD_DIM)
    return (y_prompt.reshape(1, SEQ, D_MODEL),
            y_sample.reshape(DEC_BATCH, 1, D_MODEL),
            h_prompt.reshape(state_shape),
            conv_prompt_out,
            k_all[:SEQ].reshape((1,) + (SEQ,) + kv_shape[1:]),
            v_all[:SEQ].reshape((1,) + (SEQ,) + kv_shape[1:]),
            h_sample.reshape(state_shape),
            conv_sample_out,
            k_all[SEQ:N_TOK].reshape((DEC_BATCH, 1) + kv_shape[1:]),
            v_all[SEQ:N_TOK].reshape((DEC_BATCH, 1) + kv_shape[1:]))
```

```python
import functools

import jax
import jax.numpy as jnp
import numpy as np
from jax import lax
from jax.experimental import pallas as pl
from jax.experimental.pallas import tpu as pltpu

F32 = jnp.float32
BF16 = jnp.bfloat16
I32 = jnp.int32

D_MODEL = 2048
SEQ = 8192
DEC_BATCH = 32
PAST_LEN = 8192
PAGE_SIZE = 128

SSM_D_INNER = 4096
SSM_HEAD_DIM = 64
SSM_N_HEADS = 64
SSM_N_GROUPS = 8
SSM_D_STATE = 128
SSM_CONV_W = 4
SSM_CHUNK = 128
SSM_CONV_DIM = 6144
SSM_HEADS_PER_GROUP = SSM_N_HEADS // SSM_N_GROUPS

ATTN_HEAD_DIM = 128
ATTN_N_HEADS = 16
ATTN_N_KV_HEADS = 4
ATTN_GROUP = 4
ROT_DIM = 32
ROPE_THETA = 500000.0
MOBA_BLOCK = 256
MOBA_TOPK = 3
N_PAST_BLOCKS = PAST_LEN // MOBA_BLOCK
PAGES_PER_BLOCK = MOBA_BLOCK // PAGE_SIZE
N_PAGES = PAST_LEN // PAGE_SIZE

D_FF = 5632
N_EXPERTS = 8
MOE_TOPK = 2
D_FF_EXPERT = 7168
D_PLE = 256
DEEPNORM_ALPHA = 4.0 ** 0.25
LN_EPS = 1e-5
RMS_EPS = 1e-5

LANES = 128
VMEM_LIMIT_BYTES = 56 * 1024 * 1024

N_TOK = SEQ + DEC_BATCH
TOK_PAD = 8448
MM_ROWS = 768
ROW_BLOCK = 256
MOE_ROWS = 256
N_ASSIGN = N_TOK * MOE_TOPK
MOE_BLOCKS = -(-N_ASSIGN // MOE_ROWS) + N_EXPERTS
MOE_BUF = MOE_BLOCKS * MOE_ROWS
NEG_INF = float("-inf")


def _params(*sem):
    return pltpu.CompilerParams(dimension_semantics=sem, vmem_limit_bytes=VMEM_LIMIT_BYTES)


def _split3(a):
    hi = a.astype(BF16)
    r1 = a - hi.astype(F32)
    mid = r1.astype(BF16)
    lo = (r1 - mid.astype(F32)).astype(BF16)
    return hi, mid, lo


def _dot_f32(a, b, dims):
    a0, a1, a2 = _split3(a)
    b0, b1, b2 = _split3(b)
    dn = (dims, ((), ()))
    d = lambda x, y: lax.dot_general(x, y, dn, preferred_element_type=F32)
    return (d(a0, b0) + (d(a0, b1) + d(a1, b0))) + ((d(a0, b2) + d(a2, b0)) + d(a1, b1))


def _softplus(x):
    return jnp.maximum(x, 0.0) + jnp.log1p(jnp.exp(-jnp.abs(x)))


def _silu(x):
    return x * jax.nn.sigmoid(x)


def _rope_lanes(a, cf, s1, s2):
    return a * cf + pltpu.roll(a, LANES - ROT_DIM // 2, 1) * s1 + pltpu.roll(a, ROT_DIM // 2, 1) * s2


def _mm_kernel(x_ref, w_ref, *rest, rope, valid_cols, w_transposed):
    if rope:
        cf_ref, s1_ref, s2_ref, o_ref, wb_ref = rest
    else:
        o_ref, wb_ref = rest

    @pl.when(pl.program_id(1) == 0)
    def _():
        w = w_ref[...]
        if valid_cols is not None:
            axis = 0 if w_transposed else 1
            w = jnp.where(lax.broadcasted_iota(I32, w.shape, axis) < valid_cols, w, 0.0)
        wb_ref[...] = (w.T if w_transposed else w).astype(BF16)

    acc = jnp.dot(x_ref[...], wb_ref[...], preferred_element_type=F32)
    if rope:
        cf, s1, s2 = cf_ref[...], s1_ref[...], s2_ref[...]
        for c in range(acc.shape[1] // LANES):
            sl = slice(c * LANES, (c + 1) * LANES)
            o_ref[:, sl] = _rope_lanes(acc[:, sl], cf, s1, s2).astype(o_ref.dtype)
    else:
        o_ref[...] = acc.astype(o_ref.dtype)


def matmul(x, w, *, bn, col0=0, ncols=None, rope=None, out_dtype=F32, bm=MM_ROWS, w_transposed=False):
    m, k = x.shape
    n_total = w.shape[0] if w_transposed else w.shape[1]
    ncols = n_total if ncols is None else ncols
    assert m % bm == 0 and ncols % bn == 0 and col0 % bn == 0
    j0 = col0 // bn
    valid_cols = None
    if col0 + ncols > n_total:
        assert ncols == bn
        valid_cols = n_total - col0
    w_spec = (pl.BlockSpec((bn, k), lambda j, i: (j + j0, 0)) if w_transposed
              else pl.BlockSpec((k, bn), lambda j, i: (0, j + j0)))
    in_specs = [pl.BlockSpec((bm, k), lambda j, i: (i, 0)), w_spec]
    args = [x, w]
    if rope is not None:
        in_specs += [pl.BlockSpec((bm, LANES), lambda j, i: (i, 0))] * 3
        args += list(rope)
    return pl.pallas_call(
        functools.partial(_mm_kernel, rope=rope is not None, valid_cols=valid_cols, w_transposed=w_transposed),
        grid=(ncols // bn, m // bm),
        in_specs=in_specs,
        out_specs=pl.BlockSpec((bm, bn), lambda j, i: (i, j)),
        out_shape=jax.ShapeDtypeStruct((m, ncols), out_dtype),
        scratch_shapes=[pltpu.VMEM((k, bn), BF16)],
        compiler_params=_params("arbitrary", "arbitrary"),
        name="matmul",
    )(*args)


def _mm_swiglu_kernel(x_ref, wg_ref, wu_ref, o_ref, wgb_ref, wub_ref):
    @pl.when(pl.program_id(1) == 0)
    def _():
        wgb_ref[...] = wg_ref[...].astype(BF16)
        wub_ref[...] = wu_ref[...].astype(BF16)

    x = x_ref[...]
    g = jnp.dot(x, wgb_ref[...], preferred_element_type=F32)
    u = jnp.dot(x, wub_ref[...], preferred_element_type=F32)
    o_ref[...] = (_silu(g) * u).astype(o_ref.dtype)


def matmul_swiglu(x, w_gu, *, bn, bm=MM_ROWS):
    m, k = x.shape
    f = w_gu.shape[1] // 2
    assert m % bm == 0 and f % bn == 0
    nj = f // bn
    return pl.pallas_call(
        _mm_swiglu_kernel,
        grid=(nj, m // bm),
        in_specs=[pl.BlockSpec((bm, k), lambda j, i: (i, 0)),
                  pl.BlockSpec((k, bn), lambda j, i: (0, j)),
                  pl.BlockSpec((k, bn), lambda j, i: (0, j + nj))],
        out_specs=pl.BlockSpec((bm, bn), lambda j, i: (i, j)),
        out_shape=jax.ShapeDtypeStruct((m, f), BF16),
        scratch_shapes=[pltpu.VMEM((k, bn), BF16), pltpu.VMEM((k, bn), BF16)],
        compiler_params=_params("arbitrary", "arbitrary"),
        name="matmul_swiglu",
    )(x, w_gu, w_gu)


def _ple_kernel(xb_ref, pb_ref, wg_ref, wp_ref, bg_ref, x_ref, o_ref, o2_ref, wgb_ref, wpb_ref, *, final):
    @pl.when(pl.program_id(1) == 0)
    def _():
        wgb_ref[...] = wg_ref[...].astype(BF16)
        wpb_ref[...] = wp_ref[...].astype(BF16)

    gate = jax.nn.sigmoid(jnp.dot(xb_ref[...], wgb_ref[...], preferred_element_type=F32) + bg_ref[...])
    proj = jnp.dot(pb_ref[...], wpb_ref[...], preferred_element_type=F32)
    out = x_ref[...] + gate * proj
    o_ref[...] = out
    if final:
        tail0 = SEQ - (TOK_PAD - MM_ROWS)

        @pl.when(pl.program_id(1) == pl.num_programs(1) - 1)
        def _():
            o2_ref[...] = out[tail0:tail0 + DEC_BATCH, :]
    else:
        o2_ref[...] = out.astype(BF16)


def ple_mix(x, xb, pb, w_gate, b_gate, w_proj, layer, *, final=False, bn=1024, bm=MM_ROWS):
    m, d = x.shape
    kp = pb.shape[1]
    full = pl.BlockSpec((bm, bn), lambda j, i: (i, j))
    if final:
        assert m == TOK_PAD and bm == MM_ROWS and 0 <= SEQ - (TOK_PAD - MM_ROWS) <= MM_ROWS - DEC_BATCH
        out_specs = [full, pl.BlockSpec((DEC_BATCH, bn), lambda j, i: (0, j))]
        out_shape = [jax.ShapeDtypeStruct((SEQ, d), F32), jax.ShapeDtypeStruct((DEC_BATCH, d), F32)]
    else:
        out_specs = [full, full]
        out_shape = [jax.ShapeDtypeStruct((m, d), F32), jax.ShapeDtypeStruct((m, d), BF16)]
    return pl.pallas_call(
        functools.partial(_ple_kernel, final=final),
        grid=(d // bn, m // bm),
        in_specs=[pl.BlockSpec((bm, d), lambda j, i: (i, 0)),
                  pl.BlockSpec((bm, kp), lambda j, i: (i, 0)),
                  pl.BlockSpec((None, d, bn), lambda j, i: (layer, 0, j)),
                  pl.BlockSpec((None, kp, bn), lambda j, i: (layer, 0, j)),
                  pl.BlockSpec((None, 1, bn), lambda j, i: (layer, 0, j)),
                  full],
        out_specs=out_specs,
        out_shape=out_shape,
        scratch_shapes=[pltpu.VMEM((d, bn), BF16), pltpu.VMEM((kp, bn), BF16)],
        compiler_params=_params("arbitrary", "arbitrary"),
        name="ple_mix",
    )(xb, pb, w_gate, w_proj, b_gate.reshape(b_gate.shape[0], 1, d), x)


def _layernorm_rows(v, g, b):
    mu = jnp.mean(v, axis=-1, keepdims=True)
    c = v - mu
    var = jnp.mean(c * c, axis=-1, keepdims=True)
    return c * lax.rsqrt(var + LN_EPS) * g + b


def _add_ln_kernel(x_ref, m_ref, g_ref, b_ref, o_ref, ob_ref):
    out = _layernorm_rows(DEEPNORM_ALPHA * x_ref[...] + m_ref[...], g_ref[...], b_ref[...])
    o_ref[...] = out
    ob_ref[...] = out.astype(BF16)


def add_ln(x, mix, g, b):
    m, d = x.shape
    row = pl.BlockSpec((ROW_BLOCK, d), lambda i: (i, 0))
    vec = pl.BlockSpec((1, d), lambda i: (0, 0))
    return pl.pallas_call(
        _add_ln_kernel,
        grid=(m // ROW_BLOCK,),
        in_specs=[row, row, vec, vec],
        out_specs=[row, row],
        out_shape=[jax.ShapeDtypeStruct((m, d), F32), jax.ShapeDtypeStruct((m, d), BF16)],
        compiler_params=_params("arbitrary"),
        name="add_ln",
    )(x, mix, g.reshape(1, d), b.reshape(1, d))


def _gate_norm_kernel(y_ref, z_ref, w_ref, o_ref):
    g = y_ref[...] * _silu(z_ref[...])
    ms = jnp.mean(g * g, axis=-1, keepdims=True)
    o_ref[...] = (g * lax.rsqrt(ms + RMS_EPS) * w_ref[...]).astype(o_ref.dtype)


def gate_norm(y, zx, norm_w):
    m, di = y.shape
    row = pl.BlockSpec((ROW_BLOCK, di), lambda i: (i, 0))
    return pl.pallas_call(
        _gate_norm_kernel,
        grid=(m // ROW_BLOCK,),
        in_specs=[row, row, pl.BlockSpec((1, di), lambda i: (0, 0))],
        out_specs=row,
        out_shape=jax.ShapeDtypeStruct((m, di), BF16),
        compiler_params=_params("arbitrary"),
        name="gate_norm",
    )(y, zx, norm_w.reshape(1, di))


CONV_COLS = 2048


def _conv_prompt_kernel(x_ref, h_ref, w_ref, b_ref, o_ref):
    x = x_ref[...]
    halo = jnp.where(pl.program_id(0) == 0, 0.0, h_ref[...])
    w = w_ref[...]
    acc = b_ref[...] + x * w[SSM_CONV_W - 1:SSM_CONV_W, :]
    rows8 = lax.broadcasted_iota(I32, halo.shape, 0)
    for s in range(1, SSM_CONV_W):
        xr = pltpu.roll(x, s, 0)
        head = jnp.where(rows8 < s, pltpu.roll(halo, s, 0), xr[:8, :])
        shifted = jnp.concatenate([head, xr[8:, :]], axis=0)
        acc = acc + shifted * w[SSM_CONV_W - 1 - s:SSM_CONV_W - s, :]
    o_ref[...] = _silu(acc)


def conv_prompt(zx, conv_w, conv_b):
    c0 = SSM_D_INNER // CONV_COLS
    rb8 = MM_ROWS // 8
    return pl.pallas_call(
        _conv_prompt_kernel,
        grid=(TOK_PAD // MM_ROWS, SSM_CONV_DIM // CONV_COLS),
        in_specs=[pl.BlockSpec((MM_ROWS, CONV_COLS), lambda i, k: (i, c0 + k)),
                  pl.BlockSpec((8, CONV_COLS), lambda i, k: (jnp.maximum(i * rb8 - 1, 0), c0 + k)),
                  pl.BlockSpec((SSM_CONV_W, CONV_COLS), lambda i, k: (0, k)),
                  pl.BlockSpec((1, CONV_COLS), lambda i, k: (0, k))],
        out_specs=pl.BlockSpec((MM_ROWS, CONV_COLS), lambda i, k: (i, k)),
        out_shape=jax.ShapeDtypeStruct((TOK_PAD, SSM_CONV_DIM), F32),
        compiler_params=_params("arbitrary", "arbitrary"),
        name="conv_prompt",
    )(zx, zx, conv_w, conv_b.reshape(1, SSM_CONV_DIM))


def _conv_sample_kernel(st_ref, x_ref, w_ref, b_ref, o_ref):
    w = w_ref[...]
    acc = b_ref[...] + x_ref[...] * w[SSM_CONV_W - 1:SSM_CONV_W, :]
    for j in range(SSM_CONV_W - 1):
        acc = acc + st_ref[j] * w[j:j + 1, :]
    o_ref[...] = jnp.zeros(o_ref.shape, F32)
    o_ref[:DEC_BATCH, :] = _silu(acc)


def conv_sample(st, x_raw, conv_w, conv_b):
    return pl.pallas_call(
        _conv_sample_kernel,
        grid=(SSM_CONV_DIM // CONV_COLS,),
        in_specs=[pl.BlockSpec((SSM_CONV_W - 1, DEC_BATCH, CONV_COLS), lambda k: (0, 0, k)),
                  pl.BlockSpec((DEC_BATCH, CONV_COLS), lambda k: (0, k)),
                  pl.BlockSpec((SSM_CONV_W, CONV_COLS), lambda k: (0, k)),
                  pl.BlockSpec((1, CONV_COLS), lambda k: (0, k))],
        out_specs=pl.BlockSpec((TOK_PAD - SEQ, CONV_COLS), lambda k: (0, k)),
        out_shape=jax.ShapeDtypeStruct((TOK_PAD - SEQ, SSM_CONV_DIM), F32),
        compiler_params=_params("arbitrary"),
        name="conv_sample",
    )(st, x_raw, conv_w, conv_b.reshape(1, SSM_CONV_DIM))


SSD_GROUP_COLS = SSM_HEADS_PER_GROUP * SSM_HEAD_DIM
SSD_PAIR_ROWS = 2 * SSM_HEAD_DIM


def _ssd_prompt_kernel(*refs):
    y_ref = refs[-2]

    @pl.when(pl.program_id(0) < SEQ // SSM_CHUNK)
    def _():
        _ssd_prompt_chunk(*refs)

    @pl.when(pl.program_id(0) >= SEQ // SSM_CHUNK)
    def _():
        y_ref[...] = jnp.zeros(y_ref.shape, y_ref.dtype)


def _ssd_prompt_chunk(xs_ref, b_ref, c_ref, dtc_ref, dtr_ref, bc_ref, br_ref, ac_ref, ar_ref, d_ref,
                      y_ref, h_ref):
    q = SSM_CHUNK
    dt_c = _softplus(dtc_ref[...] + bc_ref[...])
    dt_r = _softplus(dtr_ref[...] + br_ref[...])
    da_c = dt_c * -jnp.exp(ac_ref[...])
    da_r = dt_r * -jnp.exp(ar_ref[...])
    ti = lax.broadcasted_iota(I32, (q, q), 0)
    si = lax.broadcasted_iota(I32, (q, q), 1)
    causal = ti >= si
    tril = causal.astype(F32)
    triu = (ti <= si).astype(F32)
    acs_c = _dot_f32(tril, da_c, ((1,), (0,)))
    acs_r = _dot_f32(da_r, triu, ((1,), (0,)))
    last = acs_c[q - 1:q, :]

    @pl.when(pl.program_id(0) == 0)
    def _():
        h_ref[...] = jnp.zeros(h_ref.shape, F32)

    lane = lax.broadcasted_iota(I32, (q, LANES), 1)
    lo = lane < SSM_HEAD_DIM
    sub = lax.broadcasted_iota(I32, (SSD_PAIR_ROWS, 1), 0)
    for g in range(SSM_N_GROUPS):
        n_cols = slice(g * SSM_D_STATE, (g + 1) * SSM_D_STATE)
        bm = b_ref[:, n_cols].astype(BF16)
        cm = c_ref[:, n_cols].astype(BF16)
        cb = lax.dot_general(cm, bm, (((1,), (1,)), ((), ())), preferred_element_type=F32)
        for pair in range(SSM_HEADS_PER_GROUP // 2):
            r1 = g * SSM_HEADS_PER_GROUP + 2 * pair
            r2 = r1 + 1
            cols = slice(r1 * SSM_HEAD_DIM, (r2 + 1) * SSM_HEAD_DIM)
            x = xs_ref[:, cols]
            xdt = x * jnp.where(lo, dt_c[:, r1:r1 + 1], dt_c[:, r2:r2 + 1])
            ms = []
            for r in (r1, r2):
                seg = acs_c[:, r:r + 1] - acs_r[r:r + 1, :]
                ms.append((cb * jnp.exp(jnp.where(causal, seg, NEG_INF))).astype(BF16))
            y = (jnp.dot(ms[0], jnp.where(lo, xdt, 0.0).astype(BF16), preferred_element_type=F32)
                 + jnp.dot(ms[1], jnp.where(lo, 0.0, xdt).astype(BF16), preferred_element_type=F32))

            h_in = h_ref[cols, :]
            y_off = lax.dot_general(cm, h_in.astype(BF16), (((1,), (1,)), ((), ())),
                                    preferred_element_type=F32)
            y = y + y_off * jnp.exp(jnp.where(lo, acs_c[:, r1:r1 + 1], acs_c[:, r2:r2 + 1]))
            y_ref[:, cols] = y + d_ref[:, cols] * x

            to_end = jnp.exp(jnp.where(lo, last[:, r1:r1 + 1] - acs_c[:, r1:r1 + 1],
                                       last[:, r2:r2 + 1] - acs_c[:, r2:r2 + 1]))
            xw_t = (xdt * to_end).T.astype(BF16)
            st = jnp.dot(xw_t, bm, preferred_element_type=F32)
            decay = jnp.exp(jnp.where(sub < SSM_HEAD_DIM, last[:, r1:r1 + 1], last[:, r2:r2 + 1]))
            h_ref[cols, :] = h_in * decay + st


def ssd_prompt(xbc, dt_raw, dt_bias, a_log, d_lanes):
    nc = SEQ // SSM_CHUNK
    nh = SSM_N_HEADS
    gn = SSM_N_GROUPS * SSM_D_STATE
    clamp = lambda c: jnp.minimum(c, nc - 1)
    col_vec = pl.BlockSpec((1, nh), lambda c: (0, 0))
    row_vec = pl.BlockSpec((nh, 1), lambda c: (0, 0))
    return pl.pallas_call(
        _ssd_prompt_kernel,
        grid=(TOK_PAD // SSM_CHUNK,),
        in_specs=[pl.BlockSpec((SSM_CHUNK, SSM_D_INNER), lambda c: (c, 0)),
                  pl.BlockSpec((SSM_CHUNK, gn), lambda c: (c, SSM_D_INNER // gn)),
                  pl.BlockSpec((SSM_CHUNK, gn), lambda c: (c, SSM_D_INNER // gn + 1)),
                  pl.BlockSpec((SSM_CHUNK, nh), lambda c: (clamp(c), 0)),
                  pl.BlockSpec((nh, SSM_CHUNK), lambda c: (0, clamp(c))),
                  col_vec, row_vec, col_vec, row_vec, pl.BlockSpec((1, SSM_D_INNER), lambda c: (0, 0))],
        out_specs=[pl.BlockSpec((SSM_CHUNK, SSM_D_INNER), lambda c: (c, 0)),
                   pl.BlockSpec((nh * SSM_HEAD_DIM, SSM_D_STATE), lambda c: (0, 0))],
        out_shape=[jax.ShapeDtypeStruct((TOK_PAD, SSM_D_INNER), F32),
                   jax.ShapeDtypeStruct((nh * SSM_HEAD_DIM, SSM_D_STATE), F32)],
        compiler_params=_params("arbitrary"),
        name="ssd_prompt",
    )(xbc, xbc, xbc, dt_raw, dt_raw.T,
      dt_bias.reshape(1, nh), dt_bias.reshape(nh, 1), a_log.reshape(1, nh), a_log.reshape(nh, 1), d_lanes)


def _mxu_round(x):
    return x.astype(BF16).astype(F32)


def _ssd_sample_kernel(h0_ref, xt_ref, xl_ref, b_ref, c_ref, dt_ref, dtl_ref, bias_ref, biasl_ref,
                       alog_ref, alogl_ref, d_ref, h_ref, y_ref):
    dt = _softplus(dt_ref[...] + bias_ref[...])
    dec = jnp.exp(dt * -jnp.exp(alog_ref[...]))
    dt_l = _softplus(dtl_ref[...] + biasl_ref[...])
    dec_l = jnp.exp(dt_l * -jnp.exp(alogl_ref[...]))
    xdt_t = _mxu_round(xt_ref[...] * dt)
    xdt_l = _mxu_round(xl_ref[...] * dt_l)
    bm = _mxu_round(b_ref[...])
    cm = _mxu_round(c_ref[...])
    for grp in range(SSM_N_GROUPS):
        b_row = bm[grp:grp + 1, :]
        c_row = cm[grp:grp + 1, :]
        cb = _mxu_round(jnp.sum(c_row * b_row, axis=1, keepdims=True))
        old = []
        for r in range(SSM_HEADS_PER_GROUP):
            h = grp * SSM_HEADS_PER_GROUP + r
            rows = slice(h * SSM_HEAD_DIM, (h + 1) * SSM_HEAD_DIM)
            h0 = h0_ref[rows, :]
            h_ref[rows, :] = h0 * dec[:, h:h + 1] + xdt_t[:, h:h + 1] * b_row
            old.append(h0)
        h0_grp = jnp.concatenate(old, axis=0).astype(BF16)
        c_rows = jnp.broadcast_to(c_row, (8, SSM_D_STATE)).astype(BF16)
        ch0 = lax.dot_general(c_rows, h0_grp, (((1,), (1,)), ((), ())), preferred_element_type=F32)[0:1, :]
        cols = slice(grp * SSD_GROUP_COLS, (grp + 1) * SSD_GROUP_COLS)
        y_ref[:, cols] = (cb * xdt_l[:, cols] + ch0 * dec_l[:, cols]) + d_ref[:, cols] * xl_ref[:, cols]


def ssd_sample(h0, x, bm, cm, dt_raw, dt_bias, a_log, d_lanes):
    nb = h0.shape[0]
    nh, p = SSM_N_HEADS, SSM_HEAD_DIM
    hp = nh * p
    lanes = lambda v: jnp.repeat(v, p, axis=-1)
    x_t = jnp.transpose(x.reshape(nb, nh, p), (0, 2, 1))
    per_b = lambda *shape: pl.BlockSpec((None,) + shape, lambda b: (b,) + (0,) * len(shape))
    vec = lambda n: pl.BlockSpec((1, n), lambda b: (0, 0))
    return pl.pallas_call(
        _ssd_sample_kernel,
        grid=(nb,),
        in_specs=[per_b(hp, SSM_D_STATE), per_b(p, nh), per_b(1, hp),
                  per_b(SSM_N_GROUPS, SSM_D_STATE), per_b(SSM_N_GROUPS, SSM_D_STATE),
                  per_b(1, nh), per_b(1, hp), vec(nh), vec(hp), vec(nh), vec(hp), vec(hp)],
        out_specs=[per_b(hp, SSM_D_STATE), per_b(1, hp)],
        out_shape=[jax.ShapeDtypeStruct((nb, hp, SSM_D_STATE), F32),
                   jax.ShapeDtypeStruct((nb, 1, hp), F32)],
        compiler_params=_params("arbitrary"),
        name="ssd_sample",
    )(h0, x_t, x.reshape(nb, 1, hp), bm, cm, dt_raw.reshape(nb, 1, nh), lanes(dt_raw).reshape(nb, 1, hp),
      dt_bias.reshape(1, nh), lanes(dt_bias).reshape(1, hp), a_log.reshape(1, nh), lanes(a_log).reshape(1, hp),
      d_lanes)


KV_COLS = ATTN_N_KV_HEADS * ATTN_HEAD_DIM
Q_LANES = ATTN_GROUP * MOBA_BLOCK


def _kmeans_kernel(k_ref, o_ref):
    o_ref[...] = jnp.mean(k_ref[...], axis=0, keepdims=True).reshape(o_ref.shape)


def block_means(k):
    nb = SEQ // MOBA_BLOCK
    return pl.pallas_call(
        _kmeans_kernel,
        grid=(nb,),
        in_specs=[pl.BlockSpec((MOBA_BLOCK, KV_COLS), lambda i: (i, 0))],
        out_specs=pl.BlockSpec((None, 1, KV_COLS), lambda i: (i, 0, 0)),
        out_shape=jax.ShapeDtypeStruct((nb, 1, KV_COLS), F32),
        compiler_params=_params("arbitrary"),
        name="block_means",
    )(k)


def _top3_mask_rows(gate):
    n = gate.shape[0]
    rows = lax.broadcasted_iota(I32, gate.shape, 0)
    sel = jnp.zeros(gate.shape, jnp.bool_)
    for _ in range(MOBA_TOPK):
        m = jnp.max(gate, axis=0, keepdims=True)
        first = jnp.min(jnp.where(gate == m, rows, n), axis=0, keepdims=True)
        hit = jnp.logical_and(rows == first, m > NEG_INF)
        sel = jnp.logical_or(sel, hit)
        gate = jnp.where(rows == first, NEG_INF, gate)
    return sel


MASKED_SCORE = -1e30
LOG2_E = 1.4426950408889634


def _moba_prompt_kernel(qi_ref, jj_ref, q_ref, k_ref, v_ref, mean_ref, o_ref,
                        qaug_ref, s_ref, m_ref, l_ref, acc_ref):
    t = pl.program_id(0)
    i = qi_ref[t]
    j = jj_ref[t]
    nb = mean_ref.shape[0]
    q_scale = ATTN_HEAD_DIM ** -0.5 * LOG2_E
    hd = ATTN_HEAD_DIM

    @pl.when(i == nb)
    def _():
        o_ref[...] = jnp.zeros(o_ref.shape, o_ref.dtype)

    @pl.when(jnp.logical_and(i < nb, j == 0))
    def _():
        blk = lax.broadcasted_iota(I32, (nb, Q_LANES), 0)
        pad = jnp.zeros((LANES - nb, MOBA_BLOCK), F32)
        for g in range(ATTN_N_KV_HEADS):
            q_g = jnp.concatenate([q_ref[:, (ATTN_GROUP * g + r) * hd:(ATTN_GROUP * g + r + 1) * hd]
                                   for r in range(ATTN_GROUP)], axis=0)
            gate = lax.dot_general(mean_ref[:, g * hd:(g + 1) * hd].astype(BF16), q_g.astype(BF16),
                                   (((1,), (1,)), ((), ())), preferred_element_type=F32)
            sel = _top3_mask_rows(jnp.where(blk < i, gate, NEG_INF))
            bias = jnp.where(jnp.logical_or(sel, blk == i), 0.0, MASKED_SCORE)
            bias_q = jnp.concatenate(
                [jnp.concatenate([bias[:, r * MOBA_BLOCK:(r + 1) * MOBA_BLOCK], pad], axis=0).T
                 for r in range(ATTN_GROUP)], axis=0)
            qaug_ref[g] = jnp.concatenate([(q_g * q_scale).astype(BF16), bias_q.astype(BF16)], axis=1)
        m_ref[...] = jnp.full(m_ref.shape, NEG_INF, F32)
        l_ref[...] = jnp.zeros(l_ref.shape, F32)
        acc_ref[...] = jnp.zeros(acc_ref.shape, F32)

    def attend(own):
        blk_lane = lax.broadcasted_iota(I32, (MOBA_BLOCK, LANES), 1)
        onehot = jnp.where(blk_lane == i - j, 1.0, 0.0).astype(BF16)
        for g in range(ATTN_N_KV_HEADS):
            k_aug = jnp.concatenate([k_ref[:, g * hd:(g + 1) * hd].astype(BF16), onehot], axis=1)
            s_ref[g] = lax.dot_general(k_aug, qaug_ref[g], (((1,), (1,)), ((), ())), preferred_element_type=F32)
        if own:
            key = lax.broadcasted_iota(I32, (MOBA_BLOCK, Q_LANES), 0)
            qry = lax.broadcasted_iota(I32, (MOBA_BLOCK, Q_LANES), 1) & (MOBA_BLOCK - 1)
            causal = key <= qry
        for g in range(ATTN_N_KV_HEADS):
            s = s_ref[g]
            if own:
                s = jnp.where(causal, s, NEG_INF)
            m_old = m_ref[g]
            m_new = jnp.maximum(m_old, jnp.max(s, axis=0, keepdims=True))
            alpha = jnp.exp2(m_old - m_new)
            p = jnp.exp2(s - m_new)
            l_ref[g] = alpha * l_ref[g] + jnp.sum(p, axis=0, keepdims=True)
            m_ref[g] = m_new
            v_t = v_ref[:, g * hd:(g + 1) * hd].T.astype(BF16)
            acc_ref[g] = alpha * acc_ref[g] + jnp.dot(v_t, p.astype(BF16), preferred_element_type=F32)

    @pl.when(jnp.logical_and(i < nb, j == 0))
    def _():
        attend(True)

    @pl.when(jnp.logical_and(i < nb, j > 0))
    def _():
        attend(False)

    @pl.when(jnp.logical_and(i < nb, j == i))
    def _():
        for g in range(ATTN_N_KV_HEADS):
            out = (acc_ref[g] / l_ref[g]).T
            for r in range(ATTN_GROUP):
                h = ATTN_GROUP * g + r
                o_ref[:, h * hd:(h + 1) * hd] = out[r * MOBA_BLOCK:(r + 1) * MOBA_BLOCK, :].astype(o_ref.dtype)


def moba_prompt(q, k, v, means):
    nb = SEQ // MOBA_BLOCK
    assert nb <= LANES
    steps = [(i, j) for i in range(nb) for j in range(i + 1)] + [(nb, 0)]
    qi = jnp.asarray(np.array([s[0] for s in steps], np.int32))
    jj = jnp.asarray(np.array([s[1] for s in steps], np.int32))
    kv_spec = pl.BlockSpec((MOBA_BLOCK, KV_COLS), lambda t, qi, jj: (jnp.minimum(qi[t] - jj[t], nb - 1), 0))
    row_spec = pl.BlockSpec((MOBA_BLOCK, D_MODEL), lambda t, qi, jj: (qi[t], 0))
    return pl.pallas_call(
        _moba_prompt_kernel,
        grid_spec=pltpu.PrefetchScalarGridSpec(
            num_scalar_prefetch=2,
            grid=(len(steps),),
            in_specs=[row_spec, kv_spec, kv_spec, pl.BlockSpec((nb, KV_COLS), lambda t, qi, jj: (0, 0))],
            out_specs=row_spec,
            scratch_shapes=[pltpu.VMEM((ATTN_N_KV_HEADS, Q_LANES, ATTN_HEAD_DIM + LANES), BF16),
                            pltpu.VMEM((ATTN_N_KV_HEADS, MOBA_BLOCK, Q_LANES), F32),
                            pltpu.VMEM((ATTN_N_KV_HEADS, 1, Q_LANES), F32),
                            pltpu.VMEM((ATTN_N_KV_HEADS, 1, Q_LANES), F32),
                            pltpu.VMEM((ATTN_N_KV_HEADS, ATTN_HEAD_DIM, Q_LANES), F32)]),
        out_shape=jax.ShapeDtypeStruct((TOK_PAD, D_MODEL), BF16),
        compiler_params=_params("arbitrary"),
        name="moba_prompt",
    )(qi, jj, q, k, v, means)


PAGE_ROWS = PAGE_SIZE * ATTN_N_KV_HEADS
SEL_KEYS = MOBA_TOPK * MOBA_BLOCK


def _page_copy(pool_hbm, page, dst, sem):
    return pltpu.make_async_copy(pool_hbm.at[page], dst, sem)


def _sample_select_kernel(pt_ref, q_ref, kc_hbm, o_ref, kbuf, means_ref, sems):
    b = pl.program_id(0)
    slot = b % 2

    def fetch(seq, slot_, start):
        for p in range(N_PAGES):
            cp = _page_copy(kc_hbm, pt_ref[seq * N_PAGES + p], kbuf.at[slot_, p], sems.at[slot_])
            cp.start() if start else cp.wait()

    @pl.when(b == 0)
    def _():
        fetch(b, slot, True)

    @pl.when(b + 1 < pl.num_programs(0))
    def _():
        fetch(b + 1, 1 - slot, True)

    fetch(b, slot, False)
    fold = 8 // ATTN_N_KV_HEADS
    for n in range(N_PAST_BLOCKS):
        s = jnp.zeros((8, ATTN_HEAD_DIM), F32)
        for p in range(PAGES_PER_BLOCK):
            s = s + jnp.sum(kbuf[slot, n * PAGES_PER_BLOCK + p].reshape(PAGE_ROWS // 8, 8, ATTN_HEAD_DIM), axis=0)
        tot = s[:ATTN_N_KV_HEADS, :]
        for f in range(1, fold):
            tot = tot + s[f * ATTN_N_KV_HEADS:(f + 1) * ATTN_N_KV_HEADS, :]
        means_ref[n * ATTN_N_KV_HEADS:(n + 1) * ATTN_N_KV_HEADS, :] = tot * (1.0 / MOBA_BLOCK)
    gate = lax.dot_general(q_ref[...].astype(BF16), means_ref[...].astype(BF16), (((1,), (1,)), ((), ())),
                           preferred_element_type=F32)
    shape = gate.shape
    lane = lax.broadcasted_iota(I32, shape, 1)
    head = lax.broadcasted_iota(I32, shape, 0)
    gate = jnp.where((lane & (ATTN_N_KV_HEADS - 1)) == (head >> 2), gate, NEG_INF)
    out = jnp.zeros(shape, I32)
    for t in range(MOBA_TOPK):
        m = jnp.max(gate, axis=1, keepdims=True)
        first = jnp.min(jnp.where(gate == m, lane, shape[1]), axis=1, keepdims=True)
        out = jnp.where(lane == t, first >> 2, out)
        gate = jnp.where(lane == first, NEG_INF, gate)
    o_ref[...] = out


def sample_select(page_table, q_s, kc_pages):
    nb = q_s.shape[0]
    assert N_PAST_BLOCKS * ATTN_N_KV_HEADS == LANES
    return pl.pallas_call(
        _sample_select_kernel,
        grid_spec=pltpu.PrefetchScalarGridSpec(
            num_scalar_prefetch=1,
            grid=(nb,),
            in_specs=[pl.BlockSpec((None, ATTN_N_HEADS, ATTN_HEAD_DIM), lambda b, pt: (b, 0, 0)),
                      pl.BlockSpec(memory_space=pl.ANY)],
            out_specs=pl.BlockSpec((None, ATTN_N_HEADS, LANES), lambda b, pt: (b, 0, 0)),
            scratch_shapes=[pltpu.VMEM((2, N_PAGES, PAGE_ROWS, ATTN_HEAD_DIM), F32),
                            pltpu.VMEM((N_PAST_BLOCKS * ATTN_N_KV_HEADS, ATTN_HEAD_DIM), F32),
                            pltpu.SemaphoreType.DMA((2,))]),
        out_shape=jax.ShapeDtypeStruct((nb, ATTN_N_HEADS, LANES), I32),
        compiler_params=_params("arbitrary"),
        name="sample_select",
    )(page_table.reshape(-1), q_s, kc_pages)


def _head_rows_copy(pool_hbm, page, g, dst, sem):
    return pltpu.make_async_copy(pool_hbm.at[page, :, g, :], dst, sem)


def _sample_attend_kernel(pt_ref, sel_ref, q_ref, kn_ref, vn_ref, kc_hbm, vc_hbm, o_ref, kbuf, vbuf, sems):
    b = pl.program_id(0)
    slot = b % 2
    scale = ATTN_HEAD_DIM ** -0.5

    def fetch(seq, slot_, start):
        for h in range(ATTN_N_HEADS):
            g = h // ATTN_GROUP
            for t in range(MOBA_TOPK):
                blk = sel_ref[(seq * ATTN_N_HEADS + h) * MOBA_TOPK + t]
                for half in range(PAGES_PER_BLOCK):
                    page = pt_ref[seq * N_PAGES + blk * PAGES_PER_BLOCK + half]
                    rows = pl.ds(t * MOBA_BLOCK + half * PAGE_SIZE, PAGE_SIZE)
                    for cp in (_head_rows_copy(kc_hbm, page, g, kbuf.at[slot_, h, rows, :], sems.at[0, slot_]),
                               _head_rows_copy(vc_hbm, page, g, vbuf.at[slot_, h, rows, :], sems.at[1, slot_])):
                        cp.start() if start else cp.wait()

    @pl.when(b == 0)
    def _():
        fetch(b, slot, True)

    @pl.when(b + 1 < pl.num_programs(0))
    def _():
        fetch(b + 1, 1 - slot, True)

    fetch(b, slot, False)

    for h in range(ATTN_N_HEADS):
        g = h // ATTN_GROUP
        qh = q_ref[h:h + 1, :]
        q8 = jnp.broadcast_to(qh, (8, ATTN_HEAD_DIM)).astype(BF16)
        s = lax.dot_general(kbuf[slot, h].astype(BF16), q8, (((1,), (1,)), ((), ())),
                            preferred_element_type=F32)[:, 0:1] * scale
        s_own = jnp.sum(_mxu_round(qh) * _mxu_round(kn_ref[g:g + 1, :]), axis=1, keepdims=True) * scale
        m = jnp.maximum(jnp.max(s, axis=0, keepdims=True), s_own)
        p = jnp.exp(s - m)
        p_own = jnp.exp(s_own - m)
        denom = jnp.sum(p, axis=0, keepdims=True) + p_own
        o_sel = jnp.sum(_mxu_round(p / denom) * _mxu_round(vbuf[slot, h]), axis=0, keepdims=True)
        o_ref[h:h + 1, :] = o_sel + _mxu_round(p_own / denom) * _mxu_round(vn_ref[g:g + 1, :])


def sample_attend(page_table, sel, q_s, k_new, v_new, cache_k, cache_v):
    nb = q_s.shape[0]
    per_b = lambda rows: pl.BlockSpec((None, rows, ATTN_HEAD_DIM), lambda b, pt, sl: (b, 0, 0))
    any_spec = pl.BlockSpec(memory_space=pl.ANY)
    return pl.pallas_call(
        _sample_attend_kernel,
        grid_spec=pltpu.PrefetchScalarGridSpec(
            num_scalar_prefetch=2,
            grid=(nb,),
            in_specs=[per_b(ATTN_N_HEADS), per_b(ATTN_N_KV_HEADS), per_b(ATTN_N_KV_HEADS), any_spec, any_spec],
            out_specs=per_b(ATTN_N_HEADS),
            scratch_shapes=[pltpu.VMEM((2, ATTN_N_HEADS, SEL_KEYS, ATTN_HEAD_DIM), F32),
                            pltpu.VMEM((2, ATTN_N_HEADS, SEL_KEYS, ATTN_HEAD_DIM), F32),
                            pltpu.SemaphoreType.DMA((2, 2))]),
        out_shape=jax.ShapeDtypeStruct((nb, ATTN_N_HEADS, ATTN_HEAD_DIM), F32),
        compiler_params=_params("arbitrary"),
        name="sample_attend",
    )(page_table.reshape(-1), sel.reshape(-1), q_s, k_new, v_new, cache_k, cache_v)


def _router_kernel(x_ref, w_ref, b_ref, idx_ref, gate_ref):
    logits = jnp.dot(x_ref[...], w_ref[...].astype(BF16), preferred_element_type=F32) + b_ref[...]
    lane = lax.broadcasted_iota(I32, logits.shape, 1)
    logits = jnp.where(lane < N_EXPERTS, logits, NEG_INF)
    m1 = jnp.max(logits, axis=1, keepdims=True)
    i1 = jnp.min(jnp.where(logits == m1, lane, LANES), axis=1, keepdims=True)
    rest = jnp.where(lane == i1, NEG_INF, logits)
    m2 = jnp.max(rest, axis=1, keepdims=True)
    i2 = jnp.min(jnp.where(rest == m2, lane, LANES), axis=1, keepdims=True)
    e2 = jnp.exp(m2 - m1)
    g1 = 1.0 / (1.0 + e2)
    idx_ref[...] = jnp.where(lane == 0, i1, jnp.where(lane == 1, i2, 0))
    gate_ref[...] = jnp.where(lane == 0, g1, jnp.where(lane == 1, e2 * g1, 0.0))


def router(x, w_router, b_router):
    m, d = x.shape
    wp = jnp.zeros((d, LANES), F32).at[:, :N_EXPERTS].set(w_router)
    bp = jnp.zeros((1, LANES), F32).at[0, :N_EXPERTS].set(b_router)
    row = pl.BlockSpec((ROW_BLOCK, d), lambda i: (i, 0))
    out = pl.BlockSpec((ROW_BLOCK, LANES), lambda i: (i, 0))
    return pl.pallas_call(
        _router_kernel,
        grid=(m // ROW_BLOCK,),
        in_specs=[row, pl.BlockSpec((d, LANES), lambda i: (0, 0)), pl.BlockSpec((1, LANES), lambda i: (0, 0))],
        out_specs=[out, out],
        out_shape=[jax.ShapeDtypeStruct((m, LANES), I32), jax.ShapeDtypeStruct((m, LANES), F32)],
        compiler_params=_params("arbitrary"),
        name="router",
    )(x, wp, bp)


def _row_copy(src_hbm, row, dst, slot, sem):
    return pltpu.make_async_copy(src_hbm.at[pl.ds(row, 1), :], dst.at[pl.ds(slot, 1), :], sem)


GATHER_UNROLL = 8


def _moe_gather_kernel(tok_ref, nxt_ref, x_hbm, o_ref, buf, sems):
    i = pl.program_id(0)
    slot = i % 2

    def issue(idx_ref, slot_):
        def body(r, carry):
            _row_copy(x_hbm, idx_ref[0, 0, r], buf.at[slot_], r, sems.at[slot_]).start()
            return carry
        lax.fori_loop(0, MOE_ROWS, body, 0, unroll=GATHER_UNROLL)

    @pl.when(i == 0)
    def _():
        issue(tok_ref, slot)

    @pl.when(i + 1 < pl.num_programs(0))
    def _():
        issue(nxt_ref, 1 - slot)

    def wait(r, carry):
        _row_copy(x_hbm, 0, buf.at[slot], r, sems.at[slot]).wait()
        return carry

    lax.fori_loop(0, MOE_ROWS, wait, 0, unroll=GATHER_UNROLL)
    o_ref[...] = buf[slot].astype(o_ref.dtype)


def moe_gather(x, buf_tok):
    d = x.shape[1]
    tok = buf_tok.reshape(MOE_BLOCKS, 1, MOE_ROWS)
    idx_spec = lambda step: pl.BlockSpec((1, 1, MOE_ROWS), lambda i: (jnp.minimum(i + step, MOE_BLOCKS - 1), 0, 0),
                                         memory_space=pltpu.SMEM)
    return pl.pallas_call(
        _moe_gather_kernel,
        grid=(MOE_BLOCKS,),
        in_specs=[idx_spec(0), idx_spec(1), pl.BlockSpec(memory_space=pl.ANY)],
        out_specs=pl.BlockSpec((MOE_ROWS, d), lambda i: (i, 0)),
        out_shape=jax.ShapeDtypeStruct((MOE_BUF, d), BF16),
        scratch_shapes=[pltpu.VMEM((2, MOE_ROWS, d), F32), pltpu.SemaphoreType.DMA((2,))],
        compiler_params=_params("arbitrary"),
        name="moe_gather",
    )(tok, tok, x)


def _moe_expert_kernel(first_ref, count_ref, used_ref, *refs, swiglu, bn):
    if swiglu:
        wg_ref, wu_ref, x_hbm, o_hbm, wgb_ref, wub_ref, xbuf, obuf, xsem, osem = refs
    else:
        wg_ref, x_hbm, o_hbm, wgb_ref, xbuf, obuf, xsem, osem = refs
    e = pl.program_id(0)
    j = pl.program_id(1)
    first = first_ref[e]
    n = count_ref[e]
    cols = pl.ds(pl.multiple_of(j * bn, bn), bn)

    def rows(blk):
        return pl.ds(pl.multiple_of(blk * MOE_ROWS, MOE_ROWS), MOE_ROWS)

    def x_copy(k, slot):
        return pltpu.make_async_copy(x_hbm.at[rows(first + k), :], xbuf.at[slot], xsem.at[slot])

    def o_copy(blk, slot):
        return pltpu.make_async_copy(obuf.at[slot], o_hbm.at[rows(blk), cols], osem.at[slot])

    @pl.when(n > 0)
    def _():
        x_copy(0, 0).start()
        wgb_ref[...] = wg_ref[...].astype(BF16)
        if swiglu:
            wub_ref[...] = wu_ref[...].astype(BF16)

    def body(k, carry):
        slot = k % 2
        x_copy(k, slot).wait()

        @pl.when(k + 1 < n)
        def _():
            x_copy(k + 1, 1 - slot).start()

        @pl.when(k >= 2)
        def _():
            o_copy(first + k - 2, slot).wait()

        x = xbuf[slot]
        acc = jnp.dot(x, wgb_ref[...], preferred_element_type=F32)
        if swiglu:
            acc = _silu(acc) * jnp.dot(x, wub_ref[...], preferred_element_type=F32)
        obuf[slot] = acc.astype(obuf.dtype)
        o_copy(first + k, slot).start()
        return carry

    lax.fori_loop(0, n, body, 0)

    @pl.when(n >= 2)
    def _():
        o_copy(first + n - 2, n % 2).wait()

    @pl.when(n >= 1)
    def _():
        o_copy(first + n - 1, (n - 1) % 2).wait()

    @pl.when(e == pl.num_programs(0) - 1)
    def _():
        obuf[0] = jnp.zeros(obuf.shape[1:], obuf.dtype)

        def fill(blk, carry):
            o_copy(blk, 0).start()
            o_copy(blk, 0).wait()
            return carry

        lax.fori_loop(used_ref[0], MOE_BLOCKS, fill, 0)


def moe_expert_matmul(xs, w, plan, *, swiglu, bn, out_dtype):
    first, count, used = plan
    r, k = xs.shape
    f = w.shape[2] // 2 if swiglu else w.shape[2]
    nj = f // bn
    w_spec = lambda off: pl.BlockSpec((None, k, bn), lambda e, j, *_: (e, 0, j + off))
    any_spec = pl.BlockSpec(memory_space=pl.ANY)
    n_w = 2 if swiglu else 1
    return pl.pallas_call(
        functools.partial(_moe_expert_kernel, swiglu=swiglu, bn=bn),
        grid_spec=pltpu.PrefetchScalarGridSpec(
            num_scalar_prefetch=3,
            grid=(N_EXPERTS, nj),
            in_specs=[w_spec(0)] + ([w_spec(nj)] if swiglu else []) + [any_spec],
            out_specs=any_spec,
            scratch_shapes=[pltpu.VMEM((k, bn), BF16)] * n_w + [
                pltpu.VMEM((2, MOE_ROWS, k), xs.dtype), pltpu.VMEM((2, MOE_ROWS, bn), out_dtype),
                pltpu.SemaphoreType.DMA((2,)), pltpu.SemaphoreType.DMA((2,))]),
        out_shape=jax.ShapeDtypeStruct((r, f), out_dtype),
        compiler_params=_params("arbitrary", "arbitrary"),
        name="moe_gu" if swiglu else "moe_down",
    )(first, count, used, *([w] * n_w), xs)


def _moe_combine_ln_kernel(pos_ref, nxt_ref, y_hbm, x_ref, gate_ref, g_ref, b_ref, o_ref, ob_ref, buf, sems):
    i = pl.program_id(0)
    slot = i % 2

    def issue(idx_ref, slot_):
        def body(r, carry):
            for t in range(MOE_TOPK):
                _row_copy(y_hbm, idx_ref[0, 0, MOE_TOPK * r + t], buf.at[slot_, t], r, sems.at[slot_]).start()
            return carry
        lax.fori_loop(0, ROW_BLOCK, body, 0, unroll=GATHER_UNROLL)

    @pl.when(i == 0)
    def _():
        issue(pos_ref, slot)

    @pl.when(i + 1 < pl.num_programs(0))
    def _():
        issue(nxt_ref, 1 - slot)

    def wait(r, carry):
        for t in range(MOE_TOPK):
            _row_copy(y_hbm, 0, buf.at[slot, t], r, sems.at[slot]).wait()
        return carry

    lax.fori_loop(0, ROW_BLOCK, wait, 0, unroll=GATHER_UNROLL)
    gates = gate_ref[...]
    ff = buf[slot, 0] * gates[:, 0:1]
    for t in range(1, MOE_TOPK):
        ff = ff + buf[slot, t] * gates[:, t:t + 1]
    out = _layernorm_rows(DEEPNORM_ALPHA * x_ref[...] + ff, g_ref[...], b_ref[...])
    o_ref[...] = out
    ob_ref[...] = out.astype(BF16)


def moe_combine_ln(x, ys, pos, gates, g, b):
    m, d = x.shape
    nblk = m // ROW_BLOCK
    row = pl.BlockSpec((ROW_BLOCK, d), lambda i: (i, 0))
    vec = pl.BlockSpec((1, d), lambda i: (0, 0))
    pos3 = pos.reshape(nblk, 1, MOE_TOPK * ROW_BLOCK)
    idx_spec = lambda step: pl.BlockSpec((1, 1, MOE_TOPK * ROW_BLOCK),
                                         lambda i: (jnp.minimum(i + step, nblk - 1), 0, 0), memory_space=pltpu.SMEM)
    return pl.pallas_call(
        _moe_combine_ln_kernel,
        grid=(nblk,),
        in_specs=[idx_spec(0), idx_spec(1), pl.BlockSpec(memory_space=pl.ANY),
                  row, pl.BlockSpec((ROW_BLOCK, LANES), lambda i: (i, 0)), vec, vec],
        out_specs=[row, row],
        out_shape=[jax.ShapeDtypeStruct((m, d), F32), jax.ShapeDtypeStruct((m, d), BF16)],
        scratch_shapes=[pltpu.VMEM((2, MOE_TOPK, ROW_BLOCK, d), F32), pltpu.SemaphoreType.DMA((2,))],
        compiler_params=_params("arbitrary"),
        name="moe_combine_ln",
    )(pos3, pos3, ys, x, gates, g.reshape(1, d), b.reshape(1, d))


def moe_plan(idx):
    e = idx[:N_TOK, :MOE_TOPK].reshape(N_ASSIGN)
    onehot = (e[:, None] == jnp.arange(N_EXPERTS, dtype=I32)[None, :]).astype(I32)
    before = jnp.cumsum(onehot, axis=0) - onehot
    rank = jnp.sum(before * onehot, axis=1)
    counts = jnp.sum(onehot, axis=0)
    padded = (counts + MOE_ROWS - 1) // MOE_ROWS * MOE_ROWS
    pad_end = jnp.cumsum(padded)
    pad_start = pad_end - padded
    dest = pad_start[e] + rank
    tok = jnp.arange(N_ASSIGN, dtype=I32) // MOE_TOPK
    buf_tok = jnp.zeros((MOE_BUF,), I32).at[dest].set(tok)
    blocks = ((pad_start // MOE_ROWS).astype(I32), (padded // MOE_ROWS).astype(I32),
              (pad_end[-1:] // MOE_ROWS).astype(I32))
    pos = jnp.zeros((TOK_PAD * MOE_TOPK,), I32).at[:N_ASSIGN].set(dest)
    return buf_tok, blocks, pos


def _rope_tables():
    half = ROT_DIM // 2
    inv = ROPE_THETA ** (-jnp.arange(half, dtype=F32) * 2.0 / ROT_DIM)
    pos = jnp.concatenate([jnp.arange(SEQ, dtype=I32), jnp.full((TOK_PAD - SEQ,), PAST_LEN, I32)])
    ang = pos.astype(F32)[:, None] * inv[None, :]
    cos, sin = jnp.cos(ang), jnp.sin(ang)
    rest = LANES - ROT_DIM
    cf = jnp.concatenate([cos, cos, jnp.ones((TOK_PAD, rest), F32)], axis=1)
    s1 = jnp.concatenate([-sin, jnp.zeros((TOK_PAD, LANES - half), F32)], axis=1)
    s2 = jnp.concatenate([jnp.zeros((TOK_PAD, half), F32), sin, jnp.zeros((TOK_PAD, rest), F32)], axis=1)
    return cf, s1, s2


def _tokens(prompt, sample):
    pad = jnp.zeros((TOK_PAD - N_TOK, prompt.shape[1]), prompt.dtype)
    return jnp.concatenate([prompt, sample, pad], axis=0)


def _put_sample(buf, rows):
    tail = jnp.zeros((TOK_PAD - SEQ, buf.shape[1]), buf.dtype).at[:rows.shape[0]].set(rows.astype(buf.dtype))
    return lax.dynamic_update_slice(buf, tail, (SEQ, 0))


def kernel(x_prompt, x_sample, state_ssm, state_conv, cache_k, cache_v, page_table, p_prompt, p_sample, ln_g, ln_b, ssm_w_in, ssm_conv_w, ssm_conv_b, ssm_dt_bias, ssm_A_log, ssm_D, ssm_norm_w, ssm_w_out, attn_w_kv, attn_w_q, attn_w_o, ffn_w_gu, ffn_w_down, moe_w_router, moe_b_router, moe_w_gu, moe_w_down, ple_w_gate, ple_b_gate, ple_w_proj):
    x0 = _tokens(x_prompt[0], x_sample[:, 0])
    x0b = x0.astype(BF16)
    p_tok = [_tokens(p_prompt[i, 0], p_sample[i, :, 0]).astype(BF16) for i in range(2)]

    w_in_t = jnp.swapaxes(ssm_w_in, 1, 2)[0]
    zx_cols = SSM_D_INNER + SSM_CONV_DIM
    zx = matmul(x0b, w_in_t, bn=1024, ncols=zx_cols, w_transposed=True)
    dt_raw = matmul(x0b, w_in_t, bn=LANES, col0=zx_cols, ncols=LANES, w_transposed=True)[:, :SSM_N_HEADS]

    xbc = conv_prompt(zx, ssm_conv_w[0], ssm_conv_b[0])
    st_conv = state_conv[0]
    x_raw_s = zx[SEQ:N_TOK, SSM_D_INNER:]
    xbc_s = conv_sample(jnp.transpose(st_conv, (1, 0, 2)), x_raw_s, ssm_conv_w[0], ssm_conv_b[0])
    conv_prompt_out = zx[SEQ - (SSM_CONV_W - 1):SEQ, SSM_D_INNER:].reshape(1, 1, SSM_CONV_W - 1, SSM_CONV_DIM)
    conv_sample_out = jnp.concatenate([st_conv[:, 1:], x_raw_s[:, None, :]], axis=1)[None]

    d_lanes = jnp.repeat(ssm_D[0], SSM_HEAD_DIM).reshape(1, SSM_D_INNER)
    y, h_prompt = ssd_prompt(xbc, dt_raw[:SEQ], ssm_dt_bias[0], ssm_A_log[0], d_lanes)
    xs_s = xbc_s[:DEC_BATCH]
    gn = SSM_N_GROUPS * SSM_D_STATE
    h_sample, y_s = ssd_sample(
        state_ssm[0].reshape(DEC_BATCH, SSM_N_HEADS * SSM_HEAD_DIM, SSM_D_STATE),
        xs_s[:, :SSM_D_INNER],
        xs_s[:, SSM_D_INNER:SSM_D_INNER + gn].reshape(DEC_BATCH, SSM_N_GROUPS, SSM_D_STATE),
        xs_s[:, SSM_D_INNER + gn:].reshape(DEC_BATCH, SSM_N_GROUPS, SSM_D_STATE),
        dt_raw[SEQ:N_TOK], ssm_dt_bias[0], ssm_A_log[0], d_lanes)
    y = _put_sample(y, y_s.reshape(DEC_BATCH, SSM_D_INNER))
    gated = gate_norm(y, zx, ssm_norm_w[0])
    mix = matmul(gated, ssm_w_out[0], bn=512)
    x1, x1b = add_ln(x0, mix, ln_g[0, 0], ln_b[0, 0])

    hid = matmul_swiglu(x1b, ffn_w_gu[0], bn=512)
    ff = matmul(hid, ffn_w_down[0], bn=512)
    x2, x2b = add_ln(x1, ff, ln_g[0, 1], ln_b[0, 1])
    x3, x3b = ple_mix(x2, x2b, p_tok[0], ple_w_gate, ple_b_gate, ple_w_proj, 0)

    rope = _rope_tables()
    k_all = matmul(x3b, attn_w_kv, bn=KV_COLS, ncols=KV_COLS, rope=rope)
    v_all = matmul(x3b, attn_w_kv, bn=KV_COLS, col0=KV_COLS, ncols=KV_COLS)
    q_all = matmul(x3b, attn_w_q[0], bn=1024, rope=rope)
    att = moba_prompt(q_all, k_all, v_all, block_means(k_all).reshape(SEQ // MOBA_BLOCK, KV_COLS))

    q_s = q_all[SEQ:N_TOK].reshape(DEC_BATCH, ATTN_N_HEADS, ATTN_HEAD_DIM)
    k_s = k_all[SEQ:N_TOK].reshape(DEC_BATCH, ATTN_N_KV_HEADS, ATTN_HEAD_DIM)
    v_s = v_all[SEQ:N_TOK].reshape(DEC_BATCH, ATTN_N_KV_HEADS, ATTN_HEAD_DIM)
    n_pool = cache_k.shape[0]
    sel = sample_select(page_table, q_s, cache_k.reshape(n_pool, PAGE_ROWS, ATTN_HEAD_DIM))[:, :, :MOBA_TOPK]
    att_s = sample_attend(page_table, sel, q_s, k_s, v_s, cache_k, cache_v)
    att = _put_sample(att, att_s.reshape(DEC_BATCH, D_MODEL))
    mix = matmul(att, attn_w_o[0], bn=1024)
    x4, x4b = add_ln(x3, mix, ln_g[1, 0], ln_b[1, 0])

    idx, gates = router(x4b, moe_w_router[0], moe_b_router[0])
    buf_tok, blocks, pos = moe_plan(idx)
    xs = moe_gather(x4, buf_tok)
    hs = moe_expert_matmul(xs, moe_w_gu[0], blocks, swiglu=True, bn=1024, out_dtype=BF16)
    ys = moe_expert_matmul(hs, moe_w_down[0], blocks, swiglu=False, bn=512, out_dtype=F32)
    x5, x5b = moe_combine_ln(x4, ys, pos, gates, ln_g[1, 1], ln_b[1, 1])
    y_prompt, y_sample = ple_mix(x5, x5b, p_tok[1], ple_w_gate, ple_b_gate, ple_w_proj, 1, final=True)

    state_shape = (1, -1, SSM_N_HEADS, SSM_HEAD_DIM, SSM_D_STATE)
    kv_shape = (-1, ATTN_N_KV_HEADS, ATTN_HEAD_DIM)
    return (y_prompt.reshape(1, SEQ, D_MODEL),
            y_sample.reshape(DEC_BATCH, 1, D_MODEL),
            h_prompt.reshape(state_shape),
            conv_prompt_out,
            k_all[:SEQ].reshape((1,) + (SEQ,) + kv_shape[1:]),
            v_all[:SEQ].reshape((1,) + (SEQ,) + kv_shape[1:]),
            h_sample.reshape(state_shape),
            conv_sample_out,
            k_all[SEQ:N_TOK].reshape((DEC_BATCH, 1) + kv_shape[1:]),
            v_all[SEQ:N_TOK].reshape((DEC_BATCH, 1) + kv_shape[1:]))
```

```python
import functools

import jax
import jax.numpy as jnp
import numpy as np
from jax import lax
from jax.experimental import pallas as pl
from jax.experimental.pallas import tpu as pltpu

F32 = jnp.float32
BF16 = jnp.bfloat16
I32 = jnp.int32

D_MODEL = 2048
SEQ = 8192
DEC_BATCH = 32
PAST_LEN = 8192
PAGE_SIZE = 128

SSM_D_INNER = 4096
SSM_HEAD_DIM = 64
SSM_N_HEADS = 64
SSM_N_GROUPS = 8
SSM_D_STATE = 128
SSM_CONV_W = 4
SSM_CHUNK = 128
SSM_CONV_DIM = 6144
SSM_HEADS_PER_GROUP = SSM_N_HEADS // SSM_N_GROUPS

ATTN_HEAD_DIM = 128
ATTN_N_HEADS = 16
ATTN_N_KV_HEADS = 4
ATTN_GROUP = 4
ROT_DIM = 32
ROPE_THETA = 500000.0
MOBA_BLOCK = 256
MOBA_TOPK = 3
N_PAST_BLOCKS = PAST_LEN // MOBA_BLOCK
PAGES_PER_BLOCK = MOBA_BLOCK // PAGE_SIZE
N_PAGES = PAST_LEN // PAGE_SIZE

D_FF = 5632
N_EXPERTS = 8
MOE_TOPK = 2
D_FF_EXPERT = 7168
D_PLE = 256
DEEPNORM_ALPHA = 4.0 ** 0.25
LN_EPS = 1e-5
RMS_EPS = 1e-5

LANES = 128
VMEM_LIMIT_BYTES = 56 * 1024 * 1024

N_TOK = SEQ + DEC_BATCH
TOK_PAD = 8448
MM_ROWS = 768
ROW_BLOCK = 256
MOE_ROWS = 256
N_ASSIGN = N_TOK * MOE_TOPK
MOE_BLOCKS = -(-N_ASSIGN // MOE_ROWS) + N_EXPERTS
MOE_BUF = MOE_BLOCKS * MOE_ROWS
NEG_INF = float("-inf")


def _params(*sem):
    return pltpu.CompilerParams(dimension_semantics=sem, vmem_limit_bytes=VMEM_LIMIT_BYTES)


def _split3(a):
    hi = a.astype(BF16)
    r1 = a - hi.astype(F32)
    mid = r1.astype(BF16)
    lo = (r1 - mid.astype(F32)).astype(BF16)
    return hi, mid, lo


def _dot_f32(a, b, dims):
    a0, a1, a2 = _split3(a)
    b0, b1, b2 = _split3(b)
    dn = (dims, ((), ()))
    d = lambda x, y: lax.dot_general(x, y, dn, preferred_element_type=F32)
    return (d(a0, b0) + (d(a0, b1) + d(a1, b0))) + ((d(a0, b2) + d(a2, b0)) + d(a1, b1))


def _softplus(x):
    return jnp.maximum(x, 0.0) + jnp.log1p(jnp.exp(-jnp.abs(x)))


def _silu(x):
    return x * jax.nn.sigmoid(x)


def _rope_lanes(a, cf, s1, s2):
    return a * cf + pltpu.roll(a, LANES - ROT_DIM // 2, 1) * s1 + pltpu.roll(a, ROT_DIM // 2, 1) * s2


def _mm_kernel(x_ref, w_ref, *rest, rope, valid_cols, w_transposed):
    if rope:
        cf_ref, s1_ref, s2_ref, o_ref, wb_ref = rest
    else:
        o_ref, wb_ref = rest

    @pl.when(pl.program_id(1) == 0)
    def _():
        w = w_ref[...]
        if valid_cols is not None:
            axis = 0 if w_transposed else 1
            w = jnp.where(lax.broadcasted_iota(I32, w.shape, axis) < valid_cols, w, 0.0)
        wb_ref[...] = (w.T if w_transposed else w).astype(BF16)

    acc = jnp.dot(x_ref[...], wb_ref[...], preferred_element_type=F32)
    if rope:
        cf, s1, s2 = cf_ref[...], s1_ref[...], s2_ref[...]
        for c in range(acc.shape[1] // LANES):
            sl = slice(c * LANES, (c + 1) * LANES)
            o_ref[:, sl] = _rope_lanes(acc[:, sl], cf, s1, s2).astype(o_ref.dtype)
    else:
        o_ref[...] = acc.astype(o_ref.dtype)


def matmul(x, w, *, bn, col0=0, ncols=None, rope=None, out_dtype=F32, bm=MM_ROWS, w_transposed=False):
    m, k = x.shape
    n_total = w.shape[0] if w_transposed else w.shape[1]
    ncols = n_total if ncols is None else ncols
    assert m % bm == 0 and ncols % bn == 0 and col0 % bn == 0
    j0 = col0 // bn
    valid_cols = None
    if col0 + ncols > n_total:
        assert ncols == bn
        valid_cols = n_total - col0
    w_spec = (pl.BlockSpec((bn, k), lambda j, i: (j + j0, 0)) if w_transposed
              else pl.BlockSpec((k, bn), lambda j, i: (0, j + j0)))
    in_specs = [pl.BlockSpec((bm, k), lambda j, i: (i, 0)), w_spec]
    args = [x, w]
    if rope is not None:
        in_specs += [pl.BlockSpec((bm, LANES), lambda j, i: (i, 0))] * 3
        args += list(rope)
    return pl.pallas_call(
        functools.partial(_mm_kernel, rope=rope is not None, valid_cols=valid_cols, w_transposed=w_transposed),
        grid=(ncols // bn, m // bm),
        in_specs=in_specs,
        out_specs=pl.BlockSpec((bm, bn), lambda j, i: (i, j)),
        out_shape=jax.ShapeDtypeStruct((m, ncols), out_dtype),
        scratch_shapes=[pltpu.VMEM((k, bn), BF16)],
        compiler_params=_params("arbitrary", "arbitrary"),
        name="matmul",
    )(*args)


def _mm_swiglu_kernel(x_ref, wg_ref, wu_ref, o_ref, wgb_ref, wub_ref):
    @pl.when(pl.program_id(1) == 0)
    def _():
        wgb_ref[...] = wg_ref[...].astype(BF16)
        wub_ref[...] = wu_ref[...].astype(BF16)

    x = x_ref[...]
    g = jnp.dot(x, wgb_ref[...], preferred_element_type=F32)
    u = jnp.dot(x, wub_ref[...], preferred_element_type=F32)
    o_ref[...] = (_silu(g) * u).astype(o_ref.dtype)


def matmul_swiglu(x, w_gu, *, bn, bm=MM_ROWS):
    m, k = x.shape
    f = w_gu.shape[1] // 2
    assert m % bm == 0 and f % bn == 0
    nj = f // bn
    return pl.pallas_call(
        _mm_swiglu_kernel,
        grid=(nj, m // bm),
        in_specs=[pl.BlockSpec((bm, k), lambda j, i: (i, 0)),
                  pl.BlockSpec((k, bn), lambda j, i: (0, j)),
                  pl.BlockSpec((k, bn), lambda j, i: (0, j + nj))],
        out_specs=pl.BlockSpec((bm, bn), lambda j, i: (i, j)),
        out_shape=jax.ShapeDtypeStruct((m, f), BF16),
        scratch_shapes=[pltpu.VMEM((k, bn), BF16), pltpu.VMEM((k, bn), BF16)],
        compiler_params=_params("arbitrary", "arbitrary"),
        name="matmul_swiglu",
    )(x, w_gu, w_gu)


def _ple_kernel(xb_ref, pb_ref, wg_ref, wp_ref, bg_ref, x_ref, o_ref, o2_ref, wgb_ref, wpb_ref, *, final):
    @pl.when(pl.program_id(1) == 0)
    def _():
        wgb_ref[...] = wg_ref[...].astype(BF16)
        wpb_ref[...] = wp_ref[...].astype(BF16)

    gate = jax.nn.sigmoid(jnp.dot(xb_ref[...], wgb_ref[...], preferred_element_type=F32) + bg_ref[...])
    proj = jnp.dot(pb_ref[...], wpb_ref[...], preferred_element_type=F32)
    out = x_ref[...] + gate * proj
    o_ref[...] = out
    if final:
        tail0 = SEQ - (TOK_PAD - MM_ROWS)

        @pl.when(pl.program_id(1) == pl.num_programs(1) - 1)
        def _():
            o2_ref[...] = out[tail0:tail0 + DEC_BATCH, :]
    else:
        o2_ref[...] = out.astype(BF16)


def ple_mix(x, xb, pb, w_gate, b_gate, w_proj, layer, *, final=False, bn=1024, bm=MM_ROWS):
    m, d = x.shape
    kp = pb.shape[1]
    full = pl.BlockSpec((bm, bn), lambda j, i: (i, j))
    if final:
        assert m == TOK_PAD and bm == MM_ROWS and 0 <= SEQ - (TOK_PAD - MM_ROWS) <= MM_ROWS - DEC_BATCH
        out_specs = [full, pl.BlockSpec((DEC_BATCH, bn), lambda j, i: (0, j))]
        out_shape = [jax.ShapeDtypeStruct((SEQ, d), F32), jax.ShapeDtypeStruct((DEC_BATCH, d), F32)]
    else:
        out_specs = [full, full]
        out_shape = [jax.ShapeDtypeStruct((m, d), F32), jax.ShapeDtypeStruct((m, d), BF16)]
    return pl.pallas_call(
        functools.partial(_ple_kernel, final=final),
        grid=(d // bn, m // bm),
        in_specs=[pl.BlockSpec((bm, d), lambda j, i: (i, 0)),
                  pl.BlockSpec((bm, kp), lambda j, i: (i, 0)),
                  pl.BlockSpec((None, d, bn), lambda j, i: (layer, 0, j)),
                  pl.BlockSpec((None, kp, bn), lambda j, i: (layer, 0, j)),
                  pl.BlockSpec((None, 1, bn), lambda j, i: (layer, 0, j)),
                  full],
        out_specs=out_specs,
        out_shape=out_shape,
        scratch_shapes=[pltpu.VMEM((d, bn), BF16), pltpu.VMEM((kp, bn), BF16)],
        compiler_params=_params("arbitrary", "arbitrary"),
        name="ple_mix",
    )(xb, pb, w_gate, w_proj, b_gate.reshape(b_gate.shape[0], 1, d), x)


def _layernorm_rows(v, g, b):
    mu = jnp.mean(v, axis=-1, keepdims=True)
    c = v - mu
    var = jnp.mean(c * c, axis=-1, keepdims=True)
    return c * lax.rsqrt(var + LN_EPS) * g + b


def _add_ln_kernel(x_ref, xt_ref, m_ref, g_ref, b_ref, o_ref, ob_ref, *, n_main):
    x = x_ref[...]
    if n_main is not None:
        x = jnp.where(pl.program_id(0) < n_main, x, xt_ref[...])
    out = _layernorm_rows(DEEPNORM_ALPHA * x + m_ref[...], g_ref[...], b_ref[...])
    o_ref[...] = out
    ob_ref[...] = out.astype(BF16)


def add_ln(x, mix, g, b, x_tail=None):
    m, d = mix.shape
    row = pl.BlockSpec((ROW_BLOCK, d), lambda i: (i, 0))
    vec = pl.BlockSpec((1, d), lambda i: (0, 0))
    if x_tail is None:
        n_main, x_spec, x_tail = None, row, x
    else:
        n_main = x.shape[0] // ROW_BLOCK
        assert x_tail.shape[0] == ROW_BLOCK and n_main + 1 == m // ROW_BLOCK
        x_spec = pl.BlockSpec((ROW_BLOCK, d), lambda i: (jnp.minimum(i, n_main - 1), 0))
    return pl.pallas_call(
        functools.partial(_add_ln_kernel, n_main=n_main),
        grid=(m // ROW_BLOCK,),
        in_specs=[x_spec, pl.BlockSpec((ROW_BLOCK, d), lambda i: (0, 0)), row, vec, vec],
        out_specs=[row, row],
        out_shape=[jax.ShapeDtypeStruct((m, d), F32), jax.ShapeDtypeStruct((m, d), BF16)],
        compiler_params=_params("arbitrary"),
        name="add_ln",
    )(x, x_tail, mix, g.reshape(1, d), b.reshape(1, d))


def _gate_norm_kernel(y_ref, z_ref, w_ref, o_ref):
    g = y_ref[...] * _silu(z_ref[...])
    ms = jnp.mean(g * g, axis=-1, keepdims=True)
    o_ref[...] = (g * lax.rsqrt(ms + RMS_EPS) * w_ref[...]).astype(o_ref.dtype)


def gate_norm(y, zx, norm_w):
    m, di = y.shape
    row = pl.BlockSpec((ROW_BLOCK, di), lambda i: (i, 0))
    return pl.pallas_call(
        _gate_norm_kernel,
        grid=(m // ROW_BLOCK,),
        in_specs=[row, row, pl.BlockSpec((1, di), lambda i: (0, 0))],
        out_specs=row,
        out_shape=jax.ShapeDtypeStruct((m, di), BF16),
        compiler_params=_params("arbitrary"),
        name="gate_norm",
    )(y, zx, norm_w.reshape(1, di))


CONV_COLS = 2048


def _conv_prompt_kernel(x_ref, h_ref, w_ref, b_ref, o_ref):
    x = x_ref[...]
    halo = jnp.where(pl.program_id(0) == 0, 0.0, h_ref[...])
    w = w_ref[...]
    acc = b_ref[...] + x * w[SSM_CONV_W - 1:SSM_CONV_W, :]
    rows8 = lax.broadcasted_iota(I32, halo.shape, 0)
    for s in range(1, SSM_CONV_W):
        xr = pltpu.roll(x, s, 0)
        head = jnp.where(rows8 < s, pltpu.roll(halo, s, 0), xr[:8, :])
        shifted = jnp.concatenate([head, xr[8:, :]], axis=0)
        acc = acc + shifted * w[SSM_CONV_W - 1 - s:SSM_CONV_W - s, :]
    o_ref[...] = _silu(acc)


def conv_prompt(zx, conv_w, conv_b):
    c0 = SSM_D_INNER // CONV_COLS
    rb8 = MM_ROWS // 8
    return pl.pallas_call(
        _conv_prompt_kernel,
        grid=(TOK_PAD // MM_ROWS, SSM_CONV_DIM // CONV_COLS),
        in_specs=[pl.BlockSpec((MM_ROWS, CONV_COLS), lambda i, k: (i, c0 + k)),
                  pl.BlockSpec((8, CONV_COLS), lambda i, k: (jnp.maximum(i * rb8 - 1, 0), c0 + k)),
                  pl.BlockSpec((SSM_CONV_W, CONV_COLS), lambda i, k: (0, k)),
                  pl.BlockSpec((1, CONV_COLS), lambda i, k: (0, k))],
        out_specs=pl.BlockSpec((MM_ROWS, CONV_COLS), lambda i, k: (i, k)),
        out_shape=jax.ShapeDtypeStruct((TOK_PAD, SSM_CONV_DIM), F32),
        compiler_params=_params("arbitrary", "arbitrary"),
        name="conv_prompt",
    )(zx, zx, conv_w, conv_b.reshape(1, SSM_CONV_DIM))


def _conv_sample_kernel(st_ref, x_ref, w_ref, b_ref, o_ref):
    w = w_ref[...]
    acc = b_ref[...] + x_ref[...] * w[SSM_CONV_W - 1:SSM_CONV_W, :]
    for j in range(SSM_CONV_W - 1):
        acc = acc + st_ref[j] * w[j:j + 1, :]
    o_ref[...] = jnp.zeros(o_ref.shape, F32)
    o_ref[:DEC_BATCH, :] = _silu(acc)


def conv_sample(st, x_raw, conv_w, conv_b):
    return pl.pallas_call(
        _conv_sample_kernel,
        grid=(SSM_CONV_DIM // CONV_COLS,),
        in_specs=[pl.BlockSpec((SSM_CONV_W - 1, DEC_BATCH, CONV_COLS), lambda k: (0, 0, k)),
                  pl.BlockSpec((DEC_BATCH, CONV_COLS), lambda k: (0, k)),
                  pl.BlockSpec((SSM_CONV_W, CONV_COLS), lambda k: (0, k)),
                  pl.BlockSpec((1, CONV_COLS), lambda k: (0, k))],
        out_specs=pl.BlockSpec((TOK_PAD - SEQ, CONV_COLS), lambda k: (0, k)),
        out_shape=jax.ShapeDtypeStruct((TOK_PAD - SEQ, SSM_CONV_DIM), F32),
        compiler_params=_params("arbitrary"),
        name="conv_sample",
    )(st, x_raw, conv_w, conv_b.reshape(1, SSM_CONV_DIM))


SSD_GROUP_COLS = SSM_HEADS_PER_GROUP * SSM_HEAD_DIM
SSD_PAIR_ROWS = 2 * SSM_HEAD_DIM


def _ssd_prompt_kernel(*refs):
    y_ref = refs[-2]

    @pl.when(pl.program_id(0) < SEQ // SSM_CHUNK)
    def _():
        _ssd_prompt_chunk(*refs)

    @pl.when(pl.program_id(0) >= SEQ // SSM_CHUNK)
    def _():
        y_ref[...] = jnp.zeros(y_ref.shape, y_ref.dtype)


def _ssd_prompt_chunk(xs_ref, b_ref, c_ref, dtc_ref, dtr_ref, bc_ref, br_ref, ac_ref, ar_ref, d_ref,
                      y_ref, h_ref):
    q = SSM_CHUNK
    dt_c = _softplus(dtc_ref[...] + bc_ref[...])
    dt_r = _softplus(dtr_ref[...] + br_ref[...])
    da_c = dt_c * -jnp.exp(ac_ref[...])
    da_r = dt_r * -jnp.exp(ar_ref[...])
    ti = lax.broadcasted_iota(I32, (q, q), 0)
    si = lax.broadcasted_iota(I32, (q, q), 1)
    causal = ti >= si
    tril = causal.astype(F32)
    triu = (ti <= si).astype(F32)
    acs_c = _dot_f32(tril, da_c, ((1,), (0,)))
    acs_r = _dot_f32(da_r, triu, ((1,), (0,)))
    last = acs_c[q - 1:q, :]

    @pl.when(pl.program_id(0) == 0)
    def _():
        h_ref[...] = jnp.zeros(h_ref.shape, F32)

    lane = lax.broadcasted_iota(I32, (q, LANES), 1)
    lo = lane < SSM_HEAD_DIM
    sub = lax.broadcasted_iota(I32, (SSD_PAIR_ROWS, 1), 0)
    for g in range(SSM_N_GROUPS):
        n_cols = slice(g * SSM_D_STATE, (g + 1) * SSM_D_STATE)
        bm = b_ref[:, n_cols].astype(BF16)
        cm = c_ref[:, n_cols].astype(BF16)
        cb = lax.dot_general(cm, bm, (((1,), (1,)), ((), ())), preferred_element_type=F32)
        for pair in range(SSM_HEADS_PER_GROUP // 2):
            r1 = g * SSM_HEADS_PER_GROUP + 2 * pair
            r2 = r1 + 1
            cols = slice(r1 * SSM_HEAD_DIM, (r2 + 1) * SSM_HEAD_DIM)
            x = xs_ref[:, cols]
            xdt = x * jnp.where(lo, dt_c[:, r1:r1 + 1], dt_c[:, r2:r2 + 1])
            ms = []
            for r in (r1, r2):
                seg = acs_c[:, r:r + 1] - acs_r[r:r + 1, :]
                ms.append((cb * jnp.exp(jnp.where(causal, seg, NEG_INF))).astype(BF16))
            y = (jnp.dot(ms[0], jnp.where(lo, xdt, 0.0).astype(BF16), preferred_element_type=F32)
                 + jnp.dot(ms[1], jnp.where(lo, 0.0, xdt).astype(BF16), preferred_element_type=F32))

            h_in = h_ref[cols, :]
            y_off = lax.dot_general(cm, h_in.astype(BF16), (((1,), (1,)), ((), ())),
                                    preferred_element_type=F32)
            y = y + y_off * jnp.exp(jnp.where(lo, acs_c[:, r1:r1 + 1], acs_c[:, r2:r2 + 1]))
            y_ref[:, cols] = y + d_ref[:, cols] * x

            to_end = jnp.exp(jnp.where(lo, last[:, r1:r1 + 1] - acs_c[:, r1:r1 + 1],
                                       last[:, r2:r2 + 1] - acs_c[:, r2:r2 + 1]))
            xw_t = (xdt * to_end).T.astype(BF16)
            st = jnp.dot(xw_t, bm, preferred_element_type=F32)
            decay = jnp.exp(jnp.where(sub < SSM_HEAD_DIM, last[:, r1:r1 + 1], last[:, r2:r2 + 1]))
            h_ref[cols, :] = h_in * decay + st


def ssd_prompt(xbc, dt_raw, dt_bias, a_log, d_lanes):
    nc = SEQ // SSM_CHUNK
    nh = SSM_N_HEADS
    gn = SSM_N_GROUPS * SSM_D_STATE
    clamp = lambda c: jnp.minimum(c, nc - 1)
    col_vec = pl.BlockSpec((1, nh), lambda c: (0, 0))
    row_vec = pl.BlockSpec((nh, 1), lambda c: (0, 0))
    return pl.pallas_call(
        _ssd_prompt_kernel,
        grid=(TOK_PAD // SSM_CHUNK,),
        in_specs=[pl.BlockSpec((SSM_CHUNK, SSM_D_INNER), lambda c: (c, 0)),
                  pl.BlockSpec((SSM_CHUNK, gn), lambda c: (c, SSM_D_INNER // gn)),
                  pl.BlockSpec((SSM_CHUNK, gn), lambda c: (c, SSM_D_INNER // gn + 1)),
                  pl.BlockSpec((SSM_CHUNK, nh), lambda c: (clamp(c), 0)),
                  pl.BlockSpec((nh, SSM_CHUNK), lambda c: (0, clamp(c))),
                  col_vec, row_vec, col_vec, row_vec, pl.BlockSpec((1, SSM_D_INNER), lambda c: (0, 0))],
        out_specs=[pl.BlockSpec((SSM_CHUNK, SSM_D_INNER), lambda c: (c, 0)),
                   pl.BlockSpec((nh * SSM_HEAD_DIM, SSM_D_STATE), lambda c: (0, 0))],
        out_shape=[jax.ShapeDtypeStruct((TOK_PAD, SSM_D_INNER), F32),
                   jax.ShapeDtypeStruct((nh * SSM_HEAD_DIM, SSM_D_STATE), F32)],
        compiler_params=_params("arbitrary"),
        name="ssd_prompt",
    )(xbc, xbc, xbc, dt_raw, dt_raw.T,
      dt_bias.reshape(1, nh), dt_bias.reshape(nh, 1), a_log.reshape(1, nh), a_log.reshape(nh, 1), d_lanes)


def _mxu_round(x):
    return x.astype(BF16).astype(F32)


def _ssd_sample_kernel(h0_ref, xt_ref, xl_ref, b_ref, c_ref, dt_ref, dtl_ref, bias_ref, biasl_ref,
                       alog_ref, alogl_ref, d_ref, h_ref, y_ref):
    dt = _softplus(dt_ref[...] + bias_ref[...])
    dec = jnp.exp(dt * -jnp.exp(alog_ref[...]))
    dt_l = _softplus(dtl_ref[...] + biasl_ref[...])
    dec_l = jnp.exp(dt_l * -jnp.exp(alogl_ref[...]))
    xdt_t = _mxu_round(xt_ref[...] * dt)
    xdt_l = _mxu_round(xl_ref[...] * dt_l)
    bm = _mxu_round(b_ref[...])
    cm = _mxu_round(c_ref[...])
    for grp in range(SSM_N_GROUPS):
        b_row = bm[grp:grp + 1, :]
        c_row = cm[grp:grp + 1, :]
        cb = _mxu_round(jnp.sum(c_row * b_row, axis=1, keepdims=True))
        old = []
        for r in range(SSM_HEADS_PER_GROUP):
            h = grp * SSM_HEADS_PER_GROUP + r
            rows = slice(h * SSM_HEAD_DIM, (h + 1) * SSM_HEAD_DIM)
            h0 = h0_ref[rows, :]
            h_ref[rows, :] = h0 * dec[:, h:h + 1] + xdt_t[:, h:h + 1] * b_row
            old.append(h0)
        h0_grp = jnp.concatenate(old, axis=0).astype(BF16)
        c_rows = jnp.broadcast_to(c_row, (8, SSM_D_STATE)).astype(BF16)
        ch0 = lax.dot_general(c_rows, h0_grp, (((1,), (1,)), ((), ())), preferred_element_type=F32)[0:1, :]
        cols = slice(grp * SSD_GROUP_COLS, (grp + 1) * SSD_GROUP_COLS)
        y_ref[:, cols] = (cb * xdt_l[:, cols] + ch0 * dec_l[:, cols]) + d_ref[:, cols] * xl_ref[:, cols]


def ssd_sample(h0, x, bm, cm, dt_raw, dt_bias, a_log, d_lanes):
    nb = h0.shape[0]
    nh, p = SSM_N_HEADS, SSM_HEAD_DIM
    hp = nh * p
    lanes = lambda v: jnp.repeat(v, p, axis=-1)
    x_t = jnp.transpose(x.reshape(nb, nh, p), (0, 2, 1))
    per_b = lambda *shape: pl.BlockSpec((None,) + shape, lambda b: (b,) + (0,) * len(shape))
    vec = lambda n: pl.BlockSpec((1, n), lambda b: (0, 0))
    return pl.pallas_call(
        _ssd_sample_kernel,
        grid=(nb,),
        in_specs=[per_b(hp, SSM_D_STATE), per_b(p, nh), per_b(1, hp),
                  per_b(SSM_N_GROUPS, SSM_D_STATE), per_b(SSM_N_GROUPS, SSM_D_STATE),
                  per_b(1, nh), per_b(1, hp), vec(nh), vec(hp), vec(nh), vec(hp), vec(hp)],
        out_specs=[per_b(hp, SSM_D_STATE), per_b(1, hp)],
        out_shape=[jax.ShapeDtypeStruct((nb, hp, SSM_D_STATE), F32),
                   jax.ShapeDtypeStruct((nb, 1, hp), F32)],
        compiler_params=_params("arbitrary"),
        name="ssd_sample",
    )(h0, x_t, x.reshape(nb, 1, hp), bm, cm, dt_raw.reshape(nb, 1, nh), lanes(dt_raw).reshape(nb, 1, hp),
      dt_bias.reshape(1, nh), lanes(dt_bias).reshape(1, hp), a_log.reshape(1, nh), lanes(a_log).reshape(1, hp),
      d_lanes)


KV_COLS = ATTN_N_KV_HEADS * ATTN_HEAD_DIM
Q_LANES = ATTN_GROUP * MOBA_BLOCK


def _kmeans_kernel(k_ref, o_ref):
    o_ref[...] = jnp.mean(k_ref[...], axis=0, keepdims=True).reshape(o_ref.shape)


def block_means(k):
    nb = SEQ // MOBA_BLOCK
    return pl.pallas_call(
        _kmeans_kernel,
        grid=(nb,),
        in_specs=[pl.BlockSpec((MOBA_BLOCK, KV_COLS), lambda i: (i, 0))],
        out_specs=pl.BlockSpec((None, 1, KV_COLS), lambda i: (i, 0, 0)),
        out_shape=jax.ShapeDtypeStruct((nb, 1, KV_COLS), F32),
        compiler_params=_params("arbitrary"),
        name="block_means",
    )(k)


def _top3_mask_rows(gate):
    n = gate.shape[0]
    rows = lax.broadcasted_iota(I32, gate.shape, 0)
    sel = jnp.zeros(gate.shape, jnp.bool_)
    for _ in range(MOBA_TOPK):
        m = jnp.max(gate, axis=0, keepdims=True)
        first = jnp.min(jnp.where(gate == m, rows, n), axis=0, keepdims=True)
        hit = jnp.logical_and(rows == first, m > NEG_INF)
        sel = jnp.logical_or(sel, hit)
        gate = jnp.where(rows == first, NEG_INF, gate)
    return sel


MASKED_SCORE = -1e30
LOG2_E = 1.4426950408889634


def _moba_prompt_kernel(qi_ref, jj_ref, q_ref, k_ref, v_ref, mean_ref, o_ref,
                        qaug_ref, s_ref, m_ref, l_ref, acc_ref):
    t = pl.program_id(0)
    i = qi_ref[t]
    j = jj_ref[t]
    nb = mean_ref.shape[0]
    q_scale = ATTN_HEAD_DIM ** -0.5 * LOG2_E
    hd = ATTN_HEAD_DIM

    @pl.when(i == nb)
    def _():
        o_ref[...] = jnp.zeros(o_ref.shape, o_ref.dtype)

    @pl.when(jnp.logical_and(i < nb, j == 0))
    def _():
        blk = lax.broadcasted_iota(I32, (nb, Q_LANES), 0)
        pad = jnp.zeros((LANES - nb, MOBA_BLOCK), F32)
        for g in range(ATTN_N_KV_HEADS):
            q_g = jnp.concatenate([q_ref[:, (ATTN_GROUP * g + r) * hd:(ATTN_GROUP * g + r + 1) * hd]
                                   for r in range(ATTN_GROUP)], axis=0)
            gate = lax.dot_general(mean_ref[:, g * hd:(g + 1) * hd].astype(BF16), q_g.astype(BF16),
                                   (((1,), (1,)), ((), ())), preferred_element_type=F32)
            sel = _top3_mask_rows(jnp.where(blk < i, gate, NEG_INF))
            bias = jnp.where(jnp.logical_or(sel, blk == i), 0.0, MASKED_SCORE)
            bias_q = jnp.concatenate(
                [jnp.concatenate([bias[:, r * MOBA_BLOCK:(r + 1) * MOBA_BLOCK], pad], axis=0).T
                 for r in range(ATTN_GROUP)], axis=0)
            qaug_ref[g] = jnp.concatenate([(q_g * q_scale).astype(BF16), bias_q.astype(BF16)], axis=1)
        m_ref[...] = jnp.full(m_ref.shape, NEG_INF, F32)
        l_ref[...] = jnp.zeros(l_ref.shape, F32)
        acc_ref[...] = jnp.zeros(acc_ref.shape, F32)

    def attend(own):
        blk_lane = lax.broadcasted_iota(I32, (MOBA_BLOCK, LANES), 1)
        onehot = jnp.where(blk_lane == i - j, 1.0, 0.0).astype(BF16)
        for g in range(ATTN_N_KV_HEADS):
            k_aug = jnp.concatenate([k_ref[:, g * hd:(g + 1) * hd].astype(BF16), onehot], axis=1)
            s_ref[g] = lax.dot_general(k_aug, qaug_ref[g], (((1,), (1,)), ((), ())), preferred_element_type=F32)
        if own:
            key = lax.broadcasted_iota(I32, (MOBA_BLOCK, Q_LANES), 0)
            qry = lax.broadcasted_iota(I32, (MOBA_BLOCK, Q_LANES), 1) & (MOBA_BLOCK - 1)
            causal = key <= qry
        for g in range(ATTN_N_KV_HEADS):
            s = s_ref[g]
            if own:
                s = jnp.where(causal, s, NEG_INF)
            m_old = m_ref[g]
            m_new = jnp.maximum(m_old, jnp.max(s, axis=0, keepdims=True))
            alpha = jnp.exp2(m_old - m_new)
            p = jnp.exp2(s - m_new)
            l_ref[g] = alpha * l_ref[g] + jnp.sum(p, axis=0, keepdims=True)
            m_ref[g] = m_new
            v_t = v_ref[:, g * hd:(g + 1) * hd].T.astype(BF16)
            acc_ref[g] = alpha * acc_ref[g] + jnp.dot(v_t, p.astype(BF16), preferred_element_type=F32)

    @pl.when(jnp.logical_and(i < nb, j == 0))
    def _():
        attend(True)

    @pl.when(jnp.logical_and(i < nb, j > 0))
    def _():
        attend(False)

    @pl.when(jnp.logical_and(i < nb, j == i))
    def _():
        for g in range(ATTN_N_KV_HEADS):
            out = (acc_ref[g] / l_ref[g]).T
            for r in range(ATTN_GROUP):
                h = ATTN_GROUP * g + r
                o_ref[:, h * hd:(h + 1) * hd] = out[r * MOBA_BLOCK:(r + 1) * MOBA_BLOCK, :].astype(o_ref.dtype)


def moba_prompt(q, k, v, means):
    nb = SEQ // MOBA_BLOCK
    assert nb <= LANES
    steps = [(i, j) for i in range(nb) for j in range(i + 1)] + [(nb, 0)]
    qi = jnp.asarray(np.array([s[0] for s in steps], np.int32))
    jj = jnp.asarray(np.array([s[1] for s in steps], np.int32))
    kv_spec = pl.BlockSpec((MOBA_BLOCK, KV_COLS), lambda t, qi, jj: (jnp.minimum(qi[t] - jj[t], nb - 1), 0))
    row_spec = pl.BlockSpec((MOBA_BLOCK, D_MODEL), lambda t, qi, jj: (qi[t], 0))
    return pl.pallas_call(
        _moba_prompt_kernel,
        grid_spec=pltpu.PrefetchScalarGridSpec(
            num_scalar_prefetch=2,
            grid=(len(steps),),
            in_specs=[row_spec, kv_spec, kv_spec, pl.BlockSpec((nb, KV_COLS), lambda t, qi, jj: (0, 0))],
            out_specs=row_spec,
            scratch_shapes=[pltpu.VMEM((ATTN_N_KV_HEADS, Q_LANES, ATTN_HEAD_DIM + LANES), BF16),
                            pltpu.VMEM((ATTN_N_KV_HEADS, MOBA_BLOCK, Q_LANES), F32),
                            pltpu.VMEM((ATTN_N_KV_HEADS, 1, Q_LANES), F32),
                            pltpu.VMEM((ATTN_N_KV_HEADS, 1, Q_LANES), F32),
                            pltpu.VMEM((ATTN_N_KV_HEADS, ATTN_HEAD_DIM, Q_LANES), F32)]),
        out_shape=jax.ShapeDtypeStruct((TOK_PAD, D_MODEL), BF16),
        compiler_params=_params("arbitrary"),
        name="moba_prompt",
    )(qi, jj, q, k, v, means)


PAGE_ROWS = PAGE_SIZE * ATTN_N_KV_HEADS
SEL_KEYS = MOBA_TOPK * MOBA_BLOCK


def _page_copy(pool_hbm, page, dst, sem):
    return pltpu.make_async_copy(pool_hbm.at[page], dst, sem)


def _sample_select_kernel(pt_ref, q_ref, kc_hbm, o_ref, kbuf, means_ref, sems):
    b = pl.program_id(0)
    slot = b % 2

    def fetch(seq, slot_, start):
        for p in range(N_PAGES):
            cp = _page_copy(kc_hbm, pt_ref[seq * N_PAGES + p], kbuf.at[slot_, p], sems.at[slot_])
            cp.start() if start else cp.wait()

    @pl.when(b == 0)
    def _():
        fetch(b, slot, True)

    @pl.when(b + 1 < pl.num_programs(0))
    def _():
        fetch(b + 1, 1 - slot, True)

    fetch(b, slot, False)
    fold = 8 // ATTN_N_KV_HEADS
    for n in range(N_PAST_BLOCKS):
        s = jnp.zeros((8, ATTN_HEAD_DIM), F32)
        for p in range(PAGES_PER_BLOCK):
            s = s + jnp.sum(kbuf[slot, n * PAGES_PER_BLOCK + p].reshape(PAGE_ROWS // 8, 8, ATTN_HEAD_DIM), axis=0)
        tot = s[:ATTN_N_KV_HEADS, :]
        for f in range(1, fold):
            tot = tot + s[f * ATTN_N_KV_HEADS:(f + 1) * ATTN_N_KV_HEADS, :]
        means_ref[n * ATTN_N_KV_HEADS:(n + 1) * ATTN_N_KV_HEADS, :] = tot * (1.0 / MOBA_BLOCK)
    gate = lax.dot_general(q_ref[...].astype(BF16), means_ref[...].astype(BF16), (((1,), (1,)), ((), ())),
                           preferred_element_type=F32)
    shape = gate.shape
    lane = lax.broadcasted_iota(I32, shape, 1)
    head = lax.broadcasted_iota(I32, shape, 0)
    gate = jnp.where((lane & (ATTN_N_KV_HEADS - 1)) == (head >> 2), gate, NEG_INF)
    out = jnp.zeros(shape, I32)
    for t in range(MOBA_TOPK):
        m = jnp.max(gate, axis=1, keepdims=True)
        first = jnp.min(jnp.where(gate == m, lane, shape[1]), axis=1, keepdims=True)
        out = jnp.where(lane == t, first >> 2, out)
        gate = jnp.where(lane == first, NEG_INF, gate)
    o_ref[...] = out


def sample_select(page_table, q_s, kc_pages):
    nb = q_s.shape[0]
    assert N_PAST_BLOCKS * ATTN_N_KV_HEADS == LANES
    return pl.pallas_call(
        _sample_select_kernel,
        grid_spec=pltpu.PrefetchScalarGridSpec(
            num_scalar_prefetch=1,
            grid=(nb,),
            in_specs=[pl.BlockSpec((None, ATTN_N_HEADS, ATTN_HEAD_DIM), lambda b, pt: (b, 0, 0)),
                      pl.BlockSpec(memory_space=pl.ANY)],
            out_specs=pl.BlockSpec((None, ATTN_N_HEADS, LANES), lambda b, pt: (b, 0, 0)),
            scratch_shapes=[pltpu.VMEM((2, N_PAGES, PAGE_ROWS, ATTN_HEAD_DIM), F32),
                            pltpu.VMEM((N_PAST_BLOCKS * ATTN_N_KV_HEADS, ATTN_HEAD_DIM), F32),
                            pltpu.SemaphoreType.DMA((2,))]),
        out_shape=jax.ShapeDtypeStruct((nb, ATTN_N_HEADS, LANES), I32),
        compiler_params=_params("arbitrary"),
        name="sample_select",
    )(page_table.reshape(-1), q_s, kc_pages)


def _head_rows_copy(pool_hbm, page, g, dst, sem):
    return pltpu.make_async_copy(pool_hbm.at[page, :, g, :], dst, sem)


def _sample_attend_kernel(pt_ref, sel_ref, q_ref, kn_ref, vn_ref, kc_hbm, vc_hbm, o_ref, kbuf, vbuf, sems):
    b = pl.program_id(0)
    slot = b % 2
    scale = ATTN_HEAD_DIM ** -0.5

    def fetch(seq, slot_, start):
        for h in range(ATTN_N_HEADS):
            g = h // ATTN_GROUP
            for t in range(MOBA_TOPK):
                blk = sel_ref[(seq * ATTN_N_HEADS + h) * MOBA_TOPK + t]
                for half in range(PAGES_PER_BLOCK):
                    page = pt_ref[seq * N_PAGES + blk * PAGES_PER_BLOCK + half]
                    rows = pl.ds(t * MOBA_BLOCK + half * PAGE_SIZE, PAGE_SIZE)
                    for cp in (_head_rows_copy(kc_hbm, page, g, kbuf.at[slot_, h, rows, :], sems.at[0, slot_]),
                               _head_rows_copy(vc_hbm, page, g, vbuf.at[slot_, h, rows, :], sems.at[1, slot_])):
                        cp.start() if start else cp.wait()

    @pl.when(b == 0)
    def _():
        fetch(b, slot, True)

    @pl.when(b + 1 < pl.num_programs(0))
    def _():
        fetch(b + 1, 1 - slot, True)

    fetch(b, slot, False)

    for h in range(ATTN_N_HEADS):
        g = h // ATTN_GROUP
        qh = q_ref[h:h + 1, :]
        q8 = jnp.broadcast_to(qh, (8, ATTN_HEAD_DIM)).astype(BF16)
        s = lax.dot_general(kbuf[slot, h].astype(BF16), q8, (((1,), (1,)), ((), ())),
                            preferred_element_type=F32)[:, 0:1] * scale
        s_own = jnp.sum(_mxu_round(qh) * _mxu_round(kn_ref[g:g + 1, :]), axis=1, keepdims=True) * scale
        m = jnp.maximum(jnp.max(s, axis=0, keepdims=True), s_own)
        p = jnp.exp(s - m)
        p_own = jnp.exp(s_own - m)
        denom = jnp.sum(p, axis=0, keepdims=True) + p_own
        o_sel = jnp.sum(_mxu_round(p / denom) * _mxu_round(vbuf[slot, h]), axis=0, keepdims=True)
        o_ref[h:h + 1, :] = o_sel + _mxu_round(p_own / denom) * _mxu_round(vn_ref[g:g + 1, :])


def sample_attend(page_table, sel, q_s, k_new, v_new, cache_k, cache_v):
    nb = q_s.shape[0]
    per_b = lambda rows: pl.BlockSpec((None, rows, ATTN_HEAD_DIM), lambda b, pt, sl: (b, 0, 0))
    any_spec = pl.BlockSpec(memory_space=pl.ANY)
    return pl.pallas_call(
        _sample_attend_kernel,
        grid_spec=pltpu.PrefetchScalarGridSpec(
            num_scalar_prefetch=2,
            grid=(nb,),
            in_specs=[per_b(ATTN_N_HEADS), per_b(ATTN_N_KV_HEADS), per_b(ATTN_N_KV_HEADS), any_spec, any_spec],
            out_specs=per_b(ATTN_N_HEADS),
            scratch_shapes=[pltpu.VMEM((2, ATTN_N_HEADS, SEL_KEYS, ATTN_HEAD_DIM), F32),
                            pltpu.VMEM((2, ATTN_N_HEADS, SEL_KEYS, ATTN_HEAD_DIM), F32),
                            pltpu.SemaphoreType.DMA((2, 2))]),
        out_shape=jax.ShapeDtypeStruct((nb, ATTN_N_HEADS, ATTN_HEAD_DIM), F32),
        compiler_params=_params("arbitrary"),
        name="sample_attend",
    )(page_table.reshape(-1), sel.reshape(-1), q_s, k_new, v_new, cache_k, cache_v)


def _router_kernel(x_ref, w_ref, b_ref, idx_ref, gate_ref):
    logits = jnp.dot(x_ref[...], w_ref[...].astype(BF16), preferred_element_type=F32) + b_ref[...]
    lane = lax.broadcasted_iota(I32, logits.shape, 1)
    logits = jnp.where(lane < N_EXPERTS, logits, NEG_INF)
    m1 = jnp.max(logits, axis=1, keepdims=True)
    i1 = jnp.min(jnp.where(logits == m1, lane, LANES), axis=1, keepdims=True)
    rest = jnp.where(lane == i1, NEG_INF, logits)
    m2 = jnp.max(rest, axis=1, keepdims=True)
    i2 = jnp.min(jnp.where(rest == m2, lane, LANES), axis=1, keepdims=True)
    e2 = jnp.exp(m2 - m1)
    g1 = 1.0 / (1.0 + e2)
    idx_ref[...] = jnp.where(lane == 0, i1, jnp.where(lane == 1, i2, 0))
    gate_ref[...] = jnp.where(lane == 0, g1, jnp.where(lane == 1, e2 * g1, 0.0))


def router(x, w_router, b_router):
    m, d = x.shape
    wp = jnp.zeros((d, LANES), F32).at[:, :N_EXPERTS].set(w_router)
    bp = jnp.zeros((1, LANES), F32).at[0, :N_EXPERTS].set(b_router)
    row = pl.BlockSpec((ROW_BLOCK, d), lambda i: (i, 0))
    out = pl.BlockSpec((ROW_BLOCK, LANES), lambda i: (i, 0))
    return pl.pallas_call(
        _router_kernel,
        grid=(m // ROW_BLOCK,),
        in_specs=[row, pl.BlockSpec((d, LANES), lambda i: (0, 0)), pl.BlockSpec((1, LANES), lambda i: (0, 0))],
        out_specs=[out, out],
        out_shape=[jax.ShapeDtypeStruct((m, LANES), I32), jax.ShapeDtypeStruct((m, LANES), F32)],
        compiler_params=_params("arbitrary"),
        name="router",
    )(x, wp, bp)


def _row_copy(src_hbm, row, dst, slot, sem):
    return pltpu.make_async_copy(src_hbm.at[pl.ds(row, 1), :], dst.at[pl.ds(slot, 1), :], sem)


GATHER_UNROLL = 8


def _moe_gather_kernel(tok_ref, nxt_ref, x_hbm, o_ref, buf, sems):
    i = pl.program_id(0)
    slot = i % 2

    def issue(idx_ref, slot_):
        def body(r2, carry):
            for q in range(2):
                r = 2 * r2 + q
                _row_copy(x_hbm, idx_ref[0, 0, r], buf.at[slot_], r, sems.at[slot_]).start(priority=q)
            return carry
        lax.fori_loop(0, MOE_ROWS // 2, body, 0, unroll=GATHER_UNROLL // 2)

    @pl.when(i == 0)
    def _():
        issue(tok_ref, slot)

    @pl.when(i + 1 < pl.num_programs(0))
    def _():
        issue(nxt_ref, 1 - slot)

    def wait(r, carry):
        _row_copy(x_hbm, 0, buf.at[slot], r, sems.at[slot]).wait()
        return carry

    lax.fori_loop(0, MOE_ROWS, wait, 0, unroll=GATHER_UNROLL)
    o_ref[...] = buf[slot].astype(o_ref.dtype)


def moe_gather(x, buf_tok):
    d = x.shape[1]
    tok = buf_tok.reshape(MOE_BLOCKS, 1, MOE_ROWS)
    idx_spec = lambda step: pl.BlockSpec((1, 1, MOE_ROWS), lambda i: (jnp.minimum(i + step, MOE_BLOCKS - 1), 0, 0),
                                         memory_space=pltpu.SMEM)
    return pl.pallas_call(
        _moe_gather_kernel,
        grid=(MOE_BLOCKS,),
        in_specs=[idx_spec(0), idx_spec(1), pl.BlockSpec(memory_space=pl.ANY)],
        out_specs=pl.BlockSpec((MOE_ROWS, d), lambda i: (i, 0)),
        out_shape=jax.ShapeDtypeStruct((MOE_BUF, d), BF16),
        scratch_shapes=[pltpu.VMEM((2, MOE_ROWS, d), F32), pltpu.SemaphoreType.DMA((2,))],
        compiler_params=_params("arbitrary"),
        name="moe_gather",
    )(tok, tok, x)


W_CHUNKS = 8


def _moe_expert_kernel(first_ref, count_ref, used_ref, w_hbm, x_hbm, o_hbm,
                       wstage, wb_ref, xbuf, obuf, wsem, xsem, osem, *, n_w, nj, bn):
    e = pl.program_id(0)
    j = pl.program_id(1)
    n_e = pl.num_programs(0)
    step = e * nj + j
    slot_w = step % 2
    first = first_ref[e]
    n = count_ref[e]
    cols = pl.ds(pl.multiple_of(j * bn, bn), bn)
    kc = w_hbm.shape[1] // W_CHUNKS

    def rows(blk):
        return pl.ds(pl.multiple_of(blk * MOE_ROWS, MOE_ROWS), MOE_ROWS)

    def x_copy(k, slot):
        return pltpu.make_async_copy(x_hbm.at[rows(first + k), :], xbuf.at[slot], xsem.at[slot])

    def o_copy(blk, slot):
        return pltpu.make_async_copy(obuf.at[slot], o_hbm.at[rows(blk), cols], osem.at[slot])

    def w_copies(e_, j_, slot, c):
        krows = pl.ds(pl.multiple_of(c * kc, kc), kc)
        return [pltpu.make_async_copy(
            w_hbm.at[e_, krows, pl.ds(pl.multiple_of((j_ + m * nj) * bn, bn), bn)],
            wstage.at[slot, m, krows, :], wsem.at[slot]) for m in range(n_w)]

    last_j = j == nj - 1
    e_next = jnp.where(last_j, e + 1, e)
    j_next = jnp.where(last_j, 0, j + 1)
    has_next = step + 1 < n_e * nj

    def fetch_next(c):
        for cp in w_copies(e_next, j_next, 1 - slot_w, c):
            cp.start()

    @pl.when(step == 0)
    def _():
        for c in range(W_CHUNKS):
            for cp in w_copies(e, j, slot_w, c):
                cp.start()

    @pl.when(n > 0)
    def _():
        x_copy(0, 0).start()

    for c in range(W_CHUNKS):
        for cp in w_copies(e, j, slot_w, c):
            cp.wait()

    @pl.when(n > 0)
    def _():
        for m in range(n_w):
            wb_ref[m] = wstage[slot_w, m].astype(BF16)

    def body(k, carry):
        slot = k % 2
        x_copy(k, slot).wait()

        @pl.when(k + 1 < n)
        def _():
            x_copy(k + 1, 1 - slot).start()

        @pl.when(jnp.logical_and(has_next, k < W_CHUNKS))
        def _():
            fetch_next(k)

        @pl.when(k >= 2)
        def _():
            o_copy(first + k - 2, slot).wait()

        x = xbuf[slot]
        acc = jnp.dot(x, wb_ref[0], preferred_element_type=F32)
        if n_w == 2:
            acc = _silu(acc) * jnp.dot(x, wb_ref[1], preferred_element_type=F32)
        obuf[slot] = acc.astype(obuf.dtype)
        o_copy(first + k, slot).start()
        return carry

    lax.fori_loop(0, n, body, 0)

    @pl.when(has_next)
    def _():
        def rest(c, carry):
            fetch_next(c)
            return carry

        lax.fori_loop(jnp.minimum(n, W_CHUNKS), W_CHUNKS, rest, 0)

    @pl.when(n >= 2)
    def _():
        o_copy(first + n - 2, n % 2).wait()

    @pl.when(n >= 1)
    def _():
        o_copy(first + n - 1, (n - 1) % 2).wait()

    @pl.when(e == n_e - 1)
    def _():
        obuf[0] = jnp.zeros(obuf.shape[1:], obuf.dtype)

        def fill(blk, carry):
            o_copy(blk, 0).start()
            o_copy(blk, 0).wait()
            return carry

        lax.fori_loop(used_ref[0], MOE_BLOCKS, fill, 0)


def moe_expert_matmul(xs, w, plan, *, swiglu, bn, out_dtype):
    first, count, used = plan
    r, k = xs.shape
    n_w = 2 if swiglu else 1
    f = w.shape[2] // n_w
    nj = f // bn
    assert f % bn == 0 and k % (8 * W_CHUNKS) == 0
    hbm = pl.BlockSpec(memory_space=pl.ANY)
    return pl.pallas_call(
        functools.partial(_moe_expert_kernel, n_w=n_w, nj=nj, bn=bn),
        grid_spec=pltpu.PrefetchScalarGridSpec(
            num_scalar_prefetch=3,
            grid=(N_EXPERTS, nj),
            in_specs=[hbm, hbm],
            out_specs=hbm,
            scratch_shapes=[pltpu.VMEM((2, n_w, k, bn), F32), pltpu.VMEM((n_w, k, bn), BF16),
                            pltpu.VMEM((2, MOE_ROWS, k), xs.dtype), pltpu.VMEM((2, MOE_ROWS, bn), out_dtype),
                            pltpu.SemaphoreType.DMA((2,)), pltpu.SemaphoreType.DMA((2,)),
                            pltpu.SemaphoreType.DMA((2,))]),
        out_shape=jax.ShapeDtypeStruct((r, f), out_dtype),
        compiler_params=_params("arbitrary", "arbitrary"),
        name="moe_gu" if swiglu else "moe_down",
    )(first, count, used, w, xs)


def _moe_combine_ln_kernel(pos_ref, nxt_ref, y_hbm, x_ref, gate_ref, g_ref, b_ref, o_ref, ob_ref, buf, sems):
    i = pl.program_id(0)
    slot = i % 2

    def issue(idx_ref, slot_):
        def body(r, carry):
            for t in range(MOE_TOPK):
                _row_copy(y_hbm, idx_ref[0, 0, MOE_TOPK * r + t], buf.at[slot_, t], r,
                          sems.at[slot_]).start(priority=t % 2)
            return carry
        lax.fori_loop(0, ROW_BLOCK, body, 0, unroll=GATHER_UNROLL)

    @pl.when(i == 0)
    def _():
        issue(pos_ref, slot)

    @pl.when(i + 1 < pl.num_programs(0))
    def _():
        issue(nxt_ref, 1 - slot)

    def wait(r, carry):
        for t in range(MOE_TOPK):
            _row_copy(y_hbm, 0, buf.at[slot, t], r, sems.at[slot]).wait()
        return carry

    lax.fori_loop(0, ROW_BLOCK, wait, 0, unroll=GATHER_UNROLL)
    gates = gate_ref[...]
    ff = buf[slot, 0] * gates[:, 0:1]
    for t in range(1, MOE_TOPK):
        ff = ff + buf[slot, t] * gates[:, t:t + 1]
    out = _layernorm_rows(DEEPNORM_ALPHA * x_ref[...] + ff, g_ref[...], b_ref[...])
    o_ref[...] = out
    ob_ref[...] = out.astype(BF16)


def moe_combine_ln(x, ys, pos, gates, g, b):
    m, d = x.shape
    nblk = m // ROW_BLOCK
    row = pl.BlockSpec((ROW_BLOCK, d), lambda i: (i, 0))
    vec = pl.BlockSpec((1, d), lambda i: (0, 0))
    pos3 = pos.reshape(nblk, 1, MOE_TOPK * ROW_BLOCK)
    idx_spec = lambda step: pl.BlockSpec((1, 1, MOE_TOPK * ROW_BLOCK),
                                         lambda i: (jnp.minimum(i + step, nblk - 1), 0, 0), memory_space=pltpu.SMEM)
    return pl.pallas_call(
        _moe_combine_ln_kernel,
        grid=(nblk,),
        in_specs=[idx_spec(0), idx_spec(1), pl.BlockSpec(memory_space=pl.ANY),
                  row, pl.BlockSpec((ROW_BLOCK, LANES), lambda i: (i, 0)), vec, vec],
        out_specs=[row, row],
        out_shape=[jax.ShapeDtypeStruct((m, d), F32), jax.ShapeDtypeStruct((m, d), BF16)],
        scratch_shapes=[pltpu.VMEM((2, MOE_TOPK, ROW_BLOCK, d), F32), pltpu.SemaphoreType.DMA((2,))],
        compiler_params=_params("arbitrary"),
        name="moe_combine_ln",
    )(pos3, pos3, ys, x, gates, g.reshape(1, d), b.reshape(1, d))


def moe_plan(idx):
    e = idx[:N_TOK, :MOE_TOPK].reshape(N_ASSIGN)
    onehot = (e[:, None] == jnp.arange(N_EXPERTS, dtype=I32)[None, :]).astype(I32)
    before = jnp.cumsum(onehot, axis=0) - onehot
    rank = jnp.sum(before * onehot, axis=1)
    counts = jnp.sum(onehot, axis=0)
    padded = (counts + MOE_ROWS - 1) // MOE_ROWS * MOE_ROWS
    pad_end = jnp.cumsum(padded)
    pad_start = pad_end - padded
    dest = pad_start[e] + rank
    tok = jnp.arange(N_ASSIGN, dtype=I32) // MOE_TOPK
    buf_tok = jnp.zeros((MOE_BUF,), I32).at[dest].set(tok)
    blocks = ((pad_start // MOE_ROWS).astype(I32), (padded // MOE_ROWS).astype(I32),
              (pad_end[-1:] // MOE_ROWS).astype(I32))
    pos = jnp.zeros((TOK_PAD * MOE_TOPK,), I32).at[:N_ASSIGN].set(dest)
    return buf_tok, blocks, pos


def _rope_tables():
    half = ROT_DIM // 2
    inv = ROPE_THETA ** (-jnp.arange(half, dtype=F32) * 2.0 / ROT_DIM)
    pos = jnp.concatenate([jnp.arange(SEQ, dtype=I32), jnp.full((TOK_PAD - SEQ,), PAST_LEN, I32)])
    ang = pos.astype(F32)[:, None] * inv[None, :]
    cos, sin = jnp.cos(ang), jnp.sin(ang)
    rest = LANES - ROT_DIM
    cf = jnp.concatenate([cos, cos, jnp.ones((TOK_PAD, rest), F32)], axis=1)
    s1 = jnp.concatenate([-sin, jnp.zeros((TOK_PAD, LANES - half), F32)], axis=1)
    s2 = jnp.concatenate([jnp.zeros((TOK_PAD, half), F32), sin, jnp.zeros((TOK_PAD, rest), F32)], axis=1)
    return cf, s1, s2


def _tokens(prompt, sample):
    pad = jnp.zeros((TOK_PAD - N_TOK, prompt.shape[1]), prompt.dtype)
    return jnp.concatenate([prompt, sample, pad], axis=0)


def _put_sample(buf, rows):
    tail = jnp.zeros((TOK_PAD - SEQ, buf.shape[1]), buf.dtype).at[:rows.shape[0]].set(rows.astype(buf.dtype))
    return lax.dynamic_update_slice(buf, tail, (SEQ, 0))


def kernel(x_prompt, x_sample, state_ssm, state_conv, cache_k, cache_v, page_table, p_prompt, p_sample, ln_g, ln_b, ssm_w_in, ssm_conv_w, ssm_conv_b, ssm_dt_bias, ssm_A_log, ssm_D, ssm_norm_w, ssm_w_out, attn_w_kv, attn_w_q, attn_w_o, ffn_w_gu, ffn_w_down, moe_w_router, moe_b_router, moe_w_gu, moe_w_down, ple_w_gate, ple_b_gate, ple_w_proj):
    x0b = _tokens(x_prompt[0].astype(BF16), x_sample[:, 0].astype(BF16))
    x0_tail = jnp.zeros((TOK_PAD - SEQ, D_MODEL), F32).at[:DEC_BATCH].set(x_sample[:, 0])
    p_tok = [_tokens(p_prompt[i, 0], p_sample[i, :, 0]).astype(BF16) for i in range(2)]

    w_in_t = jnp.swapaxes(ssm_w_in, 1, 2)[0]
    zx_cols = SSM_D_INNER + SSM_CONV_DIM
    zx = matmul(x0b, w_in_t, bn=1024, ncols=zx_cols, w_transposed=True)
    dt_raw = matmul(x0b, w_in_t, bn=LANES, col0=zx_cols, ncols=LANES, w_transposed=True)[:, :SSM_N_HEADS]

    xbc = conv_prompt(zx, ssm_conv_w[0], ssm_conv_b[0])
    st_conv = state_conv[0]
    x_raw_s = zx[SEQ:N_TOK, SSM_D_INNER:]
    xbc_s = conv_sample(jnp.transpose(st_conv, (1, 0, 2)), x_raw_s, ssm_conv_w[0], ssm_conv_b[0])
    conv_prompt_out = zx[SEQ - (SSM_CONV_W - 1):SEQ, SSM_D_INNER:].reshape(1, 1, SSM_CONV_W - 1, SSM_CONV_DIM)
    conv_sample_out = jnp.concatenate([st_conv[:, 1:], x_raw_s[:, None, :]], axis=1)[None]

    d_lanes = jnp.repeat(ssm_D[0], SSM_HEAD_DIM).reshape(1, SSM_D_INNER)
    y, h_prompt = ssd_prompt(xbc, dt_raw[:SEQ], ssm_dt_bias[0], ssm_A_log[0], d_lanes)
    xs_s = xbc_s[:DEC_BATCH]
    gn = SSM_N_GROUPS * SSM_D_STATE
    h_sample, y_s = ssd_sample(
        state_ssm[0].reshape(DEC_BATCH, SSM_N_HEADS * SSM_HEAD_DIM, SSM_D_STATE),
        xs_s[:, :SSM_D_INNER],
        xs_s[:, SSM_D_INNER:SSM_D_INNER + gn].reshape(DEC_BATCH, SSM_N_GROUPS, SSM_D_STATE),
        xs_s[:, SSM_D_INNER + gn:].reshape(DEC_BATCH, SSM_N_GROUPS, SSM_D_STATE),
        dt_raw[SEQ:N_TOK], ssm_dt_bias[0], ssm_A_log[0], d_lanes)
    y = _put_sample(y, y_s.reshape(DEC_BATCH, SSM_D_INNER))
    gated = gate_norm(y, zx, ssm_norm_w[0])
    mix = matmul(gated, ssm_w_out[0], bn=512)
    x1, x1b = add_ln(x_prompt[0], mix, ln_g[0, 0], ln_b[0, 0], x_tail=x0_tail)

    hid = matmul_swiglu(x1b, ffn_w_gu[0], bn=512)
    ff = matmul(hid, ffn_w_down[0], bn=512)
    x2, x2b = add_ln(x1, ff, ln_g[0, 1], ln_b[0, 1])
    x3, x3b = ple_mix(x2, x2b, p_tok[0], ple_w_gate, ple_b_gate, ple_w_proj, 0)

    rope = _rope_tables()
    k_all = matmul(x3b, attn_w_kv, bn=KV_COLS, ncols=KV_COLS, rope=rope)
    v_all = matmul(x3b, attn_w_kv, bn=KV_COLS, col0=KV_COLS, ncols=KV_COLS)
    q_all = matmul(x3b, attn_w_q[0], bn=1024, rope=rope)
    att = moba_prompt(q_all, k_all, v_all, block_means(k_all).reshape(SEQ // MOBA_BLOCK, KV_COLS))

    q_s = q_all[SEQ:N_TOK].reshape(DEC_BATCH, ATTN_N_HEADS, ATTN_HEAD_DIM)
    k_s = k_all[SEQ:N_TOK].reshape(DEC_BATCH, ATTN_N_KV_HEADS, ATTN_HEAD_DIM)
    v_s = v_all[SEQ:N_TOK].reshape(DEC_BATCH, ATTN_N_KV_HEADS, ATTN_HEAD_DIM)
    n_pool = cache_k.shape[0]
    sel = sample_select(page_table, q_s, cache_k.reshape(n_pool, PAGE_ROWS, ATTN_HEAD_DIM))[:, :, :MOBA_TOPK]
    att_s = sample_attend(page_table, sel, q_s, k_s, v_s, cache_k, cache_v)
    att = _put_sample(att, att_s.reshape(DEC_BATCH, D_MODEL))
    mix = matmul(att, attn_w_o[0], bn=1024)
    x4, x4b = add_ln(x3, mix, ln_g[1, 0], ln_b[1, 0])

    idx, gates = router(x4b, moe_w_router[0], moe_b_router[0])
    buf_tok, blocks, pos = moe_plan(idx)
    xs = moe_gather(x4, buf_tok)
    hs = moe_expert_matmul(xs, moe_w_gu[0], blocks, swiglu=True, bn=1024, out_dtype=BF16)
    ys = moe_expert_matmul(hs, moe_w_down[0], blocks, swiglu=False, bn=512, out_dtype=F32)
    x5, x5b = moe_combine_ln(x4, ys, pos, gates, ln_g[1, 1], ln_b[1, 1])
    y_prompt, y_sample = ple_mix(x5, x5b, p_tok[1], ple_w_gate, ple_b_gate, ple_w_proj, 1, final=True)

    state_shape = (1, -1, SSM_N_HEADS, SSM_HEAD_DIM, SSM_D_STATE)
    kv_shape = (-1, ATTN_N_KV_HEADS, ATTN_HEAD_DIM)
    return (y_prompt.reshape(1, SEQ, D_MODEL),
            y_sample.reshape(DEC_BATCH, 1, D_MODEL),
            h_prompt.reshape(state_shape),
            conv_prompt_out,
            k_all[:SEQ].reshape((1,) + (SEQ,) + kv_shape[1:]),
            v_all[:SEQ].reshape((1,) + (SEQ,) + kv_shape[1:]),
            h_sample.reshape(state_shape),
            conv_sample_out,
            k_all[SEQ:N_TOK].reshape((DEC_BATCH, 1) + kv_shape[1:]),
            v_all[SEQ:N_TOK].reshape((DEC_BATCH, 1) + kv_shape[1:]))
```

```python
import functools

import jax
import jax.numpy as jnp
import numpy as np
from jax import lax
from jax.experimental import pallas as pl
from jax.experimental.pallas import tpu as pltpu

F32 = jnp.float32
BF16 = jnp.bfloat16
I32 = jnp.int32

D_MODEL = 2048
SEQ = 8192
DEC_BATCH = 32
PAST_LEN = 8192
PAGE_SIZE = 128

SSM_D_INNER = 4096
SSM_HEAD_DIM = 64
SSM_N_HEADS = 64
SSM_N_GROUPS = 8
SSM_D_STATE = 128
SSM_CONV_W = 4
SSM_CHUNK = 128
SSM_CONV_DIM = 6144
SSM_HEADS_PER_GROUP = SSM_N_HEADS // SSM_N_GROUPS

ATTN_HEAD_DIM = 128
ATTN_N_HEADS = 16
ATTN_N_KV_HEADS = 4
ATTN_GROUP = 4
ROT_DIM = 32
ROPE_THETA = 500000.0
MOBA_BLOCK = 256
MOBA_TOPK = 3
N_PAST_BLOCKS = PAST_LEN // MOBA_BLOCK
PAGES_PER_BLOCK = MOBA_BLOCK // PAGE_SIZE
N_PAGES = PAST_LEN // PAGE_SIZE

D_FF = 5632
N_EXPERTS = 8
MOE_TOPK = 2
D_FF_EXPERT = 7168
D_PLE = 256
DEEPNORM_ALPHA = 4.0 ** 0.25
LN_EPS = 1e-5
RMS_EPS = 1e-5

LANES = 128
VMEM_LIMIT_BYTES = 56 * 1024 * 1024

N_TOK = SEQ + DEC_BATCH
TOK_PAD = 8448
MM_ROWS = 768
ROW_BLOCK = 256
MOE_ROWS = 256
N_ASSIGN = N_TOK * MOE_TOPK
MOE_BLOCKS = -(-N_ASSIGN // MOE_ROWS) + N_EXPERTS
MOE_BUF = MOE_BLOCKS * MOE_ROWS
NEG_INF = float("-inf")


def _params(*sem):
    return pltpu.CompilerParams(dimension_semantics=sem, vmem_limit_bytes=VMEM_LIMIT_BYTES)


def _split3(a):
    hi = a.astype(BF16)
    r1 = a - hi.astype(F32)
    mid = r1.astype(BF16)
    lo = (r1 - mid.astype(F32)).astype(BF16)
    return hi, mid, lo


def _dot_f32(a, b, dims):
    a0, a1, a2 = _split3(a)
    b0, b1, b2 = _split3(b)
    dn = (dims, ((), ()))
    d = lambda x, y: lax.dot_general(x, y, dn, preferred_element_type=F32)
    return (d(a0, b0) + (d(a0, b1) + d(a1, b0))) + ((d(a0, b2) + d(a2, b0)) + d(a1, b1))


def _softplus(x):
    return jnp.maximum(x, 0.0) + jnp.log1p(jnp.exp(-jnp.abs(x)))


def _silu(x):
    return x * jax.nn.sigmoid(x)


def _rope_lanes(a, cf, s1, s2):
    return a * cf + pltpu.roll(a, LANES - ROT_DIM // 2, 1) * s1 + pltpu.roll(a, ROT_DIM // 2, 1) * s2


def _mm_kernel(x_ref, w_ref, *rest, rope, valid_cols, w_transposed):
    if rope:
        cf_ref, s1_ref, s2_ref, o_ref, wb_ref = rest
    else:
        o_ref, wb_ref = rest

    @pl.when(pl.program_id(1) == 0)
    def _():
        w = w_ref[...]
        if valid_cols is not None:
            axis = 0 if w_transposed else 1
            w = jnp.where(lax.broadcasted_iota(I32, w.shape, axis) < valid_cols, w, 0.0)
        wb_ref[...] = (w.T if w_transposed else w).astype(BF16)

    acc = jnp.dot(x_ref[...], wb_ref[...], preferred_element_type=F32)
    if rope:
        cf, s1, s2 = cf_ref[...], s1_ref[...], s2_ref[...]
        for c in range(acc.shape[1] // LANES):
            sl = slice(c * LANES, (c + 1) * LANES)
            o_ref[:, sl] = _rope_lanes(acc[:, sl], cf, s1, s2).astype(o_ref.dtype)
    else:
        o_ref[...] = acc.astype(o_ref.dtype)


def matmul(x, w, *, bn, col0=0, ncols=None, rope=None, out_dtype=F32, bm=MM_ROWS, w_transposed=False):
    m, k = x.shape
    n_total = w.shape[0] if w_transposed else w.shape[1]
    ncols = n_total if ncols is None else ncols
    assert m % bm == 0 and ncols % bn == 0 and col0 % bn == 0
    j0 = col0 // bn
    valid_cols = None
    if col0 + ncols > n_total:
        assert ncols == bn
        valid_cols = n_total - col0
    w_spec = (pl.BlockSpec((bn, k), lambda j, i: (j + j0, 0)) if w_transposed
              else pl.BlockSpec((k, bn), lambda j, i: (0, j + j0)))
    in_specs = [pl.BlockSpec((bm, k), lambda j, i: (i, 0)), w_spec]
    args = [x, w]
    if rope is not None:
        in_specs += [pl.BlockSpec((bm, LANES), lambda j, i: (i, 0))] * 3
        args += list(rope)
    return pl.pallas_call(
        functools.partial(_mm_kernel, rope=rope is not None, valid_cols=valid_cols, w_transposed=w_transposed),
        grid=(ncols // bn, m // bm),
        in_specs=in_specs,
        out_specs=pl.BlockSpec((bm, bn), lambda j, i: (i, j)),
        out_shape=jax.ShapeDtypeStruct((m, ncols), out_dtype),
        scratch_shapes=[pltpu.VMEM((k, bn), BF16)],
        compiler_params=_params("arbitrary", "arbitrary"),
        name="matmul",
    )(*args)


def _mm_swiglu_kernel(x_ref, wg_ref, wu_ref, o_ref, wgb_ref, wub_ref):
    @pl.when(pl.program_id(1) == 0)
    def _():
        wgb_ref[...] = wg_ref[...].astype(BF16)
        wub_ref[...] = wu_ref[...].astype(BF16)

    x = x_ref[...]
    g = jnp.dot(x, wgb_ref[...], preferred_element_type=F32)
    u = jnp.dot(x, wub_ref[...], preferred_element_type=F32)
    o_ref[...] = (_silu(g) * u).astype(o_ref.dtype)


def matmul_swiglu(x, w_gu, *, bn, bm=MM_ROWS):
    m, k = x.shape
    f = w_gu.shape[1] // 2
    assert m % bm == 0 and f % bn == 0
    nj = f // bn
    return pl.pallas_call(
        _mm_swiglu_kernel,
        grid=(nj, m // bm),
        in_specs=[pl.BlockSpec((bm, k), lambda j, i: (i, 0)),
                  pl.BlockSpec((k, bn), lambda j, i: (0, j)),
                  pl.BlockSpec((k, bn), lambda j, i: (0, j + nj))],
        out_specs=pl.BlockSpec((bm, bn), lambda j, i: (i, j)),
        out_shape=jax.ShapeDtypeStruct((m, f), BF16),
        scratch_shapes=[pltpu.VMEM((k, bn), BF16), pltpu.VMEM((k, bn), BF16)],
        compiler_params=_params("arbitrary", "arbitrary"),
        name="matmul_swiglu",
    )(x, w_gu, w_gu)


def _ple_kernel(xb_ref, pb_ref, wg_ref, wp_ref, bg_ref, x_ref, o_ref, o2_ref, wgb_ref, wpb_ref, *, final):
    @pl.when(pl.program_id(1) == 0)
    def _():
        wgb_ref[...] = wg_ref[...].astype(BF16)
        wpb_ref[...] = wp_ref[...].astype(BF16)

    gate = jax.nn.sigmoid(jnp.dot(xb_ref[...], wgb_ref[...], preferred_element_type=F32) + bg_ref[...])
    proj = jnp.dot(pb_ref[...], wpb_ref[...], preferred_element_type=F32)
    out = x_ref[...] + gate * proj
    o_ref[...] = out
    if final:
        tail0 = SEQ - (TOK_PAD - MM_ROWS)

        @pl.when(pl.program_id(1) == pl.num_programs(1) - 1)
        def _():
            o2_ref[...] = out[tail0:tail0 + DEC_BATCH, :]
    else:
        o2_ref[...] = out.astype(BF16)


def ple_mix(x, xb, pb, w_gate, b_gate, w_proj, layer, *, final=False, bn=1024, bm=MM_ROWS):
    m, d = x.shape
    kp = pb.shape[1]
    full = pl.BlockSpec((bm, bn), lambda j, i: (i, j))
    if final:
        assert m == TOK_PAD and bm == MM_ROWS and 0 <= SEQ - (TOK_PAD - MM_ROWS) <= MM_ROWS - DEC_BATCH
        out_specs = [full, pl.BlockSpec((DEC_BATCH, bn), lambda j, i: (0, j))]
        out_shape = [jax.ShapeDtypeStruct((SEQ, d), F32), jax.ShapeDtypeStruct((DEC_BATCH, d), F32)]
    else:
        out_specs = [full, full]
        out_shape = [jax.ShapeDtypeStruct((m, d), F32), jax.ShapeDtypeStruct((m, d), BF16)]
    return pl.pallas_call(
        functools.partial(_ple_kernel, final=final),
        grid=(d // bn, m // bm),
        in_specs=[pl.BlockSpec((bm, d), lambda j, i: (i, 0)),
                  pl.BlockSpec((bm, kp), lambda j, i: (i, 0)),
                  pl.BlockSpec((None, d, bn), lambda j, i: (layer, 0, j)),
                  pl.BlockSpec((None, kp, bn), lambda j, i: (layer, 0, j)),
                  pl.BlockSpec((None, 1, bn), lambda j, i: (layer, 0, j)),
                  full],
        out_specs=out_specs,
        out_shape=out_shape,
        scratch_shapes=[pltpu.VMEM((d, bn), BF16), pltpu.VMEM((kp, bn), BF16)],
        compiler_params=_params("arbitrary", "arbitrary"),
        name="ple_mix",
    )(xb, pb, w_gate, w_proj, b_gate.reshape(b_gate.shape[0], 1, d), x)


def _layernorm_rows(v, g, b):
    mu = jnp.mean(v, axis=-1, keepdims=True)
    c = v - mu
    var = jnp.mean(c * c, axis=-1, keepdims=True)
    return c * lax.rsqrt(var + LN_EPS) * g + b


def _add_ln_kernel(x_ref, xt_ref, m_ref, g_ref, b_ref, o_ref, ob_ref, *, n_main):
    x = x_ref[...]
    if n_main is not None:
        x = jnp.where(pl.program_id(0) < n_main, x, xt_ref[...])
    out = _layernorm_rows(DEEPNORM_ALPHA * x + m_ref[...], g_ref[...], b_ref[...])
    o_ref[...] = out
    ob_ref[...] = out.astype(BF16)


def add_ln(x, mix, g, b, x_tail=None):
    m, d = mix.shape
    row = pl.BlockSpec((ROW_BLOCK, d), lambda i: (i, 0))
    vec = pl.BlockSpec((1, d), lambda i: (0, 0))
    if x_tail is None:
        n_main, x_spec, x_tail = None, row, x
    else:
        n_main = x.shape[0] // ROW_BLOCK
        assert x_tail.shape[0] == ROW_BLOCK and n_main + 1 == m // ROW_BLOCK
        x_spec = pl.BlockSpec((ROW_BLOCK, d), lambda i: (jnp.minimum(i, n_main - 1), 0))
    return pl.pallas_call(
        functools.partial(_add_ln_kernel, n_main=n_main),
        grid=(m // ROW_BLOCK,),
        in_specs=[x_spec, pl.BlockSpec((ROW_BLOCK, d), lambda i: (0, 0)), row, vec, vec],
        out_specs=[row, row],
        out_shape=[jax.ShapeDtypeStruct((m, d), F32), jax.ShapeDtypeStruct((m, d), BF16)],
        compiler_params=_params("arbitrary"),
        name="add_ln",
    )(x, x_tail, mix, g.reshape(1, d), b.reshape(1, d))


def _gate_norm_kernel(y_ref, z_ref, w_ref, o_ref):
    g = y_ref[...] * _silu(z_ref[...])
    ms = jnp.mean(g * g, axis=-1, keepdims=True)
    o_ref[...] = (g * lax.rsqrt(ms + RMS_EPS) * w_ref[...]).astype(o_ref.dtype)


def gate_norm(y, zx, norm_w):
    m, di = y.shape
    row = pl.BlockSpec((ROW_BLOCK, di), lambda i: (i, 0))
    return pl.pallas_call(
        _gate_norm_kernel,
        grid=(m // ROW_BLOCK,),
        in_specs=[row, row, pl.BlockSpec((1, di), lambda i: (0, 0))],
        out_specs=row,
        out_shape=jax.ShapeDtypeStruct((m, di), BF16),
        compiler_params=_params("arbitrary"),
        name="gate_norm",
    )(y, zx, norm_w.reshape(1, di))


CONV_COLS = 2048


def _causal_conv_silu(x, halo, w, b):
    acc = b + x * w[SSM_CONV_W - 1:SSM_CONV_W, :]
    rows8 = lax.broadcasted_iota(I32, halo.shape, 0)
    for s in range(1, SSM_CONV_W):
        xr = pltpu.roll(x, s, 0)
        head = jnp.where(rows8 < s, pltpu.roll(halo, s, 0), xr[:8, :])
        shifted = jnp.concatenate([head, xr[8:, :]], axis=0)
        acc = acc + shifted * w[SSM_CONV_W - 1 - s:SSM_CONV_W - s, :]
    return _silu(acc)


def _conv_sample_kernel(st_ref, x_ref, w_ref, b_ref, o_ref):
    w = w_ref[...]
    acc = b_ref[...] + x_ref[...] * w[SSM_CONV_W - 1:SSM_CONV_W, :]
    for j in range(SSM_CONV_W - 1):
        acc = acc + st_ref[j] * w[j:j + 1, :]
    o_ref[...] = jnp.zeros(o_ref.shape, F32)
    o_ref[:DEC_BATCH, :] = _silu(acc)


def conv_sample(st, x_raw, conv_w, conv_b):
    return pl.pallas_call(
        _conv_sample_kernel,
        grid=(SSM_CONV_DIM // CONV_COLS,),
        in_specs=[pl.BlockSpec((SSM_CONV_W - 1, DEC_BATCH, CONV_COLS), lambda k: (0, 0, k)),
                  pl.BlockSpec((DEC_BATCH, CONV_COLS), lambda k: (0, k)),
                  pl.BlockSpec((SSM_CONV_W, CONV_COLS), lambda k: (0, k)),
                  pl.BlockSpec((1, CONV_COLS), lambda k: (0, k))],
        out_specs=pl.BlockSpec((TOK_PAD - SEQ, CONV_COLS), lambda k: (0, k)),
        out_shape=jax.ShapeDtypeStruct((TOK_PAD - SEQ, SSM_CONV_DIM), F32),
        compiler_params=_params("arbitrary"),
        name="conv_sample",
    )(st, x_raw, conv_w, conv_b.reshape(1, SSM_CONV_DIM))


SSD_GROUP_COLS = SSM_HEADS_PER_GROUP * SSM_HEAD_DIM
SSD_PAIR_ROWS = 2 * SSM_HEAD_DIM


N_CONV_BLOCKS = SSM_CONV_DIM // CONV_COLS


def _ssm_prompt_kernel(*refs):
    g_ref = refs[-4]

    @pl.when(pl.program_id(0) < SEQ // SSM_CHUNK)
    def _():
        _ssm_prompt_chunk(*refs)

    @pl.when(pl.program_id(0) >= SEQ // SSM_CHUNK)
    def _():
        g_ref[...] = jnp.zeros(g_ref.shape, g_ref.dtype)


def _ssm_prompt_chunk(*refs):
    nb = N_CONV_BLOCKS
    raw, halo = refs[:nb], refs[nb:2 * nb]
    (cw_ref, cb_ref, z_ref, dtc_ref, dtr_ref, bc_ref, br_ref, ac_ref, ar_ref, d_ref, nw_ref,
     g_ref, h_ref, xa_ref, y_ref) = refs[2 * nb:]
    first = pl.program_id(0) == 0
    step = 2 * LANES
    for k in range(nb):
        for c in range(0, CONV_COLS, step):
            src = slice(c, c + step)
            dst = slice(k * CONV_COLS + c, k * CONV_COLS + c + step)
            xa_ref[:, dst] = _causal_conv_silu(raw[k][:, src], jnp.where(first, 0.0, halo[k][:, src]),
                                               cw_ref[:, dst], cb_ref[:, dst])
    b0 = SSM_D_INNER
    c0 = SSM_D_INNER + SSM_N_GROUPS * SSM_D_STATE
    q = SSM_CHUNK
    dt_c = _softplus(dtc_ref[...] + bc_ref[...])
    dt_r = _softplus(dtr_ref[...] + br_ref[...])
    da_c = dt_c * -jnp.exp(ac_ref[...])
    da_r = dt_r * -jnp.exp(ar_ref[...])
    ti = lax.broadcasted_iota(I32, (q, q), 0)
    si = lax.broadcasted_iota(I32, (q, q), 1)
    causal = ti >= si
    tril = causal.astype(F32)
    triu = (ti <= si).astype(F32)
    acs_c = _dot_f32(tril, da_c, ((1,), (0,)))
    acs_r = _dot_f32(da_r, triu, ((1,), (0,)))
    last = acs_c[q - 1:q, :]

    @pl.when(pl.program_id(0) == 0)
    def _():
        h_ref[...] = jnp.zeros(h_ref.shape, F32)

    lane = lax.broadcasted_iota(I32, (q, LANES), 1)
    lo = lane < SSM_HEAD_DIM
    sub = lax.broadcasted_iota(I32, (SSD_PAIR_ROWS, 1), 0)
    for g in range(SSM_N_GROUPS):
        bm = xa_ref[:, b0 + g * SSM_D_STATE:b0 + (g + 1) * SSM_D_STATE].astype(BF16)
        cm = xa_ref[:, c0 + g * SSM_D_STATE:c0 + (g + 1) * SSM_D_STATE].astype(BF16)
        cb = lax.dot_general(cm, bm, (((1,), (1,)), ((), ())), preferred_element_type=F32)
        for pair in range(SSM_HEADS_PER_GROUP // 2):
            r1 = g * SSM_HEADS_PER_GROUP + 2 * pair
            r2 = r1 + 1
            cols = slice(r1 * SSM_HEAD_DIM, (r2 + 1) * SSM_HEAD_DIM)
            x = xa_ref[:, cols]
            xdt = x * jnp.where(lo, dt_c[:, r1:r1 + 1], dt_c[:, r2:r2 + 1])
            ms = []
            for r in (r1, r2):
                seg = acs_c[:, r:r + 1] - acs_r[r:r + 1, :]
                ms.append((cb * jnp.exp(jnp.where(causal, seg, NEG_INF))).astype(BF16))
            y = (jnp.dot(ms[0], jnp.where(lo, xdt, 0.0).astype(BF16), preferred_element_type=F32)
                 + jnp.dot(ms[1], jnp.where(lo, 0.0, xdt).astype(BF16), preferred_element_type=F32))

            h_in = h_ref[cols, :]
            y_off = lax.dot_general(cm, h_in.astype(BF16), (((1,), (1,)), ((), ())),
                                    preferred_element_type=F32)
            y = y + y_off * jnp.exp(jnp.where(lo, acs_c[:, r1:r1 + 1], acs_c[:, r2:r2 + 1]))
            y_ref[:, cols] = y + d_ref[:, cols] * x

            to_end = jnp.exp(jnp.where(lo, last[:, r1:r1 + 1] - acs_c[:, r1:r1 + 1],
                                       last[:, r2:r2 + 1] - acs_c[:, r2:r2 + 1]))
            xw_t = (xdt * to_end).T.astype(BF16)
            st = jnp.dot(xw_t, bm, preferred_element_type=F32)
            decay = jnp.exp(jnp.where(sub < SSM_HEAD_DIM, last[:, r1:r1 + 1], last[:, r2:r2 + 1]))
            h_ref[cols, :] = h_in * decay + st

    gated = y_ref[...] * _silu(z_ref[...])
    ms = jnp.mean(gated * gated, axis=-1, keepdims=True)
    g_ref[...] = (gated * lax.rsqrt(ms + RMS_EPS) * nw_ref[...]).astype(g_ref.dtype)


def ssm_prompt(zx, dt_raw, conv_w, conv_b, dt_bias, a_log, d_lanes, norm_w):
    nc = SEQ // SSM_CHUNK
    nh = SSM_N_HEADS
    cb0 = SSM_D_INNER // CONV_COLS
    halo_rows = SSM_CHUNK // 8
    clamp = lambda c: jnp.minimum(c, nc - 1)
    const = lambda shape: pl.BlockSpec(shape, lambda c: (0,) * len(shape))
    raw_specs = [pl.BlockSpec((SSM_CHUNK, CONV_COLS), functools.partial(lambda k, c: (c, cb0 + k), k))
                 for k in range(N_CONV_BLOCKS)]
    halo_specs = [pl.BlockSpec((8, CONV_COLS),
                               functools.partial(lambda k, c: (jnp.maximum(c * halo_rows - 1, 0), cb0 + k), k))
                  for k in range(N_CONV_BLOCKS)]
    return pl.pallas_call(
        _ssm_prompt_kernel,
        grid=(TOK_PAD // SSM_CHUNK,),
        in_specs=raw_specs + halo_specs + [
            const((SSM_CONV_W, SSM_CONV_DIM)), const((1, SSM_CONV_DIM)),
            pl.BlockSpec((SSM_CHUNK, SSM_D_INNER), lambda c: (c, 0)),
            pl.BlockSpec((SSM_CHUNK, nh), lambda c: (clamp(c), 0)),
            pl.BlockSpec((nh, SSM_CHUNK), lambda c: (0, clamp(c))),
            const((1, nh)), const((nh, 1)), const((1, nh)), const((nh, 1)),
            const((1, SSM_D_INNER)), const((1, SSM_D_INNER))],
        out_specs=[pl.BlockSpec((SSM_CHUNK, SSM_D_INNER), lambda c: (c, 0)),
                   const((nh * SSM_HEAD_DIM, SSM_D_STATE))],
        out_shape=[jax.ShapeDtypeStruct((TOK_PAD, SSM_D_INNER), BF16),
                   jax.ShapeDtypeStruct((nh * SSM_HEAD_DIM, SSM_D_STATE), F32)],
        scratch_shapes=[pltpu.VMEM((SSM_CHUNK, SSM_CONV_DIM), F32), pltpu.VMEM((SSM_CHUNK, SSM_D_INNER), F32)],
        compiler_params=_params("arbitrary"),
        name="ssm_prompt",
    )(*([zx] * (2 * N_CONV_BLOCKS)), conv_w, conv_b.reshape(1, SSM_CONV_DIM), zx, dt_raw, dt_raw.T,
      dt_bias.reshape(1, nh), dt_bias.reshape(nh, 1), a_log.reshape(1, nh), a_log.reshape(nh, 1),
      d_lanes, norm_w.reshape(1, SSM_D_INNER))


def _mxu_round(x):
    return x.astype(BF16).astype(F32)


def _ssd_sample_kernel(h0_ref, xt_ref, xl_ref, b_ref, c_ref, dt_ref, dtl_ref, bias_ref, biasl_ref,
                       alog_ref, alogl_ref, d_ref, h_ref, y_ref):
    dt = _softplus(dt_ref[...] + bias_ref[...])
    dec = jnp.exp(dt * -jnp.exp(alog_ref[...]))
    dt_l = _softplus(dtl_ref[...] + biasl_ref[...])
    dec_l = jnp.exp(dt_l * -jnp.exp(alogl_ref[...]))
    xdt_t = _mxu_round(xt_ref[...] * dt)
    xdt_l = _mxu_round(xl_ref[...] * dt_l)
    bm = _mxu_round(b_ref[...])
    cm = _mxu_round(c_ref[...])
    for grp in range(SSM_N_GROUPS):
        b_row = bm[grp:grp + 1, :]
        c_row = cm[grp:grp + 1, :]
        cb = _mxu_round(jnp.sum(c_row * b_row, axis=1, keepdims=True))
        old = []
        for r in range(SSM_HEADS_PER_GROUP):
            h = grp * SSM_HEADS_PER_GROUP + r
            rows = slice(h * SSM_HEAD_DIM, (h + 1) * SSM_HEAD_DIM)
            h0 = h0_ref[rows, :]
            h_ref[rows, :] = h0 * dec[:, h:h + 1] + xdt_t[:, h:h + 1] * b_row
            old.append(h0)
        h0_grp = jnp.concatenate(old, axis=0).astype(BF16)
        c_rows = jnp.broadcast_to(c_row, (8, SSM_D_STATE)).astype(BF16)
        ch0 = lax.dot_general(c_rows, h0_grp, (((1,), (1,)), ((), ())), preferred_element_type=F32)[0:1, :]
        cols = slice(grp * SSD_GROUP_COLS, (grp + 1) * SSD_GROUP_COLS)
        y_ref[:, cols] = (cb * xdt_l[:, cols] + ch0 * dec_l[:, cols]) + d_ref[:, cols] * xl_ref[:, cols]


def ssd_sample(h0, x, bm, cm, dt_raw, dt_bias, a_log, d_lanes):
    nb = h0.shape[0]
    nh, p = SSM_N_HEADS, SSM_HEAD_DIM
    hp = nh * p
    lanes = lambda v: jnp.repeat(v, p, axis=-1)
    x_t = jnp.transpose(x.reshape(nb, nh, p), (0, 2, 1))
    per_b = lambda *shape: pl.BlockSpec((None,) + shape, lambda b: (b,) + (0,) * len(shape))
    vec = lambda n: pl.BlockSpec((1, n), lambda b: (0, 0))
    return pl.pallas_call(
        _ssd_sample_kernel,
        grid=(nb,),
        in_specs=[per_b(hp, SSM_D_STATE), per_b(p, nh), per_b(1, hp),
                  per_b(SSM_N_GROUPS, SSM_D_STATE), per_b(SSM_N_GROUPS, SSM_D_STATE),
                  per_b(1, nh), per_b(1, hp), vec(nh), vec(hp), vec(nh), vec(hp), vec(hp)],
        out_specs=[per_b(hp, SSM_D_STATE), per_b(1, hp)],
        out_shape=[jax.ShapeDtypeStruct((nb, hp, SSM_D_STATE), F32),
                   jax.ShapeDtypeStruct((nb, 1, hp), F32)],
        compiler_params=_params("arbitrary"),
        name="ssd_sample",
    )(h0, x_t, x.reshape(nb, 1, hp), bm, cm, dt_raw.reshape(nb, 1, nh), lanes(dt_raw).reshape(nb, 1, hp),
      dt_bias.reshape(1, nh), lanes(dt_bias).reshape(1, hp), a_log.reshape(1, nh), lanes(a_log).reshape(1, hp),
      d_lanes)


KV_COLS = ATTN_N_KV_HEADS * ATTN_HEAD_DIM
Q_LANES = ATTN_GROUP * MOBA_BLOCK


def _kmeans_kernel(k_ref, o_ref):
    o_ref[...] = jnp.mean(k_ref[...], axis=0, keepdims=True).reshape(o_ref.shape)


def block_means(k):
    nb = SEQ // MOBA_BLOCK
    return pl.pallas_call(
        _kmeans_kernel,
        grid=(nb,),
        in_specs=[pl.BlockSpec((MOBA_BLOCK, KV_COLS), lambda i: (i, 0))],
        out_specs=pl.BlockSpec((None, 1, KV_COLS), lambda i: (i, 0, 0)),
        out_shape=jax.ShapeDtypeStruct((nb, 1, KV_COLS), F32),
        compiler_params=_params("arbitrary"),
        name="block_means",
    )(k)


def _top3_mask_rows(gate):
    n = gate.shape[0]
    rows = lax.broadcasted_iota(I32, gate.shape, 0)
    sel = jnp.zeros(gate.shape, jnp.bool_)
    for _ in range(MOBA_TOPK):
        m = jnp.max(gate, axis=0, keepdims=True)
        first = jnp.min(jnp.where(gate == m, rows, n), axis=0, keepdims=True)
        hit = jnp.logical_and(rows == first, m > NEG_INF)
        sel = jnp.logical_or(sel, hit)
        gate = jnp.where(rows == first, NEG_INF, gate)
    return sel


MASKED_SCORE = -1e30
LOG2_E = 1.4426950408889634


NO_BLOCK_LANE = LANES - 1


def _moba_prompt_kernel(qi_ref, fl_ref, ida_ref, idb_ref, q_ref, ka_ref, kb_ref, va_ref, vb_ref, mean_ref, o_ref,
                        qaug_ref, s_ref, m_ref, l_ref, acc_ref):
    t = pl.program_id(0)
    i = qi_ref[t]
    is_first = (fl_ref[t] & 1) == 1
    is_last = (fl_ref[t] & 2) == 2
    nb = mean_ref.shape[0]
    q_scale = ATTN_HEAD_DIM ** -0.5 * LOG2_E
    hd = ATTN_HEAD_DIM

    @pl.when(i == nb)
    def _():
        o_ref[...] = jnp.zeros(o_ref.shape, o_ref.dtype)

    @pl.when(jnp.logical_and(i < nb, is_first))
    def _():
        blk = lax.broadcasted_iota(I32, (nb, Q_LANES), 0)
        pad_row = lax.broadcasted_iota(I32, (LANES - nb, MOBA_BLOCK), 0)
        pad = jnp.where(pad_row == NO_BLOCK_LANE - nb, MASKED_SCORE, 0.0)
        for g in range(ATTN_N_KV_HEADS):
            q_g = jnp.concatenate([q_ref[:, (ATTN_GROUP * g + r) * hd:(ATTN_GROUP * g + r + 1) * hd]
                                   for r in range(ATTN_GROUP)], axis=0)
            gate = lax.dot_general(mean_ref[:, g * hd:(g + 1) * hd].astype(BF16), q_g.astype(BF16),
                                   (((1,), (1,)), ((), ())), preferred_element_type=F32)
            sel = _top3_mask_rows(jnp.where(blk < i, gate, NEG_INF))
            bias = jnp.where(jnp.logical_or(sel, blk == i), 0.0, MASKED_SCORE)
            bias_q = jnp.concatenate(
                [jnp.concatenate([bias[:, r * MOBA_BLOCK:(r + 1) * MOBA_BLOCK], pad], axis=0).T
                 for r in range(ATTN_GROUP)], axis=0)
            qaug_ref[g] = jnp.concatenate([(q_g * q_scale).astype(BF16), bias_q.astype(BF16)], axis=1)
        m_ref[...] = jnp.full(m_ref.shape, NEG_INF, F32)
        l_ref[...] = jnp.zeros(l_ref.shape, F32)
        acc_ref[...] = jnp.zeros(acc_ref.shape, F32)

    def attend(own):
        blk_lane = lax.broadcasted_iota(I32, (MOBA_BLOCK, LANES), 1)
        halves = [(ka_ref, jnp.where(blk_lane == ida_ref[t], 1.0, 0.0).astype(BF16)),
                  (kb_ref, jnp.where(blk_lane == idb_ref[t], 1.0, 0.0).astype(BF16))]
        for g in range(ATTN_N_KV_HEADS):
            k_aug = jnp.concatenate(
                [jnp.concatenate([k_ref[:, g * hd:(g + 1) * hd].astype(BF16), onehot], axis=1)
                 for k_ref, onehot in halves], axis=0)
            s_ref[g] = lax.dot_general(k_aug, qaug_ref[g], (((1,), (1,)), ((), ())), preferred_element_type=F32)
        if own:
            key = lax.broadcasted_iota(I32, (2 * MOBA_BLOCK, Q_LANES), 0)
            qry = lax.broadcasted_iota(I32, (2 * MOBA_BLOCK, Q_LANES), 1) & (MOBA_BLOCK - 1)
            visible = jnp.logical_or(key <= qry, key >= MOBA_BLOCK)
        for g in range(ATTN_N_KV_HEADS):
            s = s_ref[g]
            if own:
                s = jnp.where(visible, s, NEG_INF)
            m_old = m_ref[g]
            m_new = jnp.maximum(m_old, jnp.max(s, axis=0, keepdims=True))
            alpha = jnp.exp2(m_old - m_new)
            p = jnp.exp2(s - m_new)
            l_ref[g] = alpha * l_ref[g] + jnp.sum(p, axis=0, keepdims=True)
            m_ref[g] = m_new
            v_t = jnp.concatenate([va_ref[:, g * hd:(g + 1) * hd], vb_ref[:, g * hd:(g + 1) * hd]],
                                  axis=0).T.astype(BF16)
            acc_ref[g] = alpha * acc_ref[g] + jnp.dot(v_t, p.astype(BF16), preferred_element_type=F32)

    @pl.when(jnp.logical_and(i < nb, is_first))
    def _():
        attend(True)

    @pl.when(jnp.logical_and(i < nb, jnp.logical_not(is_first)))
    def _():
        attend(False)

    @pl.when(jnp.logical_and(i < nb, is_last))
    def _():
        for g in range(ATTN_N_KV_HEADS):
            out = (acc_ref[g] / l_ref[g]).T
            for r in range(ATTN_GROUP):
                h = ATTN_GROUP * g + r
                o_ref[:, h * hd:(h + 1) * hd] = out[r * MOBA_BLOCK:(r + 1) * MOBA_BLOCK, :].astype(o_ref.dtype)


def moba_prompt(q, k, v, means):
    nb = SEQ // MOBA_BLOCK
    assert nb < NO_BLOCK_LANE
    steps = []
    for i in range(nb):
        pairs = [(i, i - 1)] + [(a, a - 1) for a in range(i - 2, -1, -2)]
        for n, (a, b) in enumerate(pairs):
            flags = (1 if n == 0 else 0) | (2 if n == len(pairs) - 1 else 0)
            steps.append((i, flags, a, b if b >= 0 else NO_BLOCK_LANE))
    steps.append((nb, 0, NO_BLOCK_LANE, NO_BLOCK_LANE))
    tables = [jnp.asarray(np.array([s[c] for s in steps], np.int32)) for c in range(4)]
    blk = lambda ids, t: jnp.where(ids[t] < nb, ids[t], 0)
    ka_spec = pl.BlockSpec((MOBA_BLOCK, KV_COLS), lambda t, qi, fl, ida, idb: (blk(ida, t), 0))
    kb_spec = pl.BlockSpec((MOBA_BLOCK, KV_COLS), lambda t, qi, fl, ida, idb: (blk(idb, t), 0))
    row_spec = pl.BlockSpec((MOBA_BLOCK, D_MODEL), lambda t, qi, *_: (qi[t], 0))
    return pl.pallas_call(
        _moba_prompt_kernel,
        grid_spec=pltpu.PrefetchScalarGridSpec(
            num_scalar_prefetch=4,
            grid=(len(steps),),
            in_specs=[row_spec, ka_spec, kb_spec, ka_spec, kb_spec,
                      pl.BlockSpec((nb, KV_COLS), lambda t, *_: (0, 0))],
            out_specs=row_spec,
            scratch_shapes=[pltpu.VMEM((ATTN_N_KV_HEADS, Q_LANES, ATTN_HEAD_DIM + LANES), BF16),
                            pltpu.VMEM((ATTN_N_KV_HEADS, 2 * MOBA_BLOCK, Q_LANES), F32),
                            pltpu.VMEM((ATTN_N_KV_HEADS, 1, Q_LANES), F32),
                            pltpu.VMEM((ATTN_N_KV_HEADS, 1, Q_LANES), F32),
                            pltpu.VMEM((ATTN_N_KV_HEADS, ATTN_HEAD_DIM, Q_LANES), F32)]),
        out_shape=jax.ShapeDtypeStruct((TOK_PAD, D_MODEL), BF16),
        compiler_params=_params("arbitrary"),
        name="moba_prompt",
    )(*tables, q, k, k, v, v, means)


PAGE_ROWS = PAGE_SIZE * ATTN_N_KV_HEADS
SEL_KEYS = MOBA_TOPK * MOBA_BLOCK


def _page_copy(pool_hbm, page, dst, sem):
    return pltpu.make_async_copy(pool_hbm.at[page], dst, sem)


def _sample_select_kernel(pt_ref, q_ref, kc_hbm, o_ref, kbuf, means_ref, sems):
    b = pl.program_id(0)
    slot = b % 2

    def fetch(seq, slot_, start):
        for p in range(N_PAGES):
            cp = _page_copy(kc_hbm, pt_ref[seq * N_PAGES + p], kbuf.at[slot_, p], sems.at[slot_])
            cp.start() if start else cp.wait()

    @pl.when(b == 0)
    def _():
        fetch(b, slot, True)

    @pl.when(b + 1 < pl.num_programs(0))
    def _():
        fetch(b + 1, 1 - slot, True)

    fetch(b, slot, False)
    fold = 8 // ATTN_N_KV_HEADS
    for n in range(N_PAST_BLOCKS):
        s = jnp.zeros((8, ATTN_HEAD_DIM), F32)
        for p in range(PAGES_PER_BLOCK):
            s = s + jnp.sum(kbuf[slot, n * PAGES_PER_BLOCK + p].reshape(PAGE_ROWS // 8, 8, ATTN_HEAD_DIM), axis=0)
        tot = s[:ATTN_N_KV_HEADS, :]
        for f in range(1, fold):
            tot = tot + s[f * ATTN_N_KV_HEADS:(f + 1) * ATTN_N_KV_HEADS, :]
        means_ref[n * ATTN_N_KV_HEADS:(n + 1) * ATTN_N_KV_HEADS, :] = tot * (1.0 / MOBA_BLOCK)
    gate = lax.dot_general(q_ref[...].astype(BF16), means_ref[...].astype(BF16), (((1,), (1,)), ((), ())),
                           preferred_element_type=F32)
    shape = gate.shape
    lane = lax.broadcasted_iota(I32, shape, 1)
    head = lax.broadcasted_iota(I32, shape, 0)
    gate = jnp.where((lane & (ATTN_N_KV_HEADS - 1)) == (head >> 2), gate, NEG_INF)
    out = jnp.zeros(shape, I32)
    for t in range(MOBA_TOPK):
        m = jnp.max(gate, axis=1, keepdims=True)
        first = jnp.min(jnp.where(gate == m, lane, shape[1]), axis=1, keepdims=True)
        out = jnp.where(lane == t, first >> 2, out)
        gate = jnp.where(lane == first, NEG_INF, gate)
    o_ref[...] = out


def sample_select(page_table, q_s, kc_pages):
    nb = q_s.shape[0]
    assert N_PAST_BLOCKS * ATTN_N_KV_HEADS == LANES
    return pl.pallas_call(
        _sample_select_kernel,
        grid_spec=pltpu.PrefetchScalarGridSpec(
            num_scalar_prefetch=1,
            grid=(nb,),
            in_specs=[pl.BlockSpec((None, ATTN_N_HEADS, ATTN_HEAD_DIM), lambda b, pt: (b, 0, 0)),
                      pl.BlockSpec(memory_space=pl.ANY)],
            out_specs=pl.BlockSpec((None, ATTN_N_HEADS, LANES), lambda b, pt: (b, 0, 0)),
            scratch_shapes=[pltpu.VMEM((2, N_PAGES, PAGE_ROWS, ATTN_HEAD_DIM), F32),
                            pltpu.VMEM((N_PAST_BLOCKS * ATTN_N_KV_HEADS, ATTN_HEAD_DIM), F32),
                            pltpu.SemaphoreType.DMA((2,))]),
        out_shape=jax.ShapeDtypeStruct((nb, ATTN_N_HEADS, LANES), I32),
        compiler_params=_params("arbitrary"),
        name="sample_select",
    )(page_table.reshape(-1), q_s, kc_pages)


def _head_rows_copy(pool_hbm, page, g, dst, sem):
    return pltpu.make_async_copy(pool_hbm.at[page, :, g, :], dst, sem)


def _sample_attend_kernel(pt_ref, sel_ref, q_ref, kn_ref, vn_ref, kc_hbm, vc_hbm, o_ref, kbuf, vbuf, sems):
    b = pl.program_id(0)
    slot = b % 2
    scale = ATTN_HEAD_DIM ** -0.5

    def fetch(seq, slot_, start):
        for h in range(ATTN_N_HEADS):
            g = h // ATTN_GROUP
            for t in range(MOBA_TOPK):
                blk = sel_ref[(seq * ATTN_N_HEADS + h) * MOBA_TOPK + t]
                for half in range(PAGES_PER_BLOCK):
                    page = pt_ref[seq * N_PAGES + blk * PAGES_PER_BLOCK + half]
                    rows = pl.ds(t * MOBA_BLOCK + half * PAGE_SIZE, PAGE_SIZE)
                    for cp in (_head_rows_copy(kc_hbm, page, g, kbuf.at[slot_, h, rows, :], sems.at[0, slot_]),
                               _head_rows_copy(vc_hbm, page, g, vbuf.at[slot_, h, rows, :], sems.at[1, slot_])):
                        cp.start() if start else cp.wait()

    @pl.when(b == 0)
    def _():
        fetch(b, slot, True)

    @pl.when(b + 1 < pl.num_programs(0))
    def _():
        fetch(b + 1, 1 - slot, True)

    fetch(b, slot, False)

    for h in range(ATTN_N_HEADS):
        g = h // ATTN_GROUP
        qh = q_ref[h:h + 1, :]
        q8 = jnp.broadcast_to(qh, (8, ATTN_HEAD_DIM)).astype(BF16)
        s = lax.dot_general(kbuf[slot, h].astype(BF16), q8, (((1,), (1,)), ((), ())),
                            preferred_element_type=F32)[:, 0:1] * scale
        s_own = jnp.sum(_mxu_round(qh) * _mxu_round(kn_ref[g:g + 1, :]), axis=1, keepdims=True) * scale
        m = jnp.maximum(jnp.max(s, axis=0, keepdims=True), s_own)
        p = jnp.exp(s - m)
        p_own = jnp.exp(s_own - m)
        denom = jnp.sum(p, axis=0, keepdims=True) + p_own
        o_sel = jnp.sum(_mxu_round(p / denom) * _mxu_round(vbuf[slot, h]), axis=0, keepdims=True)
        o_ref[h:h + 1, :] = o_sel + _mxu_round(p_own / denom) * _mxu_round(vn_ref[g:g + 1, :])


def sample_attend(page_table, sel, q_s, k_new, v_new, cache_k, cache_v):
    nb = q_s.shape[0]
    per_b = lambda rows: pl.BlockSpec((None, rows, ATTN_HEAD_DIM), lambda b, pt, sl: (b, 0, 0))
    any_spec = pl.BlockSpec(memory_space=pl.ANY)
    return pl.pallas_call(
        _sample_attend_kernel,
        grid_spec=pltpu.PrefetchScalarGridSpec(
            num_scalar_prefetch=2,
            grid=(nb,),
            in_specs=[per_b(ATTN_N_HEADS), per_b(ATTN_N_KV_HEADS), per_b(ATTN_N_KV_HEADS), any_spec, any_spec],
            out_specs=per_b(ATTN_N_HEADS),
            scratch_shapes=[pltpu.VMEM((2, ATTN_N_HEADS, SEL_KEYS, ATTN_HEAD_DIM), F32),
                            pltpu.VMEM((2, ATTN_N_HEADS, SEL_KEYS, ATTN_HEAD_DIM), F32),
                            pltpu.SemaphoreType.DMA((2, 2))]),
        out_shape=jax.ShapeDtypeStruct((nb, ATTN_N_HEADS, ATTN_HEAD_DIM), F32),
        compiler_params=_params("arbitrary"),
        name="sample_attend",
    )(page_table.reshape(-1), sel.reshape(-1), q_s, k_new, v_new, cache_k, cache_v)


def _router_kernel(x_ref, w_ref, b_ref, idx_ref, gate_ref):
    logits = jnp.dot(x_ref[...], w_ref[...].astype(BF16), preferred_element_type=F32) + b_ref[...]
    lane = lax.broadcasted_iota(I32, logits.shape, 1)
    logits = jnp.where(lane < N_EXPERTS, logits, NEG_INF)
    m1 = jnp.max(logits, axis=1, keepdims=True)
    i1 = jnp.min(jnp.where(logits == m1, lane, LANES), axis=1, keepdims=True)
    rest = jnp.where(lane == i1, NEG_INF, logits)
    m2 = jnp.max(rest, axis=1, keepdims=True)
    i2 = jnp.min(jnp.where(rest == m2, lane, LANES), axis=1, keepdims=True)
    e2 = jnp.exp(m2 - m1)
    g1 = 1.0 / (1.0 + e2)
    idx_ref[...] = jnp.where(lane == 0, i1, jnp.where(lane == 1, i2, 0))
    gate_ref[...] = jnp.where(lane == 0, g1, jnp.where(lane == 1, e2 * g1, 0.0))


def router(x, w_router, b_router):
    m, d = x.shape
    wp = jnp.zeros((d, LANES), F32).at[:, :N_EXPERTS].set(w_router)
    bp = jnp.zeros((1, LANES), F32).at[0, :N_EXPERTS].set(b_router)
    row = pl.BlockSpec((ROW_BLOCK, d), lambda i: (i, 0))
    out = pl.BlockSpec((ROW_BLOCK, LANES), lambda i: (i, 0))
    return pl.pallas_call(
        _router_kernel,
        grid=(m // ROW_BLOCK,),
        in_specs=[row, pl.BlockSpec((d, LANES), lambda i: (0, 0)), pl.BlockSpec((1, LANES), lambda i: (0, 0))],
        out_specs=[out, out],
        out_shape=[jax.ShapeDtypeStruct((m, LANES), I32), jax.ShapeDtypeStruct((m, LANES), F32)],
        compiler_params=_params("arbitrary"),
        name="router",
    )(x, wp, bp)


def _row_copy(src_hbm, row, dst, slot, sem):
    return pltpu.make_async_copy(src_hbm.at[pl.ds(row, 1), :], dst.at[pl.ds(slot, 1), :], sem)


GATHER_UNROLL = 8


def _moe_gather_kernel(tok_ref, nxt_ref, x_hbm, o_ref, buf, sems):
    i = pl.program_id(0)
    slot = i % 2

    def issue(idx_ref, slot_):
        def body(r2, carry):
            for q in range(2):
                r = 2 * r2 + q
                _row_copy(x_hbm, idx_ref[0, 0, r], buf.at[slot_], r, sems.at[slot_]).start(priority=q)
            return carry
        lax.fori_loop(0, MOE_ROWS // 2, body, 0, unroll=GATHER_UNROLL // 2)

    @pl.when(i == 0)
    def _():
        issue(tok_ref, slot)

    @pl.when(i + 1 < pl.num_programs(0))
    def _():
        issue(nxt_ref, 1 - slot)

    def wait(r, carry):
        _row_copy(x_hbm, 0, buf.at[slot], r, sems.at[slot]).wait()
        return carry

    lax.fori_loop(0, MOE_ROWS, wait, 0, unroll=GATHER_UNROLL)
    o_ref[...] = buf[slot].astype(o_ref.dtype)


def moe_gather(x, buf_tok):
    d = x.shape[1]
    tok = buf_tok.reshape(MOE_BLOCKS, 1, MOE_ROWS)
    idx_spec = lambda step: pl.BlockSpec((1, 1, MOE_ROWS), lambda i: (jnp.minimum(i + step, MOE_BLOCKS - 1), 0, 0),
                                         memory_space=pltpu.SMEM)
    return pl.pallas_call(
        _moe_gather_kernel,
        grid=(MOE_BLOCKS,),
        in_specs=[idx_spec(0), idx_spec(1), pl.BlockSpec(memory_space=pl.ANY)],
        out_specs=pl.BlockSpec((MOE_ROWS, d), lambda i: (i, 0)),
        out_shape=jax.ShapeDtypeStruct((MOE_BUF, d), BF16),
        scratch_shapes=[pltpu.VMEM((2, MOE_ROWS, d), F32), pltpu.SemaphoreType.DMA((2,))],
        compiler_params=_params("arbitrary"),
        name="moe_gather",
    )(tok, tok, x)


W_CHUNKS = 8


def _moe_expert_kernel(first_ref, count_ref, used_ref, w_hbm, x_hbm, o_hbm,
                       wstage, wb_ref, xbuf, obuf, wsem, xsem, osem, *, n_w, nj, bn):
    e = pl.program_id(0)
    j = pl.program_id(1)
    n_e = pl.num_programs(0)
    step = e * nj + j
    slot_w = step % 2
    first = first_ref[e]
    n = count_ref[e]
    cols = pl.ds(pl.multiple_of(j * bn, bn), bn)
    kc = w_hbm.shape[1] // W_CHUNKS

    def rows(blk):
        return pl.ds(pl.multiple_of(blk * MOE_ROWS, MOE_ROWS), MOE_ROWS)

    def x_copy(k, slot):
        return pltpu.make_async_copy(x_hbm.at[rows(first + k), :], xbuf.at[slot], xsem.at[slot])

    def o_copy(blk, slot):
        return pltpu.make_async_copy(obuf.at[slot], o_hbm.at[rows(blk), cols], osem.at[slot])

    def w_copies(e_, j_, slot, c):
        krows = pl.ds(pl.multiple_of(c * kc, kc), kc)
        return [pltpu.make_async_copy(
            w_hbm.at[e_, krows, pl.ds(pl.multiple_of((j_ + m * nj) * bn, bn), bn)],
            wstage.at[slot, m, krows, :], wsem.at[slot]) for m in range(n_w)]

    last_j = j == nj - 1
    e_next = jnp.where(last_j, e + 1, e)
    j_next = jnp.where(last_j, 0, j + 1)
    has_next = step + 1 < n_e * nj

    def fetch_next(c):
        for cp in w_copies(e_next, j_next, 1 - slot_w, c):
            cp.start()

    @pl.when(step == 0)
    def _():
        for c in range(W_CHUNKS):
            for cp in w_copies(e, j, slot_w, c):
                cp.start()

    @pl.when(n > 0)
    def _():
        x_copy(0, 0).start()

    for c in range(W_CHUNKS):
        for cp in w_copies(e, j, slot_w, c):
            cp.wait()

    @pl.when(n > 0)
    def _():
        for m in range(n_w):
            wb_ref[m] = wstage[slot_w, m].astype(BF16)

    def body(k, carry):
        slot = k % 2
        x_copy(k, slot).wait()

        @pl.when(k + 1 < n)
        def _():
            x_copy(k + 1, 1 - slot).start()

        @pl.when(jnp.logical_and(has_next, k < W_CHUNKS))
        def _():
            fetch_next(k)

        @pl.when(k >= 2)
        def _():
            o_copy(first + k - 2, slot).wait()

        x = xbuf[slot]
        acc = jnp.dot(x, wb_ref[0], preferred_element_type=F32)
        if n_w == 2:
            acc = _silu(acc) * jnp.dot(x, wb_ref[1], preferred_element_type=F32)
        obuf[slot] = acc.astype(obuf.dtype)
        o_copy(first + k, slot).start()
        return carry

    lax.fori_loop(0, n, body, 0)

    @pl.when(has_next)
    def _():
        def rest(c, carry):
            fetch_next(c)
            return carry

        lax.fori_loop(jnp.minimum(n, W_CHUNKS), W_CHUNKS, rest, 0)

    @pl.when(n >= 2)
    def _():
        o_copy(first + n - 2, n % 2).wait()

    @pl.when(n >= 1)
    def _():
        o_copy(first + n - 1, (n - 1) % 2).wait()

    @pl.when(e == n_e - 1)
    def _():
        obuf[0] = jnp.zeros(obuf.shape[1:], obuf.dtype)

        def fill(blk, carry):
            o_copy(blk, 0).start()
            o_copy(blk, 0).wait()
            return carry

        lax.fori_loop(used_ref[0], MOE_BLOCKS, fill, 0)


def moe_expert_matmul(xs, w, plan, *, swiglu, bn, out_dtype):
    first, count, used = plan
    r, k = xs.shape
    n_w = 2 if swiglu else 1
    f = w.shape[2] // n_w
    nj = f // bn
    assert f % bn == 0 and k % (8 * W_CHUNKS) == 0
    hbm = pl.BlockSpec(memory_space=pl.ANY)
    return pl.pallas_call(
        functools.partial(_moe_expert_kernel, n_w=n_w, nj=nj, bn=bn),
        grid_spec=pltpu.PrefetchScalarGridSpec(
            num_scalar_prefetch=3,
            grid=(N_EXPERTS, nj),
            in_specs=[hbm, hbm],
            out_specs=hbm,
            scratch_shapes=[pltpu.VMEM((2, n_w, k, bn), F32), pltpu.VMEM((n_w, k, bn), BF16),
                            pltpu.VMEM((2, MOE_ROWS, k), xs.dtype), pltpu.VMEM((2, MOE_ROWS, bn), out_dtype),
                            pltpu.SemaphoreType.DMA((2,)), pltpu.SemaphoreType.DMA((2,)),
                            pltpu.SemaphoreType.DMA((2,))]),
        out_shape=jax.ShapeDtypeStruct((r, f), out_dtype),
        compiler_params=_params("arbitrary", "arbitrary"),
        name="moe_gu" if swiglu else "moe_down",
    )(first, count, used, w, xs)


def _moe_combine_ln_kernel(pos_ref, nxt_ref, y_hbm, x_ref, gate_ref, g_ref, b_ref, o_ref, ob_ref, buf, sems):
    i = pl.program_id(0)
    slot = i % 2

    def issue(idx_ref, slot_):
        def body(r, carry):
            for t in range(MOE_TOPK):
                _row_copy(y_hbm, idx_ref[0, 0, MOE_TOPK * r + t], buf.at[slot_, t], r,
                          sems.at[slot_]).start(priority=t % 2)
            return carry
        lax.fori_loop(0, ROW_BLOCK, body, 0, unroll=GATHER_UNROLL)

    @pl.when(i == 0)
    def _():
        issue(pos_ref, slot)

    @pl.when(i + 1 < pl.num_programs(0))
    def _():
        issue(nxt_ref, 1 - slot)

    def wait(r, carry):
        for t in range(MOE_TOPK):
            _row_copy(y_hbm, 0, buf.at[slot, t], r, sems.at[slot]).wait()
        return carry

    lax.fori_loop(0, ROW_BLOCK, wait, 0, unroll=GATHER_UNROLL)
    gates = gate_ref[...]
    ff = buf[slot, 0] * gates[:, 0:1]
    for t in range(1, MOE_TOPK):
        ff = ff + buf[slot, t] * gates[:, t:t + 1]
    out = _layernorm_rows(DEEPNORM_ALPHA * x_ref[...] + ff, g_ref[...], b_ref[...])
    o_ref[...] = out
    ob_ref[...] = out.astype(BF16)


def moe_combine_ln(x, ys, pos, gates, g, b):
    m, d = x.shape
    nblk = m // ROW_BLOCK
    row = pl.BlockSpec((ROW_BLOCK, d), lambda i: (i, 0))
    vec = pl.BlockSpec((1, d), lambda i: (0, 0))
    pos3 = pos.reshape(nblk, 1, MOE_TOPK * ROW_BLOCK)
    idx_spec = lambda step: pl.BlockSpec((1, 1, MOE_TOPK * ROW_BLOCK),
                                         lambda i: (jnp.minimum(i + step, nblk - 1), 0, 0), memory_space=pltpu.SMEM)
    return pl.pallas_call(
        _moe_combine_ln_kernel,
        grid=(nblk,),
        in_specs=[idx_spec(0), idx_spec(1), pl.BlockSpec(memory_space=pl.ANY),
                  row, pl.BlockSpec((ROW_BLOCK, LANES), lambda i: (i, 0)), vec, vec],
        out_specs=[row, row],
        out_shape=[jax.ShapeDtypeStruct((m, d), F32), jax.ShapeDtypeStruct((m, d), BF16)],
        scratch_shapes=[pltpu.VMEM((2, MOE_TOPK, ROW_BLOCK, d), F32), pltpu.SemaphoreType.DMA((2,))],
        compiler_params=_params("arbitrary"),
        name="moe_combine_ln",
    )(pos3, pos3, ys, x, gates, g.reshape(1, d), b.reshape(1, d))


def moe_plan(idx):
    e = idx[:N_TOK, :MOE_TOPK].reshape(N_ASSIGN)
    onehot = (e[:, None] == jnp.arange(N_EXPERTS, dtype=I32)[None, :]).astype(I32)
    seg = 64
    assert N_ASSIGN % seg == 0 and seg <= 256
    oh = onehot.astype(F32).reshape(N_ASSIGN // seg, seg, N_EXPERTS)
    tri = lambda n, k: jnp.tril(jnp.ones((n, n), F32), k)
    within = jnp.einsum("ts,rse->rte", tri(seg, -1), oh)
    offset = jnp.einsum("rq,qe->re", tri(N_ASSIGN // seg, -1), jnp.sum(oh, axis=1))
    before = (within + offset[:, None, :]).reshape(N_ASSIGN, N_EXPERTS).astype(I32)
    rank = jnp.sum(before * onehot, axis=1)
    counts = jnp.sum(onehot, axis=0)
    padded = (counts + MOE_ROWS - 1) // MOE_ROWS * MOE_ROWS
    pad_end = jnp.cumsum(padded)
    pad_start = pad_end - padded
    dest = pad_start[e] + rank
    tok = jnp.arange(N_ASSIGN, dtype=I32) // MOE_TOPK
    buf_tok = jnp.zeros((MOE_BUF,), I32).at[dest].set(tok)
    blocks = ((pad_start // MOE_ROWS).astype(I32), (padded // MOE_ROWS).astype(I32),
              (pad_end[-1:] // MOE_ROWS).astype(I32))
    pos = jnp.zeros((TOK_PAD * MOE_TOPK,), I32).at[:N_ASSIGN].set(dest)
    return buf_tok, blocks, pos


def _rope_tables():
    half = ROT_DIM // 2
    inv = ROPE_THETA ** (-jnp.arange(half, dtype=F32) * 2.0 / ROT_DIM)
    pos = jnp.concatenate([jnp.arange(SEQ, dtype=I32), jnp.full((TOK_PAD - SEQ,), PAST_LEN, I32)])
    ang = pos.astype(F32)[:, None] * inv[None, :]
    cos, sin = jnp.cos(ang), jnp.sin(ang)
    rest = LANES - ROT_DIM
    cf = jnp.concatenate([cos, cos, jnp.ones((TOK_PAD, rest), F32)], axis=1)
    s1 = jnp.concatenate([-sin, jnp.zeros((TOK_PAD, LANES - half), F32)], axis=1)
    s2 = jnp.concatenate([jnp.zeros((TOK_PAD, half), F32), sin, jnp.zeros((TOK_PAD, rest), F32)], axis=1)
    return cf, s1, s2


def _tokens(prompt, sample):
    pad = jnp.zeros((TOK_PAD - N_TOK, prompt.shape[1]), prompt.dtype)
    return jnp.concatenate([prompt, sample, pad], axis=0)


def _put_sample(buf, rows):
    tail = jnp.zeros((TOK_PAD - SEQ, buf.shape[1]), buf.dtype).at[:rows.shape[0]].set(rows.astype(buf.dtype))
    return lax.dynamic_update_slice(buf, tail, (SEQ, 0))


def kernel(x_prompt, x_sample, state_ssm, state_conv, cache_k, cache_v, page_table, p_prompt, p_sample, ln_g, ln_b, ssm_w_in, ssm_conv_w, ssm_conv_b, ssm_dt_bias, ssm_A_log, ssm_D, ssm_norm_w, ssm_w_out, attn_w_kv, attn_w_q, attn_w_o, ffn_w_gu, ffn_w_down, moe_w_router, moe_b_router, moe_w_gu, moe_w_down, ple_w_gate, ple_b_gate, ple_w_proj):
    x0b = _tokens(x_prompt[0].astype(BF16), x_sample[:, 0].astype(BF16))
    x0_tail = jnp.zeros((TOK_PAD - SEQ, D_MODEL), F32).at[:DEC_BATCH].set(x_sample[:, 0])
    p_tok = [_tokens(p_prompt[i, 0], p_sample[i, :, 0]).astype(BF16) for i in range(2)]

    w_in_t = jnp.swapaxes(ssm_w_in, 1, 2)[0]
    zx_cols = SSM_D_INNER + SSM_CONV_DIM
    zx = matmul(x0b, w_in_t, bn=1024, ncols=zx_cols, w_transposed=True)
    dt_raw = matmul(x0b, w_in_t, bn=LANES, col0=zx_cols, ncols=LANES, w_transposed=True)[:, :SSM_N_HEADS]

    st_conv = state_conv[0]
    x_raw_s = zx[SEQ:N_TOK, SSM_D_INNER:]
    xbc_s = conv_sample(jnp.transpose(st_conv, (1, 0, 2)), x_raw_s, ssm_conv_w[0], ssm_conv_b[0])
    conv_prompt_out = zx[SEQ - (SSM_CONV_W - 1):SEQ, SSM_D_INNER:].reshape(1, 1, SSM_CONV_W - 1, SSM_CONV_DIM)
    conv_sample_out = jnp.concatenate([st_conv[:, 1:], x_raw_s[:, None, :]], axis=1)[None]

    d_lanes = jnp.repeat(ssm_D[0], SSM_HEAD_DIM).reshape(1, SSM_D_INNER)
    gated, h_prompt = ssm_prompt(zx, dt_raw[:SEQ], ssm_conv_w[0], ssm_conv_b[0], ssm_dt_bias[0], ssm_A_log[0],
                                 d_lanes, ssm_norm_w[0])
    xs_s = xbc_s[:DEC_BATCH]
    gn = SSM_N_GROUPS * SSM_D_STATE
    h_sample, y_s = ssd_sample(
        state_ssm[0].reshape(DEC_BATCH, SSM_N_HEADS * SSM_HEAD_DIM, SSM_D_STATE),
        xs_s[:, :SSM_D_INNER],
        xs_s[:, SSM_D_INNER:SSM_D_INNER + gn].reshape(DEC_BATCH, SSM_N_GROUPS, SSM_D_STATE),
        xs_s[:, SSM_D_INNER + gn:].reshape(DEC_BATCH, SSM_N_GROUPS, SSM_D_STATE),
        dt_raw[SEQ:N_TOK], ssm_dt_bias[0], ssm_A_log[0], d_lanes)
    y_tail = jnp.zeros((TOK_PAD - SEQ, SSM_D_INNER), F32).at[:DEC_BATCH].set(y_s.reshape(DEC_BATCH, SSM_D_INNER))
    gated = lax.dynamic_update_slice(gated, gate_norm(y_tail, zx[SEQ:], ssm_norm_w[0]), (SEQ, 0))
    mix = matmul(gated, ssm_w_out[0], bn=512)
    x1, x1b = add_ln(x_prompt[0], mix, ln_g[0, 0], ln_b[0, 0], x_tail=x0_tail)

    hid = matmul_swiglu(x1b, ffn_w_gu[0], bn=512)
    ff = matmul(hid, ffn_w_down[0], bn=512)
    x2, x2b = add_ln(x1, ff, ln_g[0, 1], ln_b[0, 1])
    x3, x3b = ple_mix(x2, x2b, p_tok[0], ple_w_gate, ple_b_gate, ple_w_proj, 0)

    rope = _rope_tables()
    k_all = matmul(x3b, attn_w_kv, bn=KV_COLS, ncols=KV_COLS, rope=rope)
    v_all = matmul(x3b, attn_w_kv, bn=KV_COLS, col0=KV_COLS, ncols=KV_COLS)
    q_all = matmul(x3b, attn_w_q[0], bn=1024, rope=rope)
    att = moba_prompt(q_all, k_all, v_all, block_means(k_all).reshape(SEQ // MOBA_BLOCK, KV_COLS))

    q_s = q_all[SEQ:N_TOK].reshape(DEC_BATCH, ATTN_N_HEADS, ATTN_HEAD_DIM)
    k_s = k_all[SEQ:N_TOK].reshape(DEC_BATCH, ATTN_N_KV_HEADS, ATTN_HEAD_DIM)
    v_s = v_all[SEQ:N_TOK].reshape(DEC_BATCH, ATTN_N_KV_HEADS, ATTN_HEAD_DIM)
    n_pool = cache_k.shape[0]
    sel = sample_select(page_table, q_s, cache_k.reshape(n_pool, PAGE_ROWS, ATTN_HEAD_DIM))[:, :, :MOBA_TOPK]
    att_s = sample_attend(page_table, sel, q_s, k_s, v_s, cache_k, cache_v)
    att = _put_sample(att, att_s.reshape(DEC_BATCH, D_MODEL))
    mix = matmul(att, attn_w_o[0], bn=1024)
    x4, x4b = add_ln(x3, mix, ln_g[1, 0], ln_b[1, 0])

    idx, gates = router(x4b, moe_w_router[0], moe_b_router[0])
    buf_tok, blocks, pos = moe_plan(idx)
    xs = moe_gather(x4, buf_tok)
    hs = moe_expert_matmul(xs, moe_w_gu[0], blocks, swiglu=True, bn=1024, out_dtype=BF16)
    ys = moe_expert_matmul(hs, moe_w_down[0], blocks, swiglu=False, bn=512, out_dtype=F32)
    x5, x5b = moe_combine_ln(x4, ys, pos, gates, ln_g[1, 1], ln_b[1, 1])
    y_prompt, y_sample = ple_mix(x5, x5b, p_tok[1], ple_w_gate, ple_b_gate, ple_w_proj, 1, final=True)

    state_shape = (1, -1, SSM_N_HEADS, SSM_HEAD_DIM, SSM_D_STATE)
    kv_shape = (-1, ATTN_N_KV_HEADS, ATTN_HEAD_DIM)
    return (y_prompt.reshape(1, SEQ, D_MODEL),
            y_sample.reshape(DEC_BATCH, 1, D_MODEL),
            h_prompt.reshape(state_shape),
            conv_prompt_out,
            k_all[:SEQ].reshape((1,) + (SEQ,) + kv_shape[1:]),
            v_all[:SEQ].reshape((1,) + (SEQ,) + kv_shape[1:]),
            h_sample.reshape(state_shape),
            conv_sample_out,
            k_all[SEQ:N_TOK].reshape((DEC_BATCH, 1) + kv_shape[1:]),
            v_all[SEQ:N_TOK].reshape((DEC_BATCH, 1) + kv_shape[1:]))
```

```python
import functools

import jax
import jax.numpy as jnp
import numpy as np
from jax import lax
from jax.experimental import pallas as pl
from jax.experimental.pallas import tpu as pltpu

F32 = jnp.float32
BF16 = jnp.bfloat16
I32 = jnp.int32

D_MODEL = 2048
SEQ = 8192
DEC_BATCH = 32
PAST_LEN = 8192
PAGE_SIZE = 128

SSM_D_INNER = 4096
SSM_HEAD_DIM = 64
SSM_N_HEADS = 64
SSM_N_GROUPS = 8
SSM_D_STATE = 128
SSM_CONV_W = 4
SSM_CHUNK = 128
SSM_CONV_DIM = 6144
SSM_HEADS_PER_GROUP = SSM_N_HEADS // SSM_N_GROUPS

ATTN_HEAD_DIM = 128
ATTN_N_HEADS = 16
ATTN_N_KV_HEADS = 4
ATTN_GROUP = 4
ROT_DIM = 32
ROPE_THETA = 500000.0
MOBA_BLOCK = 256
MOBA_TOPK = 3
N_PAST_BLOCKS = PAST_LEN // MOBA_BLOCK
PAGES_PER_BLOCK = MOBA_BLOCK // PAGE_SIZE
N_PAGES = PAST_LEN // PAGE_SIZE

D_FF = 5632
N_EXPERTS = 8
MOE_TOPK = 2
D_FF_EXPERT = 7168
D_PLE = 256
DEEPNORM_ALPHA = 4.0 ** 0.25
LN_EPS = 1e-5
RMS_EPS = 1e-5

LANES = 128
VMEM_LIMIT_BYTES = 56 * 1024 * 1024

N_TOK = SEQ + DEC_BATCH
TOK_PAD = 8448
MM_ROWS = 768
ROW_BLOCK = 256
MOE_ROWS = 256
N_ASSIGN = N_TOK * MOE_TOPK
MOE_BLOCKS = -(-N_ASSIGN // MOE_ROWS) + N_EXPERTS
MOE_BUF = MOE_BLOCKS * MOE_ROWS
NEG_INF = float("-inf")


def _params(*sem):
    return pltpu.CompilerParams(dimension_semantics=sem, vmem_limit_bytes=VMEM_LIMIT_BYTES)


def _split3(a):
    hi = a.astype(BF16)
    r1 = a - hi.astype(F32)
    mid = r1.astype(BF16)
    lo = (r1 - mid.astype(F32)).astype(BF16)
    return hi, mid, lo


def _dot_f32(a, b, dims):
    a0, a1, a2 = _split3(a)
    b0, b1, b2 = _split3(b)
    dn = (dims, ((), ()))
    d = lambda x, y: lax.dot_general(x, y, dn, preferred_element_type=F32)
    return (d(a0, b0) + (d(a0, b1) + d(a1, b0))) + ((d(a0, b2) + d(a2, b0)) + d(a1, b1))


def _softplus(x):
    return jnp.maximum(x, 0.0) + jnp.log1p(jnp.exp(-jnp.abs(x)))


def _silu(x):
    return x * jax.nn.sigmoid(x)


def _rope_lanes(a, cf, s1, s2):
    return a * cf + pltpu.roll(a, LANES - ROT_DIM // 2, 1) * s1 + pltpu.roll(a, ROT_DIM // 2, 1) * s2


def _mm_kernel(x_ref, w_ref, *rest, rope, valid_cols, w_transposed):
    if rope:
        cf_ref, s1_ref, s2_ref, o_ref, wb_ref = rest
    else:
        o_ref, wb_ref = rest

    @pl.when(pl.program_id(1) == 0)
    def _():
        w = w_ref[...]
        if valid_cols is not None:
            axis = 0 if w_transposed else 1
            w = jnp.where(lax.broadcasted_iota(I32, w.shape, axis) < valid_cols, w, 0.0)
        wb_ref[...] = (w.T if w_transposed else w).astype(BF16)

    acc = jnp.dot(x_ref[...], wb_ref[...], preferred_element_type=F32)
    if rope:
        cf, s1, s2 = cf_ref[...], s1_ref[...], s2_ref[...]
        for c in range(acc.shape[1] // LANES):
            sl = slice(c * LANES, (c + 1) * LANES)
            o_ref[:, sl] = _rope_lanes(acc[:, sl], cf, s1, s2).astype(o_ref.dtype)
    else:
        o_ref[...] = acc.astype(o_ref.dtype)


def matmul(x, w, *, bn, col0=0, ncols=None, rope=None, out_dtype=F32, bm=MM_ROWS, w_transposed=False):
    m, k = x.shape
    n_total = w.shape[0] if w_transposed else w.shape[1]
    ncols = n_total if ncols is None else ncols
    assert m % bm == 0 and ncols % bn == 0 and col0 % bn == 0
    j0 = col0 // bn
    valid_cols = None
    if col0 + ncols > n_total:
        assert ncols == bn
        valid_cols = n_total - col0
    w_spec = (pl.BlockSpec((bn, k), lambda j, i: (j + j0, 0)) if w_transposed
              else pl.BlockSpec((k, bn), lambda j, i: (0, j + j0)))
    in_specs = [pl.BlockSpec((bm, k), lambda j, i: (i, 0)), w_spec]
    args = [x, w]
    if rope is not None:
        in_specs += [pl.BlockSpec((bm, LANES), lambda j, i: (i, 0))] * 3
        args += list(rope)
    return pl.pallas_call(
        functools.partial(_mm_kernel, rope=rope is not None, valid_cols=valid_cols, w_transposed=w_transposed),
        grid=(ncols // bn, m // bm),
        in_specs=in_specs,
        out_specs=pl.BlockSpec((bm, bn), lambda j, i: (i, j)),
        out_shape=jax.ShapeDtypeStruct((m, ncols), out_dtype),
        scratch_shapes=[pltpu.VMEM((k, bn), BF16)],
        compiler_params=_params("arbitrary", "arbitrary"),
        name="matmul",
    )(*args)


def _mm_swiglu_kernel(x_ref, wg_ref, wu_ref, o_ref, wgb_ref, wub_ref):
    @pl.when(pl.program_id(1) == 0)
    def _():
        wgb_ref[...] = wg_ref[...].astype(BF16)
        wub_ref[...] = wu_ref[...].astype(BF16)

    x = x_ref[...]
    g = jnp.dot(x, wgb_ref[...], preferred_element_type=F32)
    u = jnp.dot(x, wub_ref[...], preferred_element_type=F32)
    o_ref[...] = (_silu(g) * u).astype(o_ref.dtype)


def matmul_swiglu(x, w_gu, *, bn, bm=MM_ROWS):
    m, k = x.shape
    f = w_gu.shape[1] // 2
    assert m % bm == 0 and f % bn == 0
    nj = f // bn
    return pl.pallas_call(
        _mm_swiglu_kernel,
        grid=(nj, m // bm),
        in_specs=[pl.BlockSpec((bm, k), lambda j, i: (i, 0)),
                  pl.BlockSpec((k, bn), lambda j, i: (0, j)),
                  pl.BlockSpec((k, bn), lambda j, i: (0, j + nj))],
        out_specs=pl.BlockSpec((bm, bn), lambda j, i: (i, j)),
        out_shape=jax.ShapeDtypeStruct((m, f), BF16),
        scratch_shapes=[pltpu.VMEM((k, bn), BF16), pltpu.VMEM((k, bn), BF16)],
        compiler_params=_params("arbitrary", "arbitrary"),
        name="matmul_swiglu",
    )(x, w_gu, w_gu)


def _ple_kernel(xb_ref, pb_ref, wg_ref, wp_ref, bg_ref, x_ref, o_ref, o2_ref, wgb_ref, wpb_ref, *, final):
    @pl.when(pl.program_id(1) == 0)
    def _():
        wgb_ref[...] = wg_ref[...].astype(BF16)
        wpb_ref[...] = wp_ref[...].astype(BF16)

    gate = jax.nn.sigmoid(jnp.dot(xb_ref[...], wgb_ref[...], preferred_element_type=F32) + bg_ref[...])
    proj = jnp.dot(pb_ref[...], wpb_ref[...], preferred_element_type=F32)
    out = x_ref[...] + gate * proj
    o_ref[...] = out
    if final:
        tail0 = SEQ - (TOK_PAD - MM_ROWS)

        @pl.when(pl.program_id(1) == pl.num_programs(1) - 1)
        def _():
            o2_ref[...] = out[tail0:tail0 + DEC_BATCH, :]
    else:
        o2_ref[...] = out.astype(BF16)


def ple_mix(x, xb, pb, w_gate, b_gate, w_proj, layer, *, final=False, bn=1024, bm=MM_ROWS):
    m, d = x.shape
    kp = pb.shape[1]
    full = pl.BlockSpec((bm, bn), lambda j, i: (i, j))
    if final:
        assert m == TOK_PAD and bm == MM_ROWS and 0 <= SEQ - (TOK_PAD - MM_ROWS) <= MM_ROWS - DEC_BATCH
        out_specs = [full, pl.BlockSpec((DEC_BATCH, bn), lambda j, i: (0, j))]
        out_shape = [jax.ShapeDtypeStruct((SEQ, d), F32), jax.ShapeDtypeStruct((DEC_BATCH, d), F32)]
    else:
        out_specs = [full, full]
        out_shape = [jax.ShapeDtypeStruct((m, d), F32), jax.ShapeDtypeStruct((m, d), BF16)]
    return pl.pallas_call(
        functools.partial(_ple_kernel, final=final),
        grid=(d // bn, m // bm),
        in_specs=[pl.BlockSpec((bm, d), lambda j, i: (i, 0)),
                  pl.BlockSpec((bm, kp), lambda j, i: (i, 0)),
                  pl.BlockSpec((None, d, bn), lambda j, i: (layer, 0, j)),
                  pl.BlockSpec((None, kp, bn), lambda j, i: (layer, 0, j)),
                  pl.BlockSpec((None, 1, bn), lambda j, i: (layer, 0, j)),
                  full],
        out_specs=out_specs,
        out_shape=out_shape,
        scratch_shapes=[pltpu.VMEM((d, bn), BF16), pltpu.VMEM((kp, bn), BF16)],
        compiler_params=_params("arbitrary", "arbitrary"),
        name="ple_mix",
    )(xb, pb, w_gate, w_proj, b_gate.reshape(b_gate.shape[0], 1, d), x)


def _layernorm_rows(v, g, b):
    mu = jnp.mean(v, axis=-1, keepdims=True)
    c = v - mu
    var = jnp.mean(c * c, axis=-1, keepdims=True)
    return c * lax.rsqrt(var + LN_EPS) * g + b


def _add_ln_kernel(x_ref, xt_ref, m_ref, g_ref, b_ref, o_ref, ob_ref, *, n_main):
    x = x_ref[...]
    if n_main is not None:
        x = jnp.where(pl.program_id(0) < n_main, x, xt_ref[...])
    out = _layernorm_rows(DEEPNORM_ALPHA * x + m_ref[...], g_ref[...], b_ref[...])
    o_ref[...] = out
    ob_ref[...] = out.astype(BF16)


def add_ln(x, mix, g, b, x_tail=None):
    m, d = mix.shape
    row = pl.BlockSpec((ROW_BLOCK, d), lambda i: (i, 0))
    vec = pl.BlockSpec((1, d), lambda i: (0, 0))
    if x_tail is None:
        n_main, x_spec, x_tail = None, row, x
    else:
        n_main = x.shape[0] // ROW_BLOCK
        assert x_tail.shape[0] == ROW_BLOCK and n_main + 1 == m // ROW_BLOCK
        x_spec = pl.BlockSpec((ROW_BLOCK, d), lambda i: (jnp.minimum(i, n_main - 1), 0))
    return pl.pallas_call(
        functools.partial(_add_ln_kernel, n_main=n_main),
        grid=(m // ROW_BLOCK,),
        in_specs=[x_spec, pl.BlockSpec((ROW_BLOCK, d), lambda i: (0, 0)), row, vec, vec],
        out_specs=[row, row],
        out_shape=[jax.ShapeDtypeStruct((m, d), F32), jax.ShapeDtypeStruct((m, d), BF16)],
        compiler_params=_params("arbitrary"),
        name="add_ln",
    )(x, x_tail, mix, g.reshape(1, d), b.reshape(1, d))


def _gate_norm_kernel(y_ref, z_ref, w_ref, o_ref):
    g = y_ref[...] * _silu(z_ref[...])
    ms = jnp.mean(g * g, axis=-1, keepdims=True)
    o_ref[...] = (g * lax.rsqrt(ms + RMS_EPS) * w_ref[...]).astype(o_ref.dtype)


def gate_norm(y, zx, norm_w):
    m, di = y.shape
    row = pl.BlockSpec((ROW_BLOCK, di), lambda i: (i, 0))
    return pl.pallas_call(
        _gate_norm_kernel,
        grid=(m // ROW_BLOCK,),
        in_specs=[row, row, pl.BlockSpec((1, di), lambda i: (0, 0))],
        out_specs=row,
        out_shape=jax.ShapeDtypeStruct((m, di), BF16),
        compiler_params=_params("arbitrary"),
        name="gate_norm",
    )(y, zx, norm_w.reshape(1, di))


CONV_COLS = 2048


def _causal_conv_silu(x, halo, w, b):
    acc = b + x * w[SSM_CONV_W - 1:SSM_CONV_W, :]
    rows8 = lax.broadcasted_iota(I32, halo.shape, 0)
    for s in range(1, SSM_CONV_W):
        xr = pltpu.roll(x, s, 0)
        head = jnp.where(rows8 < s, pltpu.roll(halo, s, 0), xr[:8, :])
        shifted = jnp.concatenate([head, xr[8:, :]], axis=0)
        acc = acc + shifted * w[SSM_CONV_W - 1 - s:SSM_CONV_W - s, :]
    return _silu(acc)


def _conv_sample_kernel(st_ref, x_ref, w_ref, b_ref, o_ref):
    w = w_ref[...]
    acc = b_ref[...] + x_ref[...] * w[SSM_CONV_W - 1:SSM_CONV_W, :]
    for j in range(SSM_CONV_W - 1):
        acc = acc + st_ref[j] * w[j:j + 1, :]
    o_ref[...] = jnp.zeros(o_ref.shape, F32)
    o_ref[:DEC_BATCH, :] = _silu(acc)


def conv_sample(st, x_raw, conv_w, conv_b):
    return pl.pallas_call(
        _conv_sample_kernel,
        grid=(SSM_CONV_DIM // CONV_COLS,),
        in_specs=[pl.BlockSpec((SSM_CONV_W - 1, DEC_BATCH, CONV_COLS), lambda k: (0, 0, k)),
                  pl.BlockSpec((DEC_BATCH, CONV_COLS), lambda k: (0, k)),
                  pl.BlockSpec((SSM_CONV_W, CONV_COLS), lambda k: (0, k)),
                  pl.BlockSpec((1, CONV_COLS), lambda k: (0, k))],
        out_specs=pl.BlockSpec((TOK_PAD - SEQ, CONV_COLS), lambda k: (0, k)),
        out_shape=jax.ShapeDtypeStruct((TOK_PAD - SEQ, SSM_CONV_DIM), F32),
        compiler_params=_params("arbitrary"),
        name="conv_sample",
    )(st, x_raw, conv_w, conv_b.reshape(1, SSM_CONV_DIM))


SSD_GROUP_COLS = SSM_HEADS_PER_GROUP * SSM_HEAD_DIM
SSD_PAIR_ROWS = 2 * SSM_HEAD_DIM


N_CONV_BLOCKS = SSM_CONV_DIM // CONV_COLS


def _ssm_prompt_kernel(*refs):
    g_ref = refs[-4]

    @pl.when(pl.program_id(0) < SEQ // SSM_CHUNK)
    def _():
        _ssm_prompt_chunk(*refs)

    @pl.when(pl.program_id(0) >= SEQ // SSM_CHUNK)
    def _():
        g_ref[...] = jnp.zeros(g_ref.shape, g_ref.dtype)


def _ssm_prompt_chunk(*refs):
    nb = N_CONV_BLOCKS
    raw, halo = refs[:nb], refs[nb:2 * nb]
    (cw_ref, cb_ref, z_ref, dtc_ref, dtr_ref, bc_ref, br_ref, ac_ref, ar_ref, d_ref, nw_ref,
     g_ref, h_ref, xa_ref, y_ref) = refs[2 * nb:]
    first = pl.program_id(0) == 0
    step = 2 * LANES
    for k in range(nb):
        for c in range(0, CONV_COLS, step):
            src = slice(c, c + step)
            dst = slice(k * CONV_COLS + c, k * CONV_COLS + c + step)
            xa_ref[:, dst] = _causal_conv_silu(raw[k][:, src], jnp.where(first, 0.0, halo[k][:, src]),
                                               cw_ref[:, dst], cb_ref[:, dst])
    b0 = SSM_D_INNER
    c0 = SSM_D_INNER + SSM_N_GROUPS * SSM_D_STATE
    q = SSM_CHUNK
    dt_c = _softplus(dtc_ref[...] + bc_ref[...])
    dt_r = _softplus(dtr_ref[...] + br_ref[...])
    da_c = dt_c * -jnp.exp(ac_ref[...])
    da_r = dt_r * -jnp.exp(ar_ref[...])
    ti = lax.broadcasted_iota(I32, (q, q), 0)
    si = lax.broadcasted_iota(I32, (q, q), 1)
    causal = ti >= si
    tril = causal.astype(F32)
    triu = (ti <= si).astype(F32)
    acs_c = _dot_f32(tril, da_c, ((1,), (0,)))
    acs_r = _dot_f32(da_r, triu, ((1,), (0,)))
    last = acs_c[q - 1:q, :]

    @pl.when(pl.program_id(0) == 0)
    def _():
        h_ref[...] = jnp.zeros(h_ref.shape, F32)

    lane = lax.broadcasted_iota(I32, (q, LANES), 1)
    lo = lane < SSM_HEAD_DIM
    sub = lax.broadcasted_iota(I32, (SSD_PAIR_ROWS, 1), 0)
    for g in range(SSM_N_GROUPS):
        bm = xa_ref[:, b0 + g * SSM_D_STATE:b0 + (g + 1) * SSM_D_STATE].astype(BF16)
        cm = xa_ref[:, c0 + g * SSM_D_STATE:c0 + (g + 1) * SSM_D_STATE].astype(BF16)
        cb = lax.dot_general(cm, bm, (((1,), (1,)), ((), ())), preferred_element_type=F32)
        for pair in range(SSM_HEADS_PER_GROUP // 2):
            r1 = g * SSM_HEADS_PER_GROUP + 2 * pair
            r2 = r1 + 1
            cols = slice(r1 * SSM_HEAD_DIM, (r2 + 1) * SSM_HEAD_DIM)
            x = xa_ref[:, cols]
            xdt = x * jnp.where(lo, dt_c[:, r1:r1 + 1], dt_c[:, r2:r2 + 1])
            ms = []
            for r in (r1, r2):
                seg = acs_c[:, r:r + 1] - acs_r[r:r + 1, :]
                ms.append((cb * jnp.exp(jnp.where(causal, seg, NEG_INF))).astype(BF16))
            y = (jnp.dot(ms[0], jnp.where(lo, xdt, 0.0).astype(BF16), preferred_element_type=F32)
                 + jnp.dot(ms[1], jnp.where(lo, 0.0, xdt).astype(BF16), preferred_element_type=F32))

            h_in = h_ref[cols, :]
            y_off = lax.dot_general(cm, h_in.astype(BF16), (((1,), (1,)), ((), ())),
                                    preferred_element_type=F32)
            y = y + y_off * jnp.exp(jnp.where(lo, acs_c[:, r1:r1 + 1], acs_c[:, r2:r2 + 1]))
            y_ref[:, cols] = y + d_ref[:, cols] * x

            to_end = jnp.exp(jnp.where(lo, last[:, r1:r1 + 1] - acs_c[:, r1:r1 + 1],
                                       last[:, r2:r2 + 1] - acs_c[:, r2:r2 + 1]))
            xw_t = (xdt * to_end).T.astype(BF16)
            st = jnp.dot(xw_t, bm, preferred_element_type=F32)
            decay = jnp.exp(jnp.where(sub < SSM_HEAD_DIM, last[:, r1:r1 + 1], last[:, r2:r2 + 1]))
            h_ref[cols, :] = h_in * decay + st

    gated = y_ref[...] * _silu(z_ref[...])
    ms = jnp.mean(gated * gated, axis=-1, keepdims=True)
    g_ref[...] = (gated * lax.rsqrt(ms + RMS_EPS) * nw_ref[...]).astype(g_ref.dtype)


def ssm_prompt(zx, dt_raw, conv_w, conv_b, dt_bias, a_log, d_lanes, norm_w):
    nc = SEQ // SSM_CHUNK
    nh = SSM_N_HEADS
    cb0 = SSM_D_INNER // CONV_COLS
    halo_rows = SSM_CHUNK // 8
    clamp = lambda c: jnp.minimum(c, nc - 1)
    const = lambda shape: pl.BlockSpec(shape, lambda c: (0,) * len(shape))
    raw_specs = [pl.BlockSpec((SSM_CHUNK, CONV_COLS), functools.partial(lambda k, c: (c, cb0 + k), k))
                 for k in range(N_CONV_BLOCKS)]
    halo_specs = [pl.BlockSpec((8, CONV_COLS),
                               functools.partial(lambda k, c: (jnp.maximum(c * halo_rows - 1, 0), cb0 + k), k))
                  for k in range(N_CONV_BLOCKS)]
    return pl.pallas_call(
        _ssm_prompt_kernel,
        grid=(TOK_PAD // SSM_CHUNK,),
        in_specs=raw_specs + halo_specs + [
            const((SSM_CONV_W, SSM_CONV_DIM)), const((1, SSM_CONV_DIM)),
            pl.BlockSpec((SSM_CHUNK, SSM_D_INNER), lambda c: (c, 0)),
            pl.BlockSpec((SSM_CHUNK, nh), lambda c: (clamp(c), 0)),
            pl.BlockSpec((nh, SSM_CHUNK), lambda c: (0, clamp(c))),
            const((1, nh)), const((nh, 1)), const((1, nh)), const((nh, 1)),
            const((1, SSM_D_INNER)), const((1, SSM_D_INNER))],
        out_specs=[pl.BlockSpec((SSM_CHUNK, SSM_D_INNER), lambda c: (c, 0)),
                   const((nh * SSM_HEAD_DIM, SSM_D_STATE))],
        out_shape=[jax.ShapeDtypeStruct((TOK_PAD, SSM_D_INNER), BF16),
                   jax.ShapeDtypeStruct((nh * SSM_HEAD_DIM, SSM_D_STATE), F32)],
        scratch_shapes=[pltpu.VMEM((SSM_CHUNK, SSM_CONV_DIM), F32), pltpu.VMEM((SSM_CHUNK, SSM_D_INNER), F32)],
        compiler_params=_params("arbitrary"),
        name="ssm_prompt",
    )(*([zx] * (2 * N_CONV_BLOCKS)), conv_w, conv_b.reshape(1, SSM_CONV_DIM), zx, dt_raw, dt_raw.T,
      dt_bias.reshape(1, nh), dt_bias.reshape(nh, 1), a_log.reshape(1, nh), a_log.reshape(nh, 1),
      d_lanes, norm_w.reshape(1, SSM_D_INNER))


def _mxu_round(x):
    return x.astype(BF16).astype(F32)


def _ssd_sample_kernel(h0_ref, xt_ref, xl_ref, b_ref, c_ref, dt_ref, dtl_ref, bias_ref, biasl_ref,
                       alog_ref, alogl_ref, d_ref, h_ref, y_ref):
    dt = _softplus(dt_ref[...] + bias_ref[...])
    dec = jnp.exp(dt * -jnp.exp(alog_ref[...]))
    dt_l = _softplus(dtl_ref[...] + biasl_ref[...])
    dec_l = jnp.exp(dt_l * -jnp.exp(alogl_ref[...]))
    xdt_t = _mxu_round(xt_ref[...] * dt)
    xdt_l = _mxu_round(xl_ref[...] * dt_l)
    bm = _mxu_round(b_ref[...])
    cm = _mxu_round(c_ref[...])
    for grp in range(SSM_N_GROUPS):
        b_row = bm[grp:grp + 1, :]
        c_row = cm[grp:grp + 1, :]
        cb = _mxu_round(jnp.sum(c_row * b_row, axis=1, keepdims=True))
        old = []
        for r in range(SSM_HEADS_PER_GROUP):
            h = grp * SSM_HEADS_PER_GROUP + r
            rows = slice(h * SSM_HEAD_DIM, (h + 1) * SSM_HEAD_DIM)
            h0 = h0_ref[rows, :]
            h_ref[rows, :] = h0 * dec[:, h:h + 1] + xdt_t[:, h:h + 1] * b_row
            old.append(h0)
        h0_grp = jnp.concatenate(old, axis=0).astype(BF16)
        c_rows = jnp.broadcast_to(c_row, (8, SSM_D_STATE)).astype(BF16)
        ch0 = lax.dot_general(c_rows, h0_grp, (((1,), (1,)), ((), ())), preferred_element_type=F32)[0:1, :]
        cols = slice(grp * SSD_GROUP_COLS, (grp + 1) * SSD_GROUP_COLS)
        y_ref[:, cols] = (cb * xdt_l[:, cols] + ch0 * dec_l[:, cols]) + d_ref[:, cols] * xl_ref[:, cols]


def ssd_sample(h0, x, bm, cm, dt_raw, dt_bias, a_log, d_lanes):
    nb = h0.shape[0]
    nh, p = SSM_N_HEADS, SSM_HEAD_DIM
    hp = nh * p
    lanes = lambda v: jnp.repeat(v, p, axis=-1)
    x_t = jnp.transpose(x.reshape(nb, nh, p), (0, 2, 1))
    per_b = lambda *shape: pl.BlockSpec((None,) + shape, lambda b: (b,) + (0,) * len(shape))
    vec = lambda n: pl.BlockSpec((1, n), lambda b: (0, 0))
    return pl.pallas_call(
        _ssd_sample_kernel,
        grid=(nb,),
        in_specs=[per_b(hp, SSM_D_STATE), per_b(p, nh), per_b(1, hp),
                  per_b(SSM_N_GROUPS, SSM_D_STATE), per_b(SSM_N_GROUPS, SSM_D_STATE),
                  per_b(1, nh), per_b(1, hp), vec(nh), vec(hp), vec(nh), vec(hp), vec(hp)],
        out_specs=[per_b(hp, SSM_D_STATE), per_b(1, hp)],
        out_shape=[jax.ShapeDtypeStruct((nb, hp, SSM_D_STATE), F32),
                   jax.ShapeDtypeStruct((nb, 1, hp), F32)],
        compiler_params=_params("arbitrary"),
        name="ssd_sample",
    )(h0, x_t, x.reshape(nb, 1, hp), bm, cm, dt_raw.reshape(nb, 1, nh), lanes(dt_raw).reshape(nb, 1, hp),
      dt_bias.reshape(1, nh), lanes(dt_bias).reshape(1, hp), a_log.reshape(1, nh), lanes(a_log).reshape(1, hp),
      d_lanes)


KV_COLS = ATTN_N_KV_HEADS * ATTN_HEAD_DIM
Q_LANES = ATTN_GROUP * MOBA_BLOCK


def _kmeans_kernel(k_ref, o_ref):
    o_ref[...] = jnp.mean(k_ref[...], axis=0, keepdims=True).reshape(o_ref.shape)


def block_means(k):
    nb = SEQ // MOBA_BLOCK
    return pl.pallas_call(
        _kmeans_kernel,
        grid=(nb,),
        in_specs=[pl.BlockSpec((MOBA_BLOCK, KV_COLS), lambda i: (i, 0))],
        out_specs=pl.BlockSpec((None, 1, KV_COLS), lambda i: (i, 0, 0)),
        out_shape=jax.ShapeDtypeStruct((nb, 1, KV_COLS), F32),
        compiler_params=_params("arbitrary"),
        name="block_means",
    )(k)


def _top3_mask_rows(gate):
    n = gate.shape[0]
    rows = lax.broadcasted_iota(I32, gate.shape, 0)
    sel = jnp.zeros(gate.shape, jnp.bool_)
    for _ in range(MOBA_TOPK):
        m = jnp.max(gate, axis=0, keepdims=True)
        first = jnp.min(jnp.where(gate == m, rows, n), axis=0, keepdims=True)
        hit = jnp.logical_and(rows == first, m > NEG_INF)
        sel = jnp.logical_or(sel, hit)
        gate = jnp.where(rows == first, NEG_INF, gate)
    return sel


MASKED_SCORE = -1e30
LOG2_E = 1.4426950408889634


NO_BLOCK_LANE = LANES - 1
DENOM_ROWS = 16


def _moba_prompt_kernel(qi_ref, fl_ref, ida_ref, idb_ref, q_ref, ka_ref, kb_ref, va_ref, vb_ref, mean_ref, o_ref,
                        qaug_ref, s_ref, m_ref, acc_ref):
    t = pl.program_id(0)
    i = qi_ref[t]
    is_first = (fl_ref[t] & 1) == 1
    is_last = (fl_ref[t] & 2) == 2
    nb = mean_ref.shape[0]
    q_scale = ATTN_HEAD_DIM ** -0.5 * LOG2_E
    hd = ATTN_HEAD_DIM

    @pl.when(i == nb)
    def _():
        o_ref[...] = jnp.zeros(o_ref.shape, o_ref.dtype)

    @pl.when(jnp.logical_and(i < nb, is_first))
    def _():
        blk = lax.broadcasted_iota(I32, (nb, Q_LANES), 0)
        pad_row = lax.broadcasted_iota(I32, (LANES - nb, MOBA_BLOCK), 0)
        pad = jnp.where(pad_row == NO_BLOCK_LANE - nb, MASKED_SCORE, 0.0)
        for g in range(ATTN_N_KV_HEADS):
            q_g = jnp.concatenate([q_ref[:, (ATTN_GROUP * g + r) * hd:(ATTN_GROUP * g + r + 1) * hd]
                                   for r in range(ATTN_GROUP)], axis=0)
            gate = lax.dot_general(mean_ref[:, g * hd:(g + 1) * hd].astype(BF16), q_g.astype(BF16),
                                   (((1,), (1,)), ((), ())), preferred_element_type=F32)
            sel = _top3_mask_rows(jnp.where(blk < i, gate, NEG_INF))
            bias = jnp.where(jnp.logical_or(sel, blk == i), 0.0, MASKED_SCORE)
            bias_q = jnp.concatenate(
                [jnp.concatenate([bias[:, r * MOBA_BLOCK:(r + 1) * MOBA_BLOCK], pad], axis=0).T
                 for r in range(ATTN_GROUP)], axis=0)
            qaug_ref[g] = jnp.concatenate([(q_g * q_scale).astype(BF16), bias_q.astype(BF16)], axis=1)
        m_ref[...] = jnp.full(m_ref.shape, NEG_INF, F32)
        acc_ref[...] = jnp.zeros(acc_ref.shape, F32)

    def attend(own):
        blk_lane = lax.broadcasted_iota(I32, (MOBA_BLOCK, LANES), 1)
        halves = [(ka_ref, jnp.where(blk_lane == ida_ref[t], 1.0, 0.0).astype(BF16)),
                  (kb_ref, jnp.where(blk_lane == idb_ref[t], 1.0, 0.0).astype(BF16))]
        for g in range(ATTN_N_KV_HEADS):
            k_aug = jnp.concatenate(
                [jnp.concatenate([k_ref[:, g * hd:(g + 1) * hd].astype(BF16), onehot], axis=1)
                 for k_ref, onehot in halves], axis=0)
            s_ref[g] = lax.dot_general(k_aug, qaug_ref[g], (((1,), (1,)), ((), ())), preferred_element_type=F32)
        sum_row = lax.broadcasted_iota(I32, (DENOM_ROWS, 2 * MOBA_BLOCK), 0) == 0
        ones_rows = jnp.where(sum_row, 1.0, 0.0).astype(BF16)
        if own:
            key = lax.broadcasted_iota(I32, (2 * MOBA_BLOCK, Q_LANES), 0)
            qry = lax.broadcasted_iota(I32, (2 * MOBA_BLOCK, Q_LANES), 1) & (MOBA_BLOCK - 1)
            visible = jnp.logical_or(key <= qry, key >= MOBA_BLOCK)
        for g in range(ATTN_N_KV_HEADS):
            s = s_ref[g]
            if own:
                s = jnp.where(visible, s, NEG_INF)
            m_old = m_ref[g]
            m_new = jnp.maximum(m_old, jnp.max(s, axis=0, keepdims=True))
            alpha = jnp.exp2(m_old - m_new)
            p = jnp.exp2(s - m_new)
            m_ref[g] = m_new
            v_t = jnp.concatenate([va_ref[:, g * hd:(g + 1) * hd], vb_ref[:, g * hd:(g + 1) * hd]],
                                  axis=0).T.astype(BF16)
            v_aug = jnp.concatenate([v_t, ones_rows], axis=0)
            acc_ref[g] = alpha * acc_ref[g] + jnp.dot(v_aug, p.astype(BF16), preferred_element_type=F32)

    @pl.when(jnp.logical_and(i < nb, is_first))
    def _():
        attend(True)

    @pl.when(jnp.logical_and(i < nb, jnp.logical_not(is_first)))
    def _():
        attend(False)

    @pl.when(jnp.logical_and(i < nb, is_last))
    def _():
        for g in range(ATTN_N_KV_HEADS):
            out = (acc_ref[g, :hd, :] / acc_ref[g, hd:hd + 1, :]).T
            for r in range(ATTN_GROUP):
                h = ATTN_GROUP * g + r
                o_ref[:, h * hd:(h + 1) * hd] = out[r * MOBA_BLOCK:(r + 1) * MOBA_BLOCK, :].astype(o_ref.dtype)


def moba_prompt(q, k, v, means):
    nb = SEQ // MOBA_BLOCK
    assert nb < NO_BLOCK_LANE
    steps = []
    for i in range(nb):
        pairs = [(i, i - 1)] + [(a, a - 1) for a in range(i - 2, -1, -2)]
        for n, (a, b) in enumerate(pairs):
            flags = (1 if n == 0 else 0) | (2 if n == len(pairs) - 1 else 0)
            steps.append((i, flags, a, b if b >= 0 else NO_BLOCK_LANE))
    steps.append((nb, 0, NO_BLOCK_LANE, NO_BLOCK_LANE))
    tables = [jnp.asarray(np.array([s[c] for s in steps], np.int32)) for c in range(4)]
    blk = lambda ids, t: jnp.where(ids[t] < nb, ids[t], 0)
    ka_spec = pl.BlockSpec((MOBA_BLOCK, KV_COLS), lambda t, qi, fl, ida, idb: (blk(ida, t), 0))
    kb_spec = pl.BlockSpec((MOBA_BLOCK, KV_COLS), lambda t, qi, fl, ida, idb: (blk(idb, t), 0))
    row_spec = pl.BlockSpec((MOBA_BLOCK, D_MODEL), lambda t, qi, *_: (qi[t], 0))
    return pl.pallas_call(
        _moba_prompt_kernel,
        grid_spec=pltpu.PrefetchScalarGridSpec(
            num_scalar_prefetch=4,
            grid=(len(steps),),
            in_specs=[row_spec, ka_spec, kb_spec, ka_spec, kb_spec,
                      pl.BlockSpec((nb, KV_COLS), lambda t, *_: (0, 0))],
            out_specs=row_spec,
            scratch_shapes=[pltpu.VMEM((ATTN_N_KV_HEADS, Q_LANES, ATTN_HEAD_DIM + LANES), BF16),
                            pltpu.VMEM((ATTN_N_KV_HEADS, 2 * MOBA_BLOCK, Q_LANES), F32),
                            pltpu.VMEM((ATTN_N_KV_HEADS, 1, Q_LANES), F32),
                            pltpu.VMEM((ATTN_N_KV_HEADS, ATTN_HEAD_DIM + DENOM_ROWS, Q_LANES), F32)]),
        out_shape=jax.ShapeDtypeStruct((TOK_PAD, D_MODEL), BF16),
        compiler_params=_params("arbitrary"),
        name="moba_prompt",
    )(*tables, q, k, k, v, v, means)


PAGE_ROWS = PAGE_SIZE * ATTN_N_KV_HEADS
SEL_KEYS = MOBA_TOPK * MOBA_BLOCK


def _page_copy(pool_hbm, page, dst, sem):
    return pltpu.make_async_copy(pool_hbm.at[page], dst, sem)


def _sample_select_kernel(pt_ref, q_ref, kc_hbm, o_ref, kbuf, means_ref, sems):
    b = pl.program_id(0)
    slot = b % 2

    def fetch(seq, slot_, start):
        for p in range(N_PAGES):
            cp = _page_copy(kc_hbm, pt_ref[seq * N_PAGES + p], kbuf.at[slot_, p], sems.at[slot_])
            cp.start() if start else cp.wait()

    @pl.when(b == 0)
    def _():
        fetch(b, slot, True)

    @pl.when(b + 1 < pl.num_programs(0))
    def _():
        fetch(b + 1, 1 - slot, True)

    fetch(b, slot, False)
    fold = 8 // ATTN_N_KV_HEADS
    for n in range(N_PAST_BLOCKS):
        s = jnp.zeros((8, ATTN_HEAD_DIM), F32)
        for p in range(PAGES_PER_BLOCK):
            s = s + jnp.sum(kbuf[slot, n * PAGES_PER_BLOCK + p].reshape(PAGE_ROWS // 8, 8, ATTN_HEAD_DIM), axis=0)
        tot = s[:ATTN_N_KV_HEADS, :]
        for f in range(1, fold):
            tot = tot + s[f * ATTN_N_KV_HEADS:(f + 1) * ATTN_N_KV_HEADS, :]
        means_ref[n * ATTN_N_KV_HEADS:(n + 1) * ATTN_N_KV_HEADS, :] = tot * (1.0 / MOBA_BLOCK)
    gate = lax.dot_general(q_ref[...].astype(BF16), means_ref[...].astype(BF16), (((1,), (1,)), ((), ())),
                           preferred_element_type=F32)
    shape = gate.shape
    lane = lax.broadcasted_iota(I32, shape, 1)
    head = lax.broadcasted_iota(I32, shape, 0)
    gate = jnp.where((lane & (ATTN_N_KV_HEADS - 1)) == (head >> 2), gate, NEG_INF)
    out = jnp.zeros(shape, I32)
    for t in range(MOBA_TOPK):
        m = jnp.max(gate, axis=1, keepdims=True)
        first = jnp.min(jnp.where(gate == m, lane, shape[1]), axis=1, keepdims=True)
        out = jnp.where(lane == t, first >> 2, out)
        gate = jnp.where(lane == first, NEG_INF, gate)
    o_ref[...] = out


def sample_select(page_table, q_s, kc_pages):
    nb = q_s.shape[0]
    assert N_PAST_BLOCKS * ATTN_N_KV_HEADS == LANES
    return pl.pallas_call(
        _sample_select_kernel,
        grid_spec=pltpu.PrefetchScalarGridSpec(
            num_scalar_prefetch=1,
            grid=(nb,),
            in_specs=[pl.BlockSpec((None, ATTN_N_HEADS, ATTN_HEAD_DIM), lambda b, pt: (b, 0, 0)),
                      pl.BlockSpec(memory_space=pl.ANY)],
            out_specs=pl.BlockSpec((None, ATTN_N_HEADS, LANES), lambda b, pt: (b, 0, 0)),
            scratch_shapes=[pltpu.VMEM((2, N_PAGES, PAGE_ROWS, ATTN_HEAD_DIM), F32),
                            pltpu.VMEM((N_PAST_BLOCKS * ATTN_N_KV_HEADS, ATTN_HEAD_DIM), F32),
                            pltpu.SemaphoreType.DMA((2,))]),
        out_shape=jax.ShapeDtypeStruct((nb, ATTN_N_HEADS, LANES), I32),
        compiler_params=_params("arbitrary"),
        name="sample_select",
    )(page_table.reshape(-1), q_s, kc_pages)


def _head_rows_copy(pool_hbm, page, g, dst, sem):
    return pltpu.make_async_copy(pool_hbm.at[page, :, g, :], dst, sem)


def _sample_attend_kernel(pt_ref, sel_ref, q_ref, kn_ref, vn_ref, kc_hbm, vc_hbm, o_ref, kbuf, vbuf, sems):
    b = pl.program_id(0)
    slot = b % 2
    scale = ATTN_HEAD_DIM ** -0.5

    def fetch(seq, slot_, start):
        for h in range(ATTN_N_HEADS):
            g = h // ATTN_GROUP
            for t in range(MOBA_TOPK):
                blk = sel_ref[(seq * ATTN_N_HEADS + h) * MOBA_TOPK + t]
                for half in range(PAGES_PER_BLOCK):
                    page = pt_ref[seq * N_PAGES + blk * PAGES_PER_BLOCK + half]
                    rows = pl.ds(t * MOBA_BLOCK + half * PAGE_SIZE, PAGE_SIZE)
                    for cp in (_head_rows_copy(kc_hbm, page, g, kbuf.at[slot_, h, rows, :], sems.at[0, slot_]),
                               _head_rows_copy(vc_hbm, page, g, vbuf.at[slot_, h, rows, :], sems.at[1, slot_])):
                        cp.start() if start else cp.wait()

    @pl.when(b == 0)
    def _():
        fetch(b, slot, True)

    @pl.when(b + 1 < pl.num_programs(0))
    def _():
        fetch(b + 1, 1 - slot, True)

    fetch(b, slot, False)

    for h in range(ATTN_N_HEADS):
        g = h // ATTN_GROUP
        qh = q_ref[h:h + 1, :]
        q8 = jnp.broadcast_to(qh, (8, ATTN_HEAD_DIM)).astype(BF16)
        s = lax.dot_general(kbuf[slot, h].astype(BF16), q8, (((1,), (1,)), ((), ())),
                            preferred_element_type=F32)[:, 0:1] * scale
        s_own = jnp.sum(_mxu_round(qh) * _mxu_round(kn_ref[g:g + 1, :]), axis=1, keepdims=True) * scale
        m = jnp.maximum(jnp.max(s, axis=0, keepdims=True), s_own)
        p = jnp.exp(s - m)
        p_own = jnp.exp(s_own - m)
        denom = jnp.sum(p, axis=0, keepdims=True) + p_own
        o_sel = jnp.sum(_mxu_round(p / denom) * _mxu_round(vbuf[slot, h]), axis=0, keepdims=True)
        o_ref[h:h + 1, :] = o_sel + _mxu_round(p_own / denom) * _mxu_round(vn_ref[g:g + 1, :])


def sample_attend(page_table, sel, q_s, k_new, v_new, cache_k, cache_v):
    nb = q_s.shape[0]
    per_b = lambda rows: pl.BlockSpec((None, rows, ATTN_HEAD_DIM), lambda b, pt, sl: (b, 0, 0))
    any_spec = pl.BlockSpec(memory_space=pl.ANY)
    return pl.pallas_call(
        _sample_attend_kernel,
        grid_spec=pltpu.PrefetchScalarGridSpec(
            num_scalar_prefetch=2,
            grid=(nb,),
            in_specs=[per_b(ATTN_N_HEADS), per_b(ATTN_N_KV_HEADS), per_b(ATTN_N_KV_HEADS), any_spec, any_spec],
            out_specs=per_b(ATTN_N_HEADS),
            scratch_shapes=[pltpu.VMEM((2, ATTN_N_HEADS, SEL_KEYS, ATTN_HEAD_DIM), F32),
                            pltpu.VMEM((2, ATTN_N_HEADS, SEL_KEYS, ATTN_HEAD_DIM), F32),
                            pltpu.SemaphoreType.DMA((2, 2))]),
        out_shape=jax.ShapeDtypeStruct((nb, ATTN_N_HEADS, ATTN_HEAD_DIM), F32),
        compiler_params=_params("arbitrary"),
        name="sample_attend",
    )(page_table.reshape(-1), sel.reshape(-1), q_s, k_new, v_new, cache_k, cache_v)


def _router_kernel(x_ref, w_ref, b_ref, idx_ref, gate_ref):
    logits = jnp.dot(x_ref[...], w_ref[...].astype(BF16), preferred_element_type=F32) + b_ref[...]
    lane = lax.broadcasted_iota(I32, logits.shape, 1)
    logits = jnp.where(lane < N_EXPERTS, logits, NEG_INF)
    m1 = jnp.max(logits, axis=1, keepdims=True)
    i1 = jnp.min(jnp.where(logits == m1, lane, LANES), axis=1, keepdims=True)
    rest = jnp.where(lane == i1, NEG_INF, logits)
    m2 = jnp.max(rest, axis=1, keepdims=True)
    i2 = jnp.min(jnp.where(rest == m2, lane, LANES), axis=1, keepdims=True)
    e2 = jnp.exp(m2 - m1)
    g1 = 1.0 / (1.0 + e2)
    idx_ref[...] = jnp.where(lane == 0, i1, jnp.where(lane == 1, i2, 0))
    gate_ref[...] = jnp.where(lane == 0, g1, jnp.where(lane == 1, e2 * g1, 0.0))


def router(x, w_router, b_router):
    m, d = x.shape
    wp = jnp.zeros((d, LANES), F32).at[:, :N_EXPERTS].set(w_router)
    bp = jnp.zeros((1, LANES), F32).at[0, :N_EXPERTS].set(b_router)
    row = pl.BlockSpec((ROW_BLOCK, d), lambda i: (i, 0))
    out = pl.BlockSpec((ROW_BLOCK, LANES), lambda i: (i, 0))
    return pl.pallas_call(
        _router_kernel,
        grid=(m // ROW_BLOCK,),
        in_specs=[row, pl.BlockSpec((d, LANES), lambda i: (0, 0)), pl.BlockSpec((1, LANES), lambda i: (0, 0))],
        out_specs=[out, out],
        out_shape=[jax.ShapeDtypeStruct((m, LANES), I32), jax.ShapeDtypeStruct((m, LANES), F32)],
        compiler_params=_params("arbitrary"),
        name="router",
    )(x, wp, bp)


def _row_copy(src_hbm, row, dst, slot, sem):
    return pltpu.make_async_copy(src_hbm.at[pl.ds(row, 1), :], dst.at[pl.ds(slot, 1), :], sem)


GATHER_UNROLL = 8


def _moe_gather_kernel(tok_ref, nxt_ref, x_hbm, o_ref, buf, sems):
    i = pl.program_id(0)
    slot = i % 2

    def issue(idx_ref, slot_):
        def body(r2, carry):
            for q in range(2):
                r = 2 * r2 + q
                _row_copy(x_hbm, idx_ref[0, 0, r], buf.at[slot_], r, sems.at[slot_]).start(priority=q)
            return carry
        lax.fori_loop(0, MOE_ROWS // 2, body, 0, unroll=GATHER_UNROLL // 2)

    @pl.when(i == 0)
    def _():
        issue(tok_ref, slot)

    @pl.when(i + 1 < pl.num_programs(0))
    def _():
        issue(nxt_ref, 1 - slot)

    def wait(r, carry):
        _row_copy(x_hbm, 0, buf.at[slot], r, sems.at[slot]).wait()
        return carry

    lax.fori_loop(0, MOE_ROWS, wait, 0, unroll=GATHER_UNROLL)
    o_ref[...] = buf[slot].astype(o_ref.dtype)


def moe_gather(x, buf_tok):
    d = x.shape[1]
    tok = buf_tok.reshape(MOE_BLOCKS, 1, MOE_ROWS)
    idx_spec = lambda step: pl.BlockSpec((1, 1, MOE_ROWS), lambda i: (jnp.minimum(i + step, MOE_BLOCKS - 1), 0, 0),
                                         memory_space=pltpu.SMEM)
    return pl.pallas_call(
        _moe_gather_kernel,
        grid=(MOE_BLOCKS,),
        in_specs=[idx_spec(0), idx_spec(1), pl.BlockSpec(memory_space=pl.ANY)],
        out_specs=pl.BlockSpec((MOE_ROWS, d), lambda i: (i, 0)),
        out_shape=jax.ShapeDtypeStruct((MOE_BUF, d), BF16),
        scratch_shapes=[pltpu.VMEM((2, MOE_ROWS, d), F32), pltpu.SemaphoreType.DMA((2,))],
        compiler_params=_params("arbitrary"),
        name="moe_gather",
    )(tok, tok, x)


W_CHUNKS = 8


def _moe_expert_kernel(first_ref, count_ref, used_ref, w_hbm, x_hbm, o_hbm,
                       wstage, wb_ref, xbuf, obuf, wsem, xsem, osem, *, n_w, nj, bn):
    e = pl.program_id(0)
    j = pl.program_id(1)
    n_e = pl.num_programs(0)
    step = e * nj + j
    slot_w = step % 2
    first = first_ref[e]
    n = count_ref[e]
    cols = pl.ds(pl.multiple_of(j * bn, bn), bn)
    kc = w_hbm.shape[1] // W_CHUNKS

    def rows(blk):
        return pl.ds(pl.multiple_of(blk * MOE_ROWS, MOE_ROWS), MOE_ROWS)

    def x_copy(k, slot):
        return pltpu.make_async_copy(x_hbm.at[rows(first + k), :], xbuf.at[slot], xsem.at[slot])

    def o_copy(blk, slot):
        return pltpu.make_async_copy(obuf.at[slot], o_hbm.at[rows(blk), cols], osem.at[slot])

    def w_copies(e_, j_, slot, c):
        krows = pl.ds(pl.multiple_of(c * kc, kc), kc)
        return [pltpu.make_async_copy(
            w_hbm.at[e_, krows, pl.ds(pl.multiple_of((j_ + m * nj) * bn, bn), bn)],
            wstage.at[slot, m, krows, :], wsem.at[slot]) for m in range(n_w)]

    last_j = j == nj - 1
    e_next = jnp.where(last_j, e + 1, e)
    j_next = jnp.where(last_j, 0, j + 1)
    has_next = step + 1 < n_e * nj

    def fetch_next(c):
        for cp in w_copies(e_next, j_next, 1 - slot_w, c):
            cp.start()

    @pl.when(step == 0)
    def _():
        for c in range(W_CHUNKS):
            for cp in w_copies(e, j, slot_w, c):
                cp.start()

    @pl.when(n > 0)
    def _():
        x_copy(0, 0).start()

    for c in range(W_CHUNKS):
        for cp in w_copies(e, j, slot_w, c):
            cp.wait()

    @pl.when(n > 0)
    def _():
        for m in range(n_w):
            wb_ref[m] = wstage[slot_w, m].astype(BF16)

    def body(k, carry):
        slot = k % 2
        x_copy(k, slot).wait()

        @pl.when(k + 1 < n)
        def _():
            x_copy(k + 1, 1 - slot).start()

        @pl.when(jnp.logical_and(has_next, k < W_CHUNKS))
        def _():
            fetch_next(k)

        @pl.when(k >= 2)
        def _():
            o_copy(first + k - 2, slot).wait()

        x = xbuf[slot]
        acc = jnp.dot(x, wb_ref[0], preferred_element_type=F32)
        if n_w == 2:
            acc = _silu(acc) * jnp.dot(x, wb_ref[1], preferred_element_type=F32)
        obuf[slot] = acc.astype(obuf.dtype)
        o_copy(first + k, slot).start()
        return carry

    lax.fori_loop(0, n, body, 0)

    @pl.when(has_next)
    def _():
        def rest(c, carry):
            fetch_next(c)
            return carry

        lax.fori_loop(jnp.minimum(n, W_CHUNKS), W_CHUNKS, rest, 0)

    @pl.when(n >= 2)
    def _():
        o_copy(first + n - 2, n % 2).wait()

    @pl.when(n >= 1)
    def _():
        o_copy(first + n - 1, (n - 1) % 2).wait()

    @pl.when(e == n_e - 1)
    def _():
        obuf[0] = jnp.zeros(obuf.shape[1:], obuf.dtype)

        def fill(blk, carry):
            o_copy(blk, 0).start()
            o_copy(blk, 0).wait()
            return carry

        lax.fori_loop(used_ref[0], MOE_BLOCKS, fill, 0)


def moe_expert_matmul(xs, w, plan, *, swiglu, bn, out_dtype):
    first, count, used = plan
    r, k = xs.shape
    n_w = 2 if swiglu else 1
    f = w.shape[2] // n_w
    nj = f // bn
    assert f % bn == 0 and k % (8 * W_CHUNKS) == 0
    hbm = pl.BlockSpec(memory_space=pl.ANY)
    return pl.pallas_call(
        functools.partial(_moe_expert_kernel, n_w=n_w, nj=nj, bn=bn),
        grid_spec=pltpu.PrefetchScalarGridSpec(
            num_scalar_prefetch=3,
            grid=(N_EXPERTS, nj),
            in_specs=[hbm, hbm],
            out_specs=hbm,
            scratch_shapes=[pltpu.VMEM((2, n_w, k, bn), F32), pltpu.VMEM((n_w, k, bn), BF16),
                            pltpu.VMEM((2, MOE_ROWS, k), xs.dtype), pltpu.VMEM((2, MOE_ROWS, bn), out_dtype),
                            pltpu.SemaphoreType.DMA((2,)), pltpu.SemaphoreType.DMA((2,)),
                            pltpu.SemaphoreType.DMA((2,))]),
        out_shape=jax.ShapeDtypeStruct((r, f), out_dtype),
        compiler_params=_params("arbitrary", "arbitrary"),
        name="moe_gu" if swiglu else "moe_down",
    )(first, count, used, w, xs)


def _moe_combine_ln_kernel(pos_ref, nxt_ref, y_hbm, x_ref, gate_ref, g_ref, b_ref, o_ref, ob_ref, buf, sems):
    i = pl.program_id(0)
    slot = i % 2

    def issue(idx_ref, slot_):
        def body(r, carry):
            for t in range(MOE_TOPK):
                _row_copy(y_hbm, idx_ref[0, 0, MOE_TOPK * r + t], buf.at[slot_, t], r,
                          sems.at[slot_]).start(priority=t % 2)
            return carry
        lax.fori_loop(0, ROW_BLOCK, body, 0, unroll=GATHER_UNROLL)

    @pl.when(i == 0)
    def _():
        issue(pos_ref, slot)

    @pl.when(i + 1 < pl.num_programs(0))
    def _():
        issue(nxt_ref, 1 - slot)

    def wait(r, carry):
        for t in range(MOE_TOPK):
            _row_copy(y_hbm, 0, buf.at[slot, t], r, sems.at[slot]).wait()
        return carry

    lax.fori_loop(0, ROW_BLOCK, wait, 0, unroll=GATHER_UNROLL)
    gates = gate_ref[...]
    ff = buf[slot, 0] * gates[:, 0:1]
    for t in range(1, MOE_TOPK):
        ff = ff + buf[slot, t] * gates[:, t:t + 1]
    out = _layernorm_rows(DEEPNORM_ALPHA * x_ref[...] + ff, g_ref[...], b_ref[...])
    o_ref[...] = out
    ob_ref[...] = out.astype(BF16)


def moe_combine_ln(x, ys, pos, gates, g, b):
    m, d = x.shape
    nblk = m // ROW_BLOCK
    row = pl.BlockSpec((ROW_BLOCK, d), lambda i: (i, 0))
    vec = pl.BlockSpec((1, d), lambda i: (0, 0))
    pos3 = pos.reshape(nblk, 1, MOE_TOPK * ROW_BLOCK)
    idx_spec = lambda step: pl.BlockSpec((1, 1, MOE_TOPK * ROW_BLOCK),
                                         lambda i: (jnp.minimum(i + step, nblk - 1), 0, 0), memory_space=pltpu.SMEM)
    return pl.pallas_call(
        _moe_combine_ln_kernel,
        grid=(nblk,),
        in_specs=[idx_spec(0), idx_spec(1), pl.BlockSpec(memory_space=pl.ANY),
                  row, pl.BlockSpec((ROW_BLOCK, LANES), lambda i: (i, 0)), vec, vec],
        out_specs=[row, row],
        out_shape=[jax.ShapeDtypeStruct((m, d), F32), jax.ShapeDtypeStruct((m, d), BF16)],
        scratch_shapes=[pltpu.VMEM((2, MOE_TOPK, ROW_BLOCK, d), F32), pltpu.SemaphoreType.DMA((2,))],
        compiler_params=_params("arbitrary"),
        name="moe_combine_ln",
    )(pos3, pos3, ys, x, gates, g.reshape(1, d), b.reshape(1, d))


def moe_plan(idx):
    e = idx[:N_TOK, :MOE_TOPK].reshape(N_ASSIGN)
    onehot = (e[:, None] == jnp.arange(N_EXPERTS, dtype=I32)[None, :]).astype(I32)
    seg = 64
    assert N_ASSIGN % seg == 0 and seg <= 256
    oh = onehot.astype(F32).reshape(N_ASSIGN // seg, seg, N_EXPERTS)
    tri = lambda n, k: jnp.tril(jnp.ones((n, n), F32), k)
    within = jnp.einsum("ts,rse->rte", tri(seg, -1), oh)
    offset = jnp.einsum("rq,qe->re", tri(N_ASSIGN // seg, -1), jnp.sum(oh, axis=1))
    before = (within + offset[:, None, :]).reshape(N_ASSIGN, N_EXPERTS).astype(I32)
    rank = jnp.sum(before * onehot, axis=1)
    counts = jnp.sum(onehot, axis=0)
    padded = (counts + MOE_ROWS - 1) // MOE_ROWS * MOE_ROWS
    pad_end = jnp.cumsum(padded)
    pad_start = pad_end - padded
    dest = pad_start[e] + rank
    tok = jnp.arange(N_ASSIGN, dtype=I32) // MOE_TOPK
    buf_tok = jnp.zeros((MOE_BUF,), I32).at[dest].set(tok, unique_indices=True)
    blocks = ((pad_start // MOE_ROWS).astype(I32), (padded // MOE_ROWS).astype(I32),
              (pad_end[-1:] // MOE_ROWS).astype(I32))
    pos = jnp.zeros((TOK_PAD * MOE_TOPK,), I32).at[:N_ASSIGN].set(dest)
    return buf_tok, blocks, pos


def _rope_tables():
    half = ROT_DIM // 2
    inv = ROPE_THETA ** (-jnp.arange(half, dtype=F32) * 2.0 / ROT_DIM)
    pos = jnp.concatenate([jnp.arange(SEQ, dtype=I32), jnp.full((TOK_PAD - SEQ,), PAST_LEN, I32)])
    ang = pos.astype(F32)[:, None] * inv[None, :]
    cos, sin = jnp.cos(ang), jnp.sin(ang)
    rest = LANES - ROT_DIM
    cf = jnp.concatenate([cos, cos, jnp.ones((TOK_PAD, rest), F32)], axis=1)
    s1 = jnp.concatenate([-sin, jnp.zeros((TOK_PAD, LANES - half), F32)], axis=1)
    s2 = jnp.concatenate([jnp.zeros((TOK_PAD, half), F32), sin, jnp.zeros((TOK_PAD, rest), F32)], axis=1)
    return cf, s1, s2


def _tokens(prompt, sample):
    pad = jnp.zeros((TOK_PAD - N_TOK, prompt.shape[1]), prompt.dtype)
    return jnp.concatenate([prompt, sample, pad], axis=0)


def _put_sample(buf, rows):
    tail = jnp.zeros((TOK_PAD - SEQ, buf.shape[1]), buf.dtype).at[:rows.shape[0]].set(rows.astype(buf.dtype))
    return lax.dynamic_update_slice(buf, tail, (SEQ, 0))


def kernel(x_prompt, x_sample, state_ssm, state_conv, cache_k, cache_v, page_table, p_prompt, p_sample, ln_g, ln_b, ssm_w_in, ssm_conv_w, ssm_conv_b, ssm_dt_bias, ssm_A_log, ssm_D, ssm_norm_w, ssm_w_out, attn_w_kv, attn_w_q, attn_w_o, ffn_w_gu, ffn_w_down, moe_w_router, moe_b_router, moe_w_gu, moe_w_down, ple_w_gate, ple_b_gate, ple_w_proj):
    x0b = _tokens(x_prompt[0].astype(BF16), x_sample[:, 0].astype(BF16))
    x0_tail = jnp.zeros((TOK_PAD - SEQ, D_MODEL), F32).at[:DEC_BATCH].set(x_sample[:, 0])
    p_tok = [_tokens(p_prompt[i, 0], p_sample[i, :, 0]).astype(BF16) for i in range(2)]

    w_in_t = jnp.swapaxes(ssm_w_in, 1, 2)[0]
    zx_cols = SSM_D_INNER + SSM_CONV_DIM
    zx = matmul(x0b, w_in_t, bn=1024, ncols=zx_cols, w_transposed=True)
    dt_raw = matmul(x0b, w_in_t, bn=LANES, col0=zx_cols, ncols=LANES, w_transposed=True)[:, :SSM_N_HEADS]

    st_conv = state_conv[0]
    x_raw_s = zx[SEQ:N_TOK, SSM_D_INNER:]
    xbc_s = conv_sample(jnp.transpose(st_conv, (1, 0, 2)), x_raw_s, ssm_conv_w[0], ssm_conv_b[0])
    conv_prompt_out = zx[SEQ - (SSM_CONV_W - 1):SEQ, SSM_D_INNER:].reshape(1, 1, SSM_CONV_W - 1, SSM_CONV_DIM)
    conv_sample_out = jnp.concatenate([st_conv[:, 1:], x_raw_s[:, None, :]], axis=1)[None]

    d_lanes = jnp.repeat(ssm_D[0], SSM_HEAD_DIM).reshape(1, SSM_D_INNER)
    gated, h_prompt = ssm_prompt(zx, dt_raw[:SEQ], ssm_conv_w[0], ssm_conv_b[0], ssm_dt_bias[0], ssm_A_log[0],
                                 d_lanes, ssm_norm_w[0])
    xs_s = xbc_s[:DEC_BATCH]
    gn = SSM_N_GROUPS * SSM_D_STATE
    h_sample, y_s = ssd_sample(
        state_ssm[0].reshape(DEC_BATCH, SSM_N_HEADS * SSM_HEAD_DIM, SSM_D_STATE),
        xs_s[:, :SSM_D_INNER],
        xs_s[:, SSM_D_INNER:SSM_D_INNER + gn].reshape(DEC_BATCH, SSM_N_GROUPS, SSM_D_STATE),
        xs_s[:, SSM_D_INNER + gn:].reshape(DEC_BATCH, SSM_N_GROUPS, SSM_D_STATE),
        dt_raw[SEQ:N_TOK], ssm_dt_bias[0], ssm_A_log[0], d_lanes)
    y_tail = jnp.zeros((TOK_PAD - SEQ, SSM_D_INNER), F32).at[:DEC_BATCH].set(y_s.reshape(DEC_BATCH, SSM_D_INNER))
    gated = lax.dynamic_update_slice(gated, gate_norm(y_tail, zx[SEQ:], ssm_norm_w[0]), (SEQ, 0))
    mix = matmul(gated, ssm_w_out[0], bn=512)
    x1, x1b = add_ln(x_prompt[0], mix, ln_g[0, 0], ln_b[0, 0], x_tail=x0_tail)

    hid = matmul_swiglu(x1b, ffn_w_gu[0], bn=512)
    ff = matmul(hid, ffn_w_down[0], bn=512)
    x2, x2b = add_ln(x1, ff, ln_g[0, 1], ln_b[0, 1])
    x3, x3b = ple_mix(x2, x2b, p_tok[0], ple_w_gate, ple_b_gate, ple_w_proj, 0)

    rope = _rope_tables()
    k_all = matmul(x3b, attn_w_kv, bn=KV_COLS, ncols=KV_COLS, rope=rope)
    v_all = matmul(x3b, attn_w_kv, bn=KV_COLS, col0=KV_COLS, ncols=KV_COLS)
    q_all = matmul(x3b, attn_w_q[0], bn=1024, rope=rope)
    att = moba_prompt(q_all, k_all, v_all, block_means(k_all).reshape(SEQ // MOBA_BLOCK, KV_COLS))

    q_s = q_all[SEQ:N_TOK].reshape(DEC_BATCH, ATTN_N_HEADS, ATTN_HEAD_DIM)
    k_s = k_all[SEQ:N_TOK].reshape(DEC_BATCH, ATTN_N_KV_HEADS, ATTN_HEAD_DIM)
    v_s = v_all[SEQ:N_TOK].reshape(DEC_BATCH, ATTN_N_KV_HEADS, ATTN_HEAD_DIM)
    n_pool = cache_k.shape[0]
    sel = sample_select(page_table, q_s, cache_k.reshape(n_pool, PAGE_ROWS, ATTN_HEAD_DIM))[:, :, :MOBA_TOPK]
    att_s = sample_attend(page_table, sel, q_s, k_s, v_s, cache_k, cache_v)
    att = _put_sample(att, att_s.reshape(DEC_BATCH, D_MODEL))
    mix = matmul(att, attn_w_o[0], bn=1024)
    x4, x4b = add_ln(x3, mix, ln_g[1, 0], ln_b[1, 0])

    idx, gates = router(x4b, moe_w_router[0], moe_b_router[0])
    buf_tok, blocks, pos = moe_plan(idx)
    xs = moe_gather(x4, buf_tok)
    hs = moe_expert_matmul(xs, moe_w_gu[0], blocks, swiglu=True, bn=1024, out_dtype=BF16)
    ys = moe_expert_matmul(hs, moe_w_down[0], blocks, swiglu=False, bn=512, out_dtype=F32)
    x5, x5b = moe_combine_ln(x4, ys, pos, gates, ln_g[1, 1], ln_b[1, 1])
    y_prompt, y_sample = ple_mix(x5, x5b, p_tok[1], ple_w_gate, ple_b_gate, ple_w_proj, 1, final=True)

    state_shape = (1, -1, SSM_N_HEADS, SSM_HEAD_DIM, SSM_D_STATE)
    kv_shape = (-1, ATTN_N_KV_HEADS, ATTN_HEAD_DIM)
    return (y_prompt.reshape(1, SEQ, D_MODEL),
            y_sample.reshape(DEC_BATCH, 1, D_MODEL),
            h_prompt.reshape(state_shape),
            conv_prompt_out,
            k_all[:SEQ].reshape((1,) + (SEQ,) + kv_shape[1:]),
            v_all[:SEQ].reshape((1,) + (SEQ,) + kv_shape[1:]),
            h_sample.reshape(state_shape),
            conv_sample_out,
            k_all[SEQ:N_TOK].reshape((DEC_BATCH, 1) + kv_shape[1:]),
            v_all[SEQ:N_TOK].reshape((DEC_BATCH, 1) + kv_shape[1:]))
```

```python
import functools

import jax
import jax.numpy as jnp
import numpy as np
from jax import lax
from jax.experimental import pallas as pl
from jax.experimental.pallas import tpu as pltpu

F32 = jnp.float32
BF16 = jnp.bfloat16
I32 = jnp.int32

D_MODEL = 2048
SEQ = 8192
DEC_BATCH = 32
PAST_LEN = 8192
PAGE_SIZE = 128

SSM_D_INNER = 4096
SSM_HEAD_DIM = 64
SSM_N_HEADS = 64
SSM_N_GROUPS = 8
SSM_D_STATE = 128
SSM_CONV_W = 4
SSM_CHUNK = 128
SSM_CONV_DIM = 6144
SSM_HEADS_PER_GROUP = SSM_N_HEADS // SSM_N_GROUPS

ATTN_HEAD_DIM = 128
ATTN_N_HEADS = 16
ATTN_N_KV_HEADS = 4
ATTN_GROUP = 4
ROT_DIM = 32
ROPE_THETA = 500000.0
MOBA_BLOCK = 256
MOBA_TOPK = 3
N_PAST_BLOCKS = PAST_LEN // MOBA_BLOCK
PAGES_PER_BLOCK = MOBA_BLOCK // PAGE_SIZE
N_PAGES = PAST_LEN // PAGE_SIZE

D_FF = 5632
N_EXPERTS = 8
MOE_TOPK = 2
D_FF_EXPERT = 7168
D_PLE = 256
DEEPNORM_ALPHA = 4.0 ** 0.25
LN_EPS = 1e-5
RMS_EPS = 1e-5

LANES = 128
VMEM_LIMIT_BYTES = 56 * 1024 * 1024

N_TOK = SEQ + DEC_BATCH
TOK_PAD = 8448
MM_ROWS = 768
ROW_BLOCK = 256
MOE_ROWS = 256
N_ASSIGN = N_TOK * MOE_TOPK
MOE_BLOCKS = -(-N_ASSIGN // MOE_ROWS) + N_EXPERTS
MOE_BUF = MOE_BLOCKS * MOE_ROWS
NEG_INF = float("-inf")


def _params(*sem):
    return pltpu.CompilerParams(dimension_semantics=sem, vmem_limit_bytes=VMEM_LIMIT_BYTES)


def _split3(a):
    hi = a.astype(BF16)
    r1 = a - hi.astype(F32)
    mid = r1.astype(BF16)
    lo = (r1 - mid.astype(F32)).astype(BF16)
    return hi, mid, lo


def _dot_f32(a, b, dims):
    a0, a1, a2 = _split3(a)
    b0, b1, b2 = _split3(b)
    dn = (dims, ((), ()))
    d = lambda x, y: lax.dot_general(x, y, dn, preferred_element_type=F32)
    return (d(a0, b0) + (d(a0, b1) + d(a1, b0))) + ((d(a0, b2) + d(a2, b0)) + d(a1, b1))


def _softplus(x):
    return jnp.maximum(x, 0.0) + jnp.log1p(jnp.exp(-jnp.abs(x)))


def _silu(x):
    return x * jax.nn.sigmoid(x)


def _rope_lanes(a, cf, s1, s2):
    return a * cf + pltpu.roll(a, LANES - ROT_DIM // 2, 1) * s1 + pltpu.roll(a, ROT_DIM // 2, 1) * s2


def _mm_kernel(x_ref, w_ref, *rest, rope, valid_cols, w_transposed):
    if rope:
        cf_ref, s1_ref, s2_ref, o_ref, wb_ref = rest
    else:
        o_ref, wb_ref = rest

    @pl.when(pl.program_id(1) == 0)
    def _():
        w = w_ref[...]
        if valid_cols is not None:
            axis = 0 if w_transposed else 1
            w = jnp.where(lax.broadcasted_iota(I32, w.shape, axis) < valid_cols, w, 0.0)
        wb_ref[...] = (w.T if w_transposed else w).astype(BF16)

    acc = jnp.dot(x_ref[...], wb_ref[...], preferred_element_type=F32)
    if rope:
        cf, s1, s2 = cf_ref[...], s1_ref[...], s2_ref[...]
        for c in range(acc.shape[1] // LANES):
            sl = slice(c * LANES, (c + 1) * LANES)
            o_ref[:, sl] = _rope_lanes(acc[:, sl], cf, s1, s2).astype(o_ref.dtype)
    else:
        o_ref[...] = acc.astype(o_ref.dtype)


def matmul(x, w, *, bn, col0=0, ncols=None, rope=None, out_dtype=F32, bm=MM_ROWS, w_transposed=False):
    m, k = x.shape
    n_total = w.shape[0] if w_transposed else w.shape[1]
    ncols = n_total if ncols is None else ncols
    assert m % bm == 0 and ncols % bn == 0 and col0 % bn == 0
    j0 = col0 // bn
    valid_cols = None
    if col0 + ncols > n_total:
        assert ncols == bn
        valid_cols = n_total - col0
    w_spec = (pl.BlockSpec((bn, k), lambda j, i: (j + j0, 0)) if w_transposed
              else pl.BlockSpec((k, bn), lambda j, i: (0, j + j0)))
    in_specs = [pl.BlockSpec((bm, k), lambda j, i: (i, 0)), w_spec]
    args = [x, w]
    if rope is not None:
        in_specs += [pl.BlockSpec((bm, LANES), lambda j, i: (i, 0))] * 3
        args += list(rope)
    return pl.pallas_call(
        functools.partial(_mm_kernel, rope=rope is not None, valid_cols=valid_cols, w_transposed=w_transposed),
        grid=(ncols // bn, m // bm),
        in_specs=in_specs,
        out_specs=pl.BlockSpec((bm, bn), lambda j, i: (i, j)),
        out_shape=jax.ShapeDtypeStruct((m, ncols), out_dtype),
        scratch_shapes=[pltpu.VMEM((k, bn), BF16)],
        compiler_params=_params("arbitrary", "arbitrary"),
        name="matmul",
    )(*args)


def _mm_swiglu_kernel(x_ref, wg_ref, wu_ref, o_ref, wgb_ref, wub_ref):
    @pl.when(pl.program_id(1) == 0)
    def _():
        wgb_ref[...] = wg_ref[...].astype(BF16)
        wub_ref[...] = wu_ref[...].astype(BF16)

    x = x_ref[...]
    g = jnp.dot(x, wgb_ref[...], preferred_element_type=F32)
    u = jnp.dot(x, wub_ref[...], preferred_element_type=F32)
    o_ref[...] = (_silu(g) * u).astype(o_ref.dtype)


def matmul_swiglu(x, w_gu, *, bn, bm=MM_ROWS):
    m, k = x.shape
    f = w_gu.shape[1] // 2
    assert m % bm == 0 and f % bn == 0
    nj = f // bn
    return pl.pallas_call(
        _mm_swiglu_kernel,
        grid=(nj, m // bm),
        in_specs=[pl.BlockSpec((bm, k), lambda j, i: (i, 0)),
                  pl.BlockSpec((k, bn), lambda j, i: (0, j)),
                  pl.BlockSpec((k, bn), lambda j, i: (0, j + nj))],
        out_specs=pl.BlockSpec((bm, bn), lambda j, i: (i, j)),
        out_shape=jax.ShapeDtypeStruct((m, f), BF16),
        scratch_shapes=[pltpu.VMEM((k, bn), BF16), pltpu.VMEM((k, bn), BF16)],
        compiler_params=_params("arbitrary", "arbitrary"),
        name="matmul_swiglu",
    )(x, w_gu, w_gu)


def _ple_kernel(xb_ref, pb_ref, wg_ref, wp_ref, bg_ref, x_ref, o_ref, o2_ref, wgb_ref, wpb_ref, *, final):
    @pl.when(pl.program_id(1) == 0)
    def _():
        wgb_ref[...] = wg_ref[...].astype(BF16)
        wpb_ref[...] = wp_ref[...].astype(BF16)

    gate = jax.nn.sigmoid(jnp.dot(xb_ref[...], wgb_ref[...], preferred_element_type=F32) + bg_ref[...])
    proj = jnp.dot(pb_ref[...], wpb_ref[...], preferred_element_type=F32)
    out = x_ref[...] + gate * proj
    o_ref[...] = out
    if final:
        tail0 = SEQ - (TOK_PAD - MM_ROWS)

        @pl.when(pl.program_id(1) == pl.num_programs(1) - 1)
        def _():
            o2_ref[...] = out[tail0:tail0 + DEC_BATCH, :]
    else:
        o2_ref[...] = out.astype(BF16)


def ple_mix(x, xb, pb, w_gate, b_gate, w_proj, layer, *, final=False, bn=1024, bm=MM_ROWS):
    m, d = x.shape
    kp = pb.shape[1]
    full = pl.BlockSpec((bm, bn), lambda j, i: (i, j))
    if final:
        assert m == TOK_PAD and bm == MM_ROWS and 0 <= SEQ - (TOK_PAD - MM_ROWS) <= MM_ROWS - DEC_BATCH
        out_specs = [full, pl.BlockSpec((DEC_BATCH, bn), lambda j, i: (0, j))]
        out_shape = [jax.ShapeDtypeStruct((SEQ, d), F32), jax.ShapeDtypeStruct((DEC_BATCH, d), F32)]
    else:
        out_specs = [full, full]
        out_shape = [jax.ShapeDtypeStruct((m, d), F32), jax.ShapeDtypeStruct((m, d), BF16)]
    return pl.pallas_call(
        functools.partial(_ple_kernel, final=final),
        grid=(d // bn, m // bm),
        in_specs=[pl.BlockSpec((bm, d), lambda j, i: (i, 0)),
                  pl.BlockSpec((bm, kp), lambda j, i: (i, 0)),
                  pl.BlockSpec((None, d, bn), lambda j, i: (layer, 0, j)),
                  pl.BlockSpec((None, kp, bn), lambda j, i: (layer, 0, j)),
                  pl.BlockSpec((None, 1, bn), lambda j, i: (layer, 0, j)),
                  full],
        out_specs=out_specs,
        out_shape=out_shape,
        scratch_shapes=[pltpu.VMEM((d, bn), BF16), pltpu.VMEM((kp, bn), BF16)],
        compiler_params=_params("arbitrary", "arbitrary"),
        name="ple_mix",
    )(xb, pb, w_gate, w_proj, b_gate.reshape(b_gate.shape[0], 1, d), x)


def _layernorm_rows(v, g, b):
    mu = jnp.mean(v, axis=-1, keepdims=True)
    c = v - mu
    var = jnp.mean(c * c, axis=-1, keepdims=True)
    return c * lax.rsqrt(var + LN_EPS) * g + b


def _add_ln_kernel(x_ref, xt_ref, m_ref, g_ref, b_ref, o_ref, ob_ref, *, n_main):
    x = x_ref[...]
    if n_main is not None:
        x = jnp.where(pl.program_id(0) < n_main, x, xt_ref[...])
    out = _layernorm_rows(DEEPNORM_ALPHA * x + m_ref[...], g_ref[...], b_ref[...])
    o_ref[...] = out
    ob_ref[...] = out.astype(BF16)


def add_ln(x, mix, g, b, x_tail=None):
    m, d = mix.shape
    row = pl.BlockSpec((ROW_BLOCK, d), lambda i: (i, 0))
    vec = pl.BlockSpec((1, d), lambda i: (0, 0))
    if x_tail is None:
        n_main, x_spec, x_tail = None, row, x
    else:
        n_main = x.shape[0] // ROW_BLOCK
        assert x_tail.shape[0] == ROW_BLOCK and n_main + 1 == m // ROW_BLOCK
        x_spec = pl.BlockSpec((ROW_BLOCK, d), lambda i: (jnp.minimum(i, n_main - 1), 0))
    return pl.pallas_call(
        functools.partial(_add_ln_kernel, n_main=n_main),
        grid=(m // ROW_BLOCK,),
        in_specs=[x_spec, pl.BlockSpec((ROW_BLOCK, d), lambda i: (0, 0)), row, vec, vec],
        out_specs=[row, row],
        out_shape=[jax.ShapeDtypeStruct((m, d), F32), jax.ShapeDtypeStruct((m, d), BF16)],
        compiler_params=_params("arbitrary"),
        name="add_ln",
    )(x, x_tail, mix, g.reshape(1, d), b.reshape(1, d))


def _gate_norm_kernel(y_ref, z_ref, w_ref, o_ref):
    g = y_ref[...] * _silu(z_ref[...])
    ms = jnp.mean(g * g, axis=-1, keepdims=True)
    o_ref[...] = (g * lax.rsqrt(ms + RMS_EPS) * w_ref[...]).astype(o_ref.dtype)


def gate_norm(y, zx, norm_w):
    m, di = y.shape
    row = pl.BlockSpec((ROW_BLOCK, di), lambda i: (i, 0))
    return pl.pallas_call(
        _gate_norm_kernel,
        grid=(m // ROW_BLOCK,),
        in_specs=[row, row, pl.BlockSpec((1, di), lambda i: (0, 0))],
        out_specs=row,
        out_shape=jax.ShapeDtypeStruct((m, di), BF16),
        compiler_params=_params("arbitrary"),
        name="gate_norm",
    )(y, zx, norm_w.reshape(1, di))


CONV_COLS = 2048


def _causal_conv_silu(x, halo, w, b):
    acc = b + x * w[SSM_CONV_W - 1:SSM_CONV_W, :]
    rows8 = lax.broadcasted_iota(I32, halo.shape, 0)
    for s in range(1, SSM_CONV_W):
        xr = pltpu.roll(x, s, 0)
        head = jnp.where(rows8 < s, pltpu.roll(halo, s, 0), xr[:8, :])
        shifted = jnp.concatenate([head, xr[8:, :]], axis=0)
        acc = acc + shifted * w[SSM_CONV_W - 1 - s:SSM_CONV_W - s, :]
    return _silu(acc)


def _conv_sample_kernel(st_ref, x_ref, w_ref, b_ref, o_ref):
    w = w_ref[...]
    acc = b_ref[...] + x_ref[...] * w[SSM_CONV_W - 1:SSM_CONV_W, :]
    for j in range(SSM_CONV_W - 1):
        acc = acc + st_ref[j] * w[j:j + 1, :]
    o_ref[...] = jnp.zeros(o_ref.shape, F32)
    o_ref[:DEC_BATCH, :] = _silu(acc)


def conv_sample(st, x_raw, conv_w, conv_b):
    return pl.pallas_call(
        _conv_sample_kernel,
        grid=(SSM_CONV_DIM // CONV_COLS,),
        in_specs=[pl.BlockSpec((SSM_CONV_W - 1, DEC_BATCH, CONV_COLS), lambda k: (0, 0, k)),
                  pl.BlockSpec((DEC_BATCH, CONV_COLS), lambda k: (0, k)),
                  pl.BlockSpec((SSM_CONV_W, CONV_COLS), lambda k: (0, k)),
                  pl.BlockSpec((1, CONV_COLS), lambda k: (0, k))],
        out_specs=pl.BlockSpec((TOK_PAD - SEQ, CONV_COLS), lambda k: (0, k)),
        out_shape=jax.ShapeDtypeStruct((TOK_PAD - SEQ, SSM_CONV_DIM), F32),
        compiler_params=_params("arbitrary"),
        name="conv_sample",
    )(st, x_raw, conv_w, conv_b.reshape(1, SSM_CONV_DIM))


SSD_GROUP_COLS = SSM_HEADS_PER_GROUP * SSM_HEAD_DIM
SSD_PAIR_ROWS = 2 * SSM_HEAD_DIM


N_CONV_BLOCKS = SSM_CONV_DIM // CONV_COLS


def _ssm_prompt_kernel(*refs):
    g_ref = refs[-4]

    @pl.when(pl.program_id(0) < SEQ // SSM_CHUNK)
    def _():
        _ssm_prompt_chunk(*refs)

    @pl.when(pl.program_id(0) >= SEQ // SSM_CHUNK)
    def _():
        g_ref[...] = jnp.zeros(g_ref.shape, g_ref.dtype)


def _ssm_prompt_chunk(*refs):
    nb = N_CONV_BLOCKS
    raw, halo = refs[:nb], refs[nb:2 * nb]
    (cw_ref, cb_ref, z_ref, dtc_ref, dtr_ref, bc_ref, br_ref, ac_ref, ar_ref, d_ref, nw_ref,
     g_ref, h_ref, xa_ref, y_ref) = refs[2 * nb:]
    first = pl.program_id(0) == 0
    step = 2 * LANES
    for k in range(nb):
        for c in range(0, CONV_COLS, step):
            src = slice(c, c + step)
            dst = slice(k * CONV_COLS + c, k * CONV_COLS + c + step)
            xa_ref[:, dst] = _causal_conv_silu(raw[k][:, src], jnp.where(first, 0.0, halo[k][:, src]),
                                               cw_ref[:, dst], cb_ref[:, dst])
    b0 = SSM_D_INNER
    c0 = SSM_D_INNER + SSM_N_GROUPS * SSM_D_STATE
    q = SSM_CHUNK
    dt_c = _softplus(dtc_ref[...] + bc_ref[...])
    dt_r = _softplus(dtr_ref[...] + br_ref[...])
    da_c = dt_c * -jnp.exp(ac_ref[...])
    da_r = dt_r * -jnp.exp(ar_ref[...])
    ti = lax.broadcasted_iota(I32, (q, q), 0)
    si = lax.broadcasted_iota(I32, (q, q), 1)
    causal = ti >= si
    tril = causal.astype(F32)
    triu = (ti <= si).astype(F32)
    acs_c = _dot_f32(tril, da_c, ((1,), (0,)))
    acs_r = _dot_f32(da_r, triu, ((1,), (0,)))
    last = acs_c[q - 1:q, :]

    @pl.when(pl.program_id(0) == 0)
    def _():
        h_ref[...] = jnp.zeros(h_ref.shape, F32)

    lane = lax.broadcasted_iota(I32, (q, LANES), 1)
    lo = lane < SSM_HEAD_DIM
    sub = lax.broadcasted_iota(I32, (SSD_PAIR_ROWS, 1), 0)
    for g in range(SSM_N_GROUPS):
        bm = xa_ref[:, b0 + g * SSM_D_STATE:b0 + (g + 1) * SSM_D_STATE].astype(BF16)
        cm = xa_ref[:, c0 + g * SSM_D_STATE:c0 + (g + 1) * SSM_D_STATE].astype(BF16)
        cb = lax.dot_general(cm, bm, (((1,), (1,)), ((), ())), preferred_element_type=F32)
        for pair in range(SSM_HEADS_PER_GROUP // 2):
            r1 = g * SSM_HEADS_PER_GROUP + 2 * pair
            r2 = r1 + 1
            cols = slice(r1 * SSM_HEAD_DIM, (r2 + 1) * SSM_HEAD_DIM)
            x = xa_ref[:, cols]
            xdt = x * jnp.where(lo, dt_c[:, r1:r1 + 1], dt_c[:, r2:r2 + 1])
            ms = []
            for r in (r1, r2):
                seg = acs_c[:, r:r + 1] - acs_r[r:r + 1, :]
                ms.append((cb * jnp.exp(jnp.where(causal, seg, NEG_INF))).astype(BF16))
            y = (jnp.dot(ms[0], jnp.where(lo, xdt, 0.0).astype(BF16), preferred_element_type=F32)
                 + jnp.dot(ms[1], jnp.where(lo, 0.0, xdt).astype(BF16), preferred_element_type=F32))

            h_in = h_ref[cols, :]
            y_off = lax.dot_general(cm, h_in.astype(BF16), (((1,), (1,)), ((), ())),
                                    preferred_element_type=F32)
            y = y + y_off * jnp.exp(jnp.where(lo, acs_c[:, r1:r1 + 1], acs_c[:, r2:r2 + 1]))
            y_ref[:, cols] = y + d_ref[:, cols] * x

            to_end = jnp.exp(jnp.where(lo, last[:, r1:r1 + 1] - acs_c[:, r1:r1 + 1],
                                       last[:, r2:r2 + 1] - acs_c[:, r2:r2 + 1]))
            xw_t = (xdt * to_end).T.astype(BF16)
            st = jnp.dot(xw_t, bm, preferred_element_type=F32)
            decay = jnp.exp(jnp.where(sub < SSM_HEAD_DIM, last[:, r1:r1 + 1], last[:, r2:r2 + 1]))
            h_ref[cols, :] = h_in * decay + st

    gated = y_ref[...] * _silu(z_ref[...])
    ms = jnp.mean(gated * gated, axis=-1, keepdims=True)
    g_ref[...] = (gated * lax.rsqrt(ms + RMS_EPS) * nw_ref[...]).astype(g_ref.dtype)


def ssm_prompt(zx, dt_raw, conv_w, conv_b, dt_bias, a_log, d_lanes, norm_w):
    nc = SEQ // SSM_CHUNK
    nh = SSM_N_HEADS
    cb0 = SSM_D_INNER // CONV_COLS
    halo_rows = SSM_CHUNK // 8
    clamp = lambda c: jnp.minimum(c, nc - 1)
    const = lambda shape: pl.BlockSpec(shape, lambda c: (0,) * len(shape))
    raw_specs = [pl.BlockSpec((SSM_CHUNK, CONV_COLS), functools.partial(lambda k, c: (c, cb0 + k), k))
                 for k in range(N_CONV_BLOCKS)]
    halo_specs = [pl.BlockSpec((8, CONV_COLS),
                               functools.partial(lambda k, c: (jnp.maximum(c * halo_rows - 1, 0), cb0 + k), k))
                  for k in range(N_CONV_BLOCKS)]
    return pl.pallas_call(
        _ssm_prompt_kernel,
        grid=(TOK_PAD // SSM_CHUNK,),
        in_specs=raw_specs + halo_specs + [
            const((SSM_CONV_W, SSM_CONV_DIM)), const((1, SSM_CONV_DIM)),
            pl.BlockSpec((SSM_CHUNK, SSM_D_INNER), lambda c: (c, 0)),
            pl.BlockSpec((SSM_CHUNK, nh), lambda c: (clamp(c), 0)),
            pl.BlockSpec((nh, SSM_CHUNK), lambda c: (0, clamp(c))),
            const((1, nh)), const((nh, 1)), const((1, nh)), const((nh, 1)),
            const((1, SSM_D_INNER)), const((1, SSM_D_INNER))],
        out_specs=[pl.BlockSpec((SSM_CHUNK, SSM_D_INNER), lambda c: (c, 0)),
                   const((nh * SSM_HEAD_DIM, SSM_D_STATE))],
        out_shape=[jax.ShapeDtypeStruct((TOK_PAD, SSM_D_INNER), BF16),
                   jax.ShapeDtypeStruct((nh * SSM_HEAD_DIM, SSM_D_STATE), F32)],
        scratch_shapes=[pltpu.VMEM((SSM_CHUNK, SSM_CONV_DIM), F32), pltpu.VMEM((SSM_CHUNK, SSM_D_INNER), F32)],
        compiler_params=_params("arbitrary"),
        name="ssm_prompt",
    )(*([zx] * (2 * N_CONV_BLOCKS)), conv_w, conv_b.reshape(1, SSM_CONV_DIM), zx, dt_raw, dt_raw.T,
      dt_bias.reshape(1, nh), dt_bias.reshape(nh, 1), a_log.reshape(1, nh), a_log.reshape(nh, 1),
      d_lanes, norm_w.reshape(1, SSM_D_INNER))


def _mxu_round(x):
    return x.astype(BF16).astype(F32)


def _ssd_sample_kernel(h0_ref, xt_ref, xl_ref, b_ref, c_ref, dt_ref, dtl_ref, bias_ref, biasl_ref,
                       alog_ref, alogl_ref, d_ref, h_ref, y_ref):
    dt = _softplus(dt_ref[...] + bias_ref[...])
    dec = jnp.exp(dt * -jnp.exp(alog_ref[...]))
    dt_l = _softplus(dtl_ref[...] + biasl_ref[...])
    dec_l = jnp.exp(dt_l * -jnp.exp(alogl_ref[...]))
    xdt_t = _mxu_round(xt_ref[...] * dt)
    xdt_l = _mxu_round(xl_ref[...] * dt_l)
    bm = _mxu_round(b_ref[...])
    cm = _mxu_round(c_ref[...])
    for grp in range(SSM_N_GROUPS):
        b_row = bm[grp:grp + 1, :]
        c_row = cm[grp:grp + 1, :]
        cb = _mxu_round(jnp.sum(c_row * b_row, axis=1, keepdims=True))
        old = []
        for r in range(SSM_HEADS_PER_GROUP):
            h = grp * SSM_HEADS_PER_GROUP + r
            rows = slice(h * SSM_HEAD_DIM, (h + 1) * SSM_HEAD_DIM)
            h0 = h0_ref[rows, :]
            h_ref[rows, :] = h0 * dec[:, h:h + 1] + xdt_t[:, h:h + 1] * b_row
            old.append(h0)
        h0_grp = jnp.concatenate(old, axis=0).astype(BF16)
        c_rows = jnp.broadcast_to(c_row, (8, SSM_D_STATE)).astype(BF16)
        ch0 = lax.dot_general(c_rows, h0_grp, (((1,), (1,)), ((), ())), preferred_element_type=F32)[0:1, :]
        cols = slice(grp * SSD_GROUP_COLS, (grp + 1) * SSD_GROUP_COLS)
        y_ref[:, cols] = (cb * xdt_l[:, cols] + ch0 * dec_l[:, cols]) + d_ref[:, cols] * xl_ref[:, cols]


def ssd_sample(h0, x, bm, cm, dt_raw, dt_bias, a_log, d_lanes):
    nb = h0.shape[0]
    nh, p = SSM_N_HEADS, SSM_HEAD_DIM
    hp = nh * p
    lanes = lambda v: jnp.repeat(v, p, axis=-1)
    x_t = jnp.transpose(x.reshape(nb, nh, p), (0, 2, 1))
    per_b = lambda *shape: pl.BlockSpec((None,) + shape, lambda b: (b,) + (0,) * len(shape))
    vec = lambda n: pl.BlockSpec((1, n), lambda b: (0, 0))
    return pl.pallas_call(
        _ssd_sample_kernel,
        grid=(nb,),
        in_specs=[per_b(hp, SSM_D_STATE), per_b(p, nh), per_b(1, hp),
                  per_b(SSM_N_GROUPS, SSM_D_STATE), per_b(SSM_N_GROUPS, SSM_D_STATE),
                  per_b(1, nh), per_b(1, hp), vec(nh), vec(hp), vec(nh), vec(hp), vec(hp)],
        out_specs=[per_b(hp, SSM_D_STATE), per_b(1, hp)],
        out_shape=[jax.ShapeDtypeStruct((nb, hp, SSM_D_STATE), F32),
                   jax.ShapeDtypeStruct((nb, 1, hp), F32)],
        compiler_params=_params("arbitrary"),
        name="ssd_sample",
    )(h0, x_t, x.reshape(nb, 1, hp), bm, cm, dt_raw.reshape(nb, 1, nh), lanes(dt_raw).reshape(nb, 1, hp),
      dt_bias.reshape(1, nh), lanes(dt_bias).reshape(1, hp), a_log.reshape(1, nh), lanes(a_log).reshape(1, hp),
      d_lanes)


KV_COLS = ATTN_N_KV_HEADS * ATTN_HEAD_DIM
Q_LANES = ATTN_GROUP * MOBA_BLOCK


def _kmeans_kernel(k_ref, o_ref):
    o_ref[...] = jnp.mean(k_ref[...], axis=0, keepdims=True).reshape(o_ref.shape)


def block_means(k):
    nb = SEQ // MOBA_BLOCK
    return pl.pallas_call(
        _kmeans_kernel,
        grid=(nb,),
        in_specs=[pl.BlockSpec((MOBA_BLOCK, KV_COLS), lambda i: (i, 0))],
        out_specs=pl.BlockSpec((None, 1, KV_COLS), lambda i: (i, 0, 0)),
        out_shape=jax.ShapeDtypeStruct((nb, 1, KV_COLS), F32),
        compiler_params=_params("arbitrary"),
        name="block_means",
    )(k)


def _top3_mask_rows(gate):
    n = gate.shape[0]
    rows = lax.broadcasted_iota(I32, gate.shape, 0)
    sel = jnp.zeros(gate.shape, jnp.bool_)
    for _ in range(MOBA_TOPK):
        m = jnp.max(gate, axis=0, keepdims=True)
        first = jnp.min(jnp.where(gate == m, rows, n), axis=0, keepdims=True)
        hit = jnp.logical_and(rows == first, m > NEG_INF)
        sel = jnp.logical_or(sel, hit)
        gate = jnp.where(rows == first, NEG_INF, gate)
    return sel


MASKED_SCORE = -1e30
LOG2_E = 1.4426950408889634


NO_BLOCK_LANE = LANES - 1
DENOM_ROWS = 16


def _moba_prompt_kernel(qi_ref, fl_ref, ida_ref, idb_ref, q_ref, ka_ref, kb_ref, va_ref, vb_ref, mean_ref, o_ref,
                        qaug_ref, s_ref, m_ref, acc_ref):
    t = pl.program_id(0)
    i = qi_ref[t]
    is_first = (fl_ref[t] & 1) == 1
    is_last = (fl_ref[t] & 2) == 2
    nb = mean_ref.shape[0]
    q_scale = ATTN_HEAD_DIM ** -0.5 * LOG2_E
    hd = ATTN_HEAD_DIM

    @pl.when(i == nb)
    def _():
        o_ref[...] = jnp.zeros(o_ref.shape, o_ref.dtype)

    @pl.when(jnp.logical_and(i < nb, is_first))
    def _():
        blk = lax.broadcasted_iota(I32, (nb, Q_LANES), 0)
        pad_row = lax.broadcasted_iota(I32, (LANES - nb, MOBA_BLOCK), 0)
        pad = jnp.where(pad_row == NO_BLOCK_LANE - nb, MASKED_SCORE, 0.0)
        for g in range(ATTN_N_KV_HEADS):
            q_g = jnp.concatenate([q_ref[:, (ATTN_GROUP * g + r) * hd:(ATTN_GROUP * g + r + 1) * hd]
                                   for r in range(ATTN_GROUP)], axis=0)
            gate = lax.dot_general(mean_ref[:, g * hd:(g + 1) * hd].astype(BF16), q_g.astype(BF16),
                                   (((1,), (1,)), ((), ())), preferred_element_type=F32)
            sel = _top3_mask_rows(jnp.where(blk < i, gate, NEG_INF))
            bias = jnp.where(jnp.logical_or(sel, blk == i), 0.0, MASKED_SCORE)
            bias_q = jnp.concatenate(
                [jnp.concatenate([bias[:, r * MOBA_BLOCK:(r + 1) * MOBA_BLOCK], pad], axis=0).T
                 for r in range(ATTN_GROUP)], axis=0)
            qaug_ref[g] = jnp.concatenate([(q_g * q_scale).astype(BF16), bias_q.astype(BF16)], axis=1)
        m_ref[...] = jnp.full(m_ref.shape, NEG_INF, F32)
        acc_ref[...] = jnp.zeros(acc_ref.shape, F32)

    def attend(own):
        blk_lane = lax.broadcasted_iota(I32, (MOBA_BLOCK, LANES), 1)
        halves = [(ka_ref, jnp.where(blk_lane == ida_ref[t], 1.0, 0.0).astype(BF16)),
                  (kb_ref, jnp.where(blk_lane == idb_ref[t], 1.0, 0.0).astype(BF16))]
        for g in range(ATTN_N_KV_HEADS):
            k_aug = jnp.concatenate(
                [jnp.concatenate([k_ref[:, g * hd:(g + 1) * hd].astype(BF16), onehot], axis=1)
                 for k_ref, onehot in halves], axis=0)
            s_ref[g] = lax.dot_general(k_aug, qaug_ref[g], (((1,), (1,)), ((), ())), preferred_element_type=F32)
        sum_row = lax.broadcasted_iota(I32, (DENOM_ROWS, 2 * MOBA_BLOCK), 0) == 0
        ones_rows = jnp.where(sum_row, 1.0, 0.0).astype(BF16)
        if own:
            key = lax.broadcasted_iota(I32, (2 * MOBA_BLOCK, Q_LANES), 0)
            qry = lax.broadcasted_iota(I32, (2 * MOBA_BLOCK, Q_LANES), 1) & (MOBA_BLOCK - 1)
            visible = jnp.logical_or(key <= qry, key >= MOBA_BLOCK)
        for g in range(ATTN_N_KV_HEADS):
            s = s_ref[g]
            if own:
                s = jnp.where(visible, s, NEG_INF)
            m_old = m_ref[g]
            m_new = jnp.maximum(m_old, jnp.max(s, axis=0, keepdims=True))
            alpha = jnp.exp2(m_old - m_new)
            p = jnp.exp2(s - m_new)
            m_ref[g] = m_new
            v_t = jnp.concatenate([va_ref[:, g * hd:(g + 1) * hd], vb_ref[:, g * hd:(g + 1) * hd]],
                                  axis=0).T.astype(BF16)
            v_aug = jnp.concatenate([v_t, ones_rows], axis=0)
            acc_ref[g] = alpha * acc_ref[g] + jnp.dot(v_aug, p.astype(BF16), preferred_element_type=F32)

    @pl.when(jnp.logical_and(i < nb, is_first))
    def _():
        attend(True)

    @pl.when(jnp.logical_and(i < nb, jnp.logical_not(is_first)))
    def _():
        attend(False)

    @pl.when(jnp.logical_and(i < nb, is_last))
    def _():
        for g in range(ATTN_N_KV_HEADS):
            out = (acc_ref[g, :hd, :] / acc_ref[g, hd:hd + 1, :]).T
            for r in range(ATTN_GROUP):
                h = ATTN_GROUP * g + r
                o_ref[:, h * hd:(h + 1) * hd] = out[r * MOBA_BLOCK:(r + 1) * MOBA_BLOCK, :].astype(o_ref.dtype)


def moba_prompt(q, k, v, means):
    nb = SEQ // MOBA_BLOCK
    assert nb < NO_BLOCK_LANE
    steps = []
    for i in range(nb):
        pairs = [(i, i - 1)] + [(a, a - 1) for a in range(i - 2, -1, -2)]
        for n, (a, b) in enumerate(pairs):
            flags = (1 if n == 0 else 0) | (2 if n == len(pairs) - 1 else 0)
            steps.append((i, flags, a, b if b >= 0 else NO_BLOCK_LANE))
    steps.append((nb, 0, NO_BLOCK_LANE, NO_BLOCK_LANE))
    tables = [jnp.asarray(np.array([s[c] for s in steps], np.int32)) for c in range(4)]
    blk = lambda ids, t: jnp.where(ids[t] < nb, ids[t], 0)
    ka_spec = pl.BlockSpec((MOBA_BLOCK, KV_COLS), lambda t, qi, fl, ida, idb: (blk(ida, t), 0))
    kb_spec = pl.BlockSpec((MOBA_BLOCK, KV_COLS), lambda t, qi, fl, ida, idb: (blk(idb, t), 0))
    row_spec = pl.BlockSpec((MOBA_BLOCK, D_MODEL), lambda t, qi, *_: (qi[t], 0))
    return pl.pallas_call(
        _moba_prompt_kernel,
        grid_spec=pltpu.PrefetchScalarGridSpec(
            num_scalar_prefetch=4,
            grid=(len(steps),),
            in_specs=[row_spec, ka_spec, kb_spec, ka_spec, kb_spec,
                      pl.BlockSpec((nb, KV_COLS), lambda t, *_: (0, 0))],
            out_specs=row_spec,
            scratch_shapes=[pltpu.VMEM((ATTN_N_KV_HEADS, Q_LANES, ATTN_HEAD_DIM + LANES), BF16),
                            pltpu.VMEM((ATTN_N_KV_HEADS, 2 * MOBA_BLOCK, Q_LANES), F32),
                            pltpu.VMEM((ATTN_N_KV_HEADS, 1, Q_LANES), F32),
                            pltpu.VMEM((ATTN_N_KV_HEADS, ATTN_HEAD_DIM + DENOM_ROWS, Q_LANES), F32)]),
        out_shape=jax.ShapeDtypeStruct((TOK_PAD, D_MODEL), BF16),
        compiler_params=_params("arbitrary"),
        name="moba_prompt",
    )(*tables, q, k, k, v, v, means)


PAGE_ROWS = PAGE_SIZE * ATTN_N_KV_HEADS
SEL_KEYS = MOBA_TOPK * MOBA_BLOCK


def _page_copy(pool_hbm, page, dst, sem):
    return pltpu.make_async_copy(pool_hbm.at[page], dst, sem)


def _sample_select_kernel(pt_ref, q_ref, kc_hbm, o_ref, kbuf, means_ref, sems):
    b = pl.program_id(0)
    slot = b % 2

    def fetch(seq, slot_, start):
        for p in range(N_PAGES):
            cp = _page_copy(kc_hbm, pt_ref[seq * N_PAGES + p], kbuf.at[slot_, p], sems.at[slot_])
            cp.start() if start else cp.wait()

    @pl.when(b == 0)
    def _():
        fetch(b, slot, True)

    @pl.when(b + 1 < pl.num_programs(0))
    def _():
        fetch(b + 1, 1 - slot, True)

    fetch(b, slot, False)
    fold = 8 // ATTN_N_KV_HEADS
    for n in range(N_PAST_BLOCKS):
        s = jnp.zeros((8, ATTN_HEAD_DIM), F32)
        for p in range(PAGES_PER_BLOCK):
            s = s + jnp.sum(kbuf[slot, n * PAGES_PER_BLOCK + p].reshape(PAGE_ROWS // 8, 8, ATTN_HEAD_DIM), axis=0)
        tot = s[:ATTN_N_KV_HEADS, :]
        for f in range(1, fold):
            tot = tot + s[f * ATTN_N_KV_HEADS:(f + 1) * ATTN_N_KV_HEADS, :]
        means_ref[n * ATTN_N_KV_HEADS:(n + 1) * ATTN_N_KV_HEADS, :] = tot * (1.0 / MOBA_BLOCK)
    gate = lax.dot_general(q_ref[...].astype(BF16), means_ref[...].astype(BF16), (((1,), (1,)), ((), ())),
                           preferred_element_type=F32)
    shape = gate.shape
    lane = lax.broadcasted_iota(I32, shape, 1)
    head = lax.broadcasted_iota(I32, shape, 0)
    gate = jnp.where((lane & (ATTN_N_KV_HEADS - 1)) == (head >> 2), gate, NEG_INF)
    out = jnp.zeros(shape, I32)
    for t in range(MOBA_TOPK):
        m = jnp.max(gate, axis=1, keepdims=True)
        first = jnp.min(jnp.where(gate == m, lane, shape[1]), axis=1, keepdims=True)
        out = jnp.where(lane == t, first >> 2, out)
        gate = jnp.where(lane == first, NEG_INF, gate)
    o_ref[...] = out


def sample_select(page_table, q_s, kc_pages):
    nb = q_s.shape[0]
    assert N_PAST_BLOCKS * ATTN_N_KV_HEADS == LANES
    return pl.pallas_call(
        _sample_select_kernel,
        grid_spec=pltpu.PrefetchScalarGridSpec(
            num_scalar_prefetch=1,
            grid=(nb,),
            in_specs=[pl.BlockSpec((None, ATTN_N_HEADS, ATTN_HEAD_DIM), lambda b, pt: (b, 0, 0)),
                      pl.BlockSpec(memory_space=pl.ANY)],
            out_specs=pl.BlockSpec((None, ATTN_N_HEADS, LANES), lambda b, pt: (b, 0, 0)),
            scratch_shapes=[pltpu.VMEM((2, N_PAGES, PAGE_ROWS, ATTN_HEAD_DIM), F32),
                            pltpu.VMEM((N_PAST_BLOCKS * ATTN_N_KV_HEADS, ATTN_HEAD_DIM), F32),
                            pltpu.SemaphoreType.DMA((2,))]),
        out_shape=jax.ShapeDtypeStruct((nb, ATTN_N_HEADS, LANES), I32),
        compiler_params=_params("arbitrary"),
        name="sample_select",
    )(page_table.reshape(-1), q_s, kc_pages)


def _head_rows_copy(pool_hbm, page, g, dst, sem):
    return pltpu.make_async_copy(pool_hbm.at[page, :, g, :], dst, sem)


def _sample_attend_kernel(pt_ref, sel_ref, q_ref, kn_ref, vn_ref, kc_hbm, vc_hbm, o_ref, kbuf, vbuf, sems):
    b = pl.program_id(0)
    slot = b % 2
    scale = ATTN_HEAD_DIM ** -0.5

    def fetch(seq, slot_, start):
        for h in range(ATTN_N_HEADS):
            g = h // ATTN_GROUP
            for t in range(MOBA_TOPK):
                blk = sel_ref[(seq * ATTN_N_HEADS + h) * MOBA_TOPK + t]
                for half in range(PAGES_PER_BLOCK):
                    page = pt_ref[seq * N_PAGES + blk * PAGES_PER_BLOCK + half]
                    rows = pl.ds(t * MOBA_BLOCK + half * PAGE_SIZE, PAGE_SIZE)
                    for cp in (_head_rows_copy(kc_hbm, page, g, kbuf.at[slot_, h, rows, :], sems.at[0, slot_]),
                               _head_rows_copy(vc_hbm, page, g, vbuf.at[slot_, h, rows, :], sems.at[1, slot_])):
                        cp.start() if start else cp.wait()

    @pl.when(b == 0)
    def _():
        fetch(b, slot, True)

    @pl.when(b + 1 < pl.num_programs(0))
    def _():
        fetch(b + 1, 1 - slot, True)

    fetch(b, slot, False)

    for h in range(ATTN_N_HEADS):
        g = h // ATTN_GROUP
        qh = q_ref[h:h + 1, :]
        q8 = jnp.broadcast_to(qh, (8, ATTN_HEAD_DIM)).astype(BF16)
        s = lax.dot_general(kbuf[slot, h].astype(BF16), q8, (((1,), (1,)), ((), ())),
                            preferred_element_type=F32)[:, 0:1] * scale
        s_own = jnp.sum(_mxu_round(qh) * _mxu_round(kn_ref[g:g + 1, :]), axis=1, keepdims=True) * scale
        m = jnp.maximum(jnp.max(s, axis=0, keepdims=True), s_own)
        p = jnp.exp(s - m)
        p_own = jnp.exp(s_own - m)
        denom = jnp.sum(p, axis=0, keepdims=True) + p_own
        o_sel = jnp.sum(_mxu_round(p / denom) * _mxu_round(vbuf[slot, h]), axis=0, keepdims=True)
        o_ref[h:h + 1, :] = o_sel + _mxu_round(p_own / denom) * _mxu_round(vn_ref[g:g + 1, :])


def sample_attend(page_table, sel, q_s, k_new, v_new, cache_k, cache_v):
    nb = q_s.shape[0]
    per_b = lambda rows: pl.BlockSpec((None, rows, ATTN_HEAD_DIM), lambda b, pt, sl: (b, 0, 0))
    any_spec = pl.BlockSpec(memory_space=pl.ANY)
    return pl.pallas_call(
        _sample_attend_kernel,
        grid_spec=pltpu.PrefetchScalarGridSpec(
            num_scalar_prefetch=2,
            grid=(nb,),
            in_specs=[per_b(ATTN_N_HEADS), per_b(ATTN_N_KV_HEADS), per_b(ATTN_N_KV_HEADS), any_spec, any_spec],
            out_specs=per_b(ATTN_N_HEADS),
            scratch_shapes=[pltpu.VMEM((2, ATTN_N_HEADS, SEL_KEYS, ATTN_HEAD_DIM), F32),
                            pltpu.VMEM((2, ATTN_N_HEADS, SEL_KEYS, ATTN_HEAD_DIM), F32),
                            pltpu.SemaphoreType.DMA((2, 2))]),
        out_shape=jax.ShapeDtypeStruct((nb, ATTN_N_HEADS, ATTN_HEAD_DIM), F32),
        compiler_params=_params("arbitrary"),
        name="sample_attend",
    )(page_table.reshape(-1), sel.reshape(-1), q_s, k_new, v_new, cache_k, cache_v)


def _router_kernel(x_ref, w_ref, b_ref, idx_ref, gate_ref):
    logits = jnp.dot(x_ref[...], w_ref[...].astype(BF16), preferred_element_type=F32) + b_ref[...]
    lane = lax.broadcasted_iota(I32, logits.shape, 1)
    logits = jnp.where(lane < N_EXPERTS, logits, NEG_INF)
    m1 = jnp.max(logits, axis=1, keepdims=True)
    i1 = jnp.min(jnp.where(logits == m1, lane, LANES), axis=1, keepdims=True)
    rest = jnp.where(lane == i1, NEG_INF, logits)
    m2 = jnp.max(rest, axis=1, keepdims=True)
    i2 = jnp.min(jnp.where(rest == m2, lane, LANES), axis=1, keepdims=True)
    e2 = jnp.exp(m2 - m1)
    g1 = 1.0 / (1.0 + e2)
    idx_ref[...] = jnp.where(lane == 0, i1, jnp.where(lane == 1, i2, 0))
    gate_ref[...] = jnp.where(lane == 0, g1, jnp.where(lane == 1, e2 * g1, 0.0))


def router(x, w_router, b_router):
    m, d = x.shape
    wp = jnp.zeros((d, LANES), F32).at[:, :N_EXPERTS].set(w_router)
    bp = jnp.zeros((1, LANES), F32).at[0, :N_EXPERTS].set(b_router)
    row = pl.BlockSpec((ROW_BLOCK, d), lambda i: (i, 0))
    out = pl.BlockSpec((ROW_BLOCK, LANES), lambda i: (i, 0))
    return pl.pallas_call(
        _router_kernel,
        grid=(m // ROW_BLOCK,),
        in_specs=[row, pl.BlockSpec((d, LANES), lambda i: (0, 0)), pl.BlockSpec((1, LANES), lambda i: (0, 0))],
        out_specs=[out, out],
        out_shape=[jax.ShapeDtypeStruct((m, LANES), I32), jax.ShapeDtypeStruct((m, LANES), F32)],
        compiler_params=_params("arbitrary"),
        name="router",
    )(x, wp, bp)


def _row_copy(src_hbm, row, dst, slot, sem):
    return pltpu.make_async_copy(src_hbm.at[pl.ds(row, 1), :], dst.at[pl.ds(slot, 1), :], sem)


GATHER_UNROLL = 8


def _moe_gather_kernel(tok_ref, nxt_ref, x_hbm, o_ref, buf, sems):
    i = pl.program_id(0)
    slot = i % 2

    def issue(idx_ref, slot_):
        def body(r2, carry):
            for q in range(2):
                r = 2 * r2 + q
                _row_copy(x_hbm, idx_ref[0, 0, r], buf.at[slot_], r, sems.at[slot_]).start(priority=q)
            return carry
        lax.fori_loop(0, MOE_ROWS // 2, body, 0, unroll=GATHER_UNROLL // 2)

    @pl.when(i == 0)
    def _():
        issue(tok_ref, slot)

    @pl.when(i + 1 < pl.num_programs(0))
    def _():
        issue(nxt_ref, 1 - slot)

    def wait(r, carry):
        _row_copy(x_hbm, 0, buf.at[slot], r, sems.at[slot]).wait()
        return carry

    lax.fori_loop(0, MOE_ROWS, wait, 0, unroll=GATHER_UNROLL)
    o_ref[...] = buf[slot].astype(o_ref.dtype)


def moe_gather(x, buf_tok):
    d = x.shape[1]
    tok = buf_tok.reshape(MOE_BLOCKS, 1, MOE_ROWS)
    idx_spec = lambda step: pl.BlockSpec((1, 1, MOE_ROWS), lambda i: (jnp.minimum(i + step, MOE_BLOCKS - 1), 0, 0),
                                         memory_space=pltpu.SMEM)
    return pl.pallas_call(
        _moe_gather_kernel,
        grid=(MOE_BLOCKS,),
        in_specs=[idx_spec(0), idx_spec(1), pl.BlockSpec(memory_space=pl.ANY)],
        out_specs=pl.BlockSpec((MOE_ROWS, d), lambda i: (i, 0)),
        out_shape=jax.ShapeDtypeStruct((MOE_BUF, d), BF16),
        scratch_shapes=[pltpu.VMEM((2, MOE_ROWS, d), F32), pltpu.SemaphoreType.DMA((2,))],
        compiler_params=_params("arbitrary"),
        name="moe_gather",
    )(tok, tok, x)


W_CHUNKS = 8


def _moe_expert_kernel(first_ref, count_ref, used_ref, w_hbm, x_hbm, o_hbm,
                       wstage, wb_ref, xbuf, obuf, xodd, oodd, wsem, xsem, osem, rsem, *, n_w, nj, bn):
    e = pl.program_id(0)
    j = pl.program_id(1)
    n_e = pl.num_programs(0)
    step = e * nj + j
    slot_w = step % 2
    first = first_ref[e]
    n = count_ref[e]
    cols = pl.ds(pl.multiple_of(j * bn, bn), bn)
    kc = w_hbm.shape[1] // W_CHUNKS

    unit = xbuf.shape[1] // MOE_ROWS
    n_trips = lax.div(n, unit)
    odd = n - n_trips * unit

    def rows(blk, nblk):
        return pl.ds(pl.multiple_of(blk * MOE_ROWS, MOE_ROWS), nblk * MOE_ROWS)

    def x_copy(k, slot):
        return pltpu.make_async_copy(x_hbm.at[rows(first + k * unit, unit), :], xbuf.at[slot], xsem.at[slot])

    def o_copy(k, slot):
        return pltpu.make_async_copy(obuf.at[slot], o_hbm.at[rows(first + k * unit, unit), cols], osem.at[slot])

    def single(ref):
        return ref.at[0, pl.ds(0, MOE_ROWS), :]

    def x_odd_copy():
        return pltpu.make_async_copy(x_hbm.at[rows(first + n - 1, 1), :], single(xodd), rsem.at[0])

    def o_single_copy(blk, src):
        return pltpu.make_async_copy(src, o_hbm.at[rows(blk, 1), cols], rsem.at[1])

    def expert_rows(x):
        acc = jnp.dot(x, wb_ref[0], preferred_element_type=F32)
        if n_w == 2:
            acc = _silu(acc) * jnp.dot(x, wb_ref[1], preferred_element_type=F32)
        return acc.astype(obuf.dtype)

    def w_copies(e_, j_, slot, c):
        krows = pl.ds(pl.multiple_of(c * kc, kc), kc)
        return [pltpu.make_async_copy(
            w_hbm.at[e_, krows, pl.ds(pl.multiple_of((j_ + m * nj) * bn, bn), bn)],
            wstage.at[slot, m, krows, :], wsem.at[slot]) for m in range(n_w)]

    last_j = j == nj - 1
    e_next = jnp.where(last_j, e + 1, e)
    j_next = jnp.where(last_j, 0, j + 1)
    has_next = step + 1 < n_e * nj

    def fetch_next(c):
        for cp in w_copies(e_next, j_next, 1 - slot_w, c):
            cp.start()

    @pl.when(step == 0)
    def _():
        for c in range(W_CHUNKS):
            for cp in w_copies(e, j, slot_w, c):
                cp.start()

    @pl.when(n_trips > 0)
    def _():
        x_copy(0, 0).start()

    if unit > 1:
        @pl.when(odd > 0)
        def _():
            x_odd_copy().start()

    for c in range(W_CHUNKS):
        for cp in w_copies(e, j, slot_w, c):
            cp.wait()

    @pl.when(n > 0)
    def _():
        for m in range(n_w):
            wb_ref[m] = wstage[slot_w, m].astype(BF16)

    def body(k, carry):
        slot = k % 2
        x_copy(k, slot).wait()

        @pl.when(k + 1 < n_trips)
        def _():
            x_copy(k + 1, 1 - slot).start()

        for u in range(unit):
            @pl.when(jnp.logical_and(has_next, k * unit + u < W_CHUNKS))
            def _():
                fetch_next(k * unit + u)

        @pl.when(k >= 2)
        def _():
            o_copy(k - 2, slot).wait()

        obuf[slot] = expert_rows(xbuf[slot])
        o_copy(k, slot).start()
        return carry

    lax.fori_loop(0, n_trips, body, 0)

    @pl.when(has_next)
    def _():
        def rest(c, carry):
            fetch_next(c)
            return carry

        lax.fori_loop(jnp.minimum(n_trips * unit, W_CHUNKS), W_CHUNKS, rest, 0)

    @pl.when(n_trips >= 2)
    def _():
        o_copy(n_trips - 2, n_trips % 2).wait()

    @pl.when(n_trips >= 1)
    def _():
        o_copy(n_trips - 1, (n_trips - 1) % 2).wait()

    if unit > 1:
        @pl.when(odd > 0)
        def _():
            x_odd_copy().wait()
            oodd[0] = expert_rows(xodd[0])
            cp = o_single_copy(first + n - 1, single(oodd))
            cp.start()
            cp.wait()

    @pl.when(e == n_e - 1)
    def _():
        obuf[0] = jnp.zeros(obuf.shape[1:], obuf.dtype)

        def fill(blk, carry):
            cp = o_single_copy(blk, single(obuf))
            cp.start()
            cp.wait()
            return carry

        lax.fori_loop(used_ref[0], MOE_BLOCKS, fill, 0)


def moe_expert_matmul(xs, w, plan, *, swiglu, bn, out_dtype, blocks_per_trip):
    assert blocks_per_trip in (1, 2)
    trip_rows = blocks_per_trip * MOE_ROWS
    first, count, used = plan
    r, k = xs.shape
    n_w = 2 if swiglu else 1
    f = w.shape[2] // n_w
    nj = f // bn
    assert f % bn == 0 and k % (8 * W_CHUNKS) == 0
    hbm = pl.BlockSpec(memory_space=pl.ANY)
    return pl.pallas_call(
        functools.partial(_moe_expert_kernel, n_w=n_w, nj=nj, bn=bn),
        grid_spec=pltpu.PrefetchScalarGridSpec(
            num_scalar_prefetch=3,
            grid=(N_EXPERTS, nj),
            in_specs=[hbm, hbm],
            out_specs=hbm,
            scratch_shapes=[pltpu.VMEM((2, n_w, k, bn), F32), pltpu.VMEM((n_w, k, bn), BF16),
                            pltpu.VMEM((2, trip_rows, k), xs.dtype), pltpu.VMEM((2, trip_rows, bn), out_dtype),
                            pltpu.VMEM((1, MOE_ROWS, k), xs.dtype), pltpu.VMEM((1, MOE_ROWS, bn), out_dtype),
                            pltpu.SemaphoreType.DMA((2,)), pltpu.SemaphoreType.DMA((2,)),
                            pltpu.SemaphoreType.DMA((2,)), pltpu.SemaphoreType.DMA((2,))]),
        out_shape=jax.ShapeDtypeStruct((r, f), out_dtype),
        compiler_params=_params("arbitrary", "arbitrary"),
        name="moe_gu" if swiglu else "moe_down",
    )(first, count, used, w, xs)


def _moe_combine_ln_kernel(pos_ref, nxt_ref, y_hbm, x_ref, gate_ref, g_ref, b_ref, o_ref, ob_ref, buf, sems):
    i = pl.program_id(0)
    slot = i % 2

    def issue(idx_ref, slot_):
        def body(r, carry):
            for t in range(MOE_TOPK):
                _row_copy(y_hbm, idx_ref[0, 0, MOE_TOPK * r + t], buf.at[slot_, t], r,
                          sems.at[slot_]).start(priority=t % 2)
            return carry
        lax.fori_loop(0, ROW_BLOCK, body, 0, unroll=GATHER_UNROLL)

    @pl.when(i == 0)
    def _():
        issue(pos_ref, slot)

    @pl.when(i + 1 < pl.num_programs(0))
    def _():
        issue(nxt_ref, 1 - slot)

    def wait(r, carry):
        for t in range(MOE_TOPK):
            _row_copy(y_hbm, 0, buf.at[slot, t], r, sems.at[slot]).wait()
        return carry

    lax.fori_loop(0, ROW_BLOCK, wait, 0, unroll=GATHER_UNROLL)
    gates = gate_ref[...]
    ff = buf[slot, 0] * gates[:, 0:1]
    for t in range(1, MOE_TOPK):
        ff = ff + buf[slot, t] * gates[:, t:t + 1]
    out = _layernorm_rows(DEEPNORM_ALPHA * x_ref[...] + ff, g_ref[...], b_ref[...])
    o_ref[...] = out
    ob_ref[...] = out.astype(BF16)


def moe_combine_ln(x, ys, pos, gates, g, b):
    m, d = x.shape
    nblk = m // ROW_BLOCK
    row = pl.BlockSpec((ROW_BLOCK, d), lambda i: (i, 0))
    vec = pl.BlockSpec((1, d), lambda i: (0, 0))
    pos3 = pos.reshape(nblk, 1, MOE_TOPK * ROW_BLOCK)
    idx_spec = lambda step: pl.BlockSpec((1, 1, MOE_TOPK * ROW_BLOCK),
                                         lambda i: (jnp.minimum(i + step, nblk - 1), 0, 0), memory_space=pltpu.SMEM)
    return pl.pallas_call(
        _moe_combine_ln_kernel,
        grid=(nblk,),
        in_specs=[idx_spec(0), idx_spec(1), pl.BlockSpec(memory_space=pl.ANY),
                  row, pl.BlockSpec((ROW_BLOCK, LANES), lambda i: (i, 0)), vec, vec],
        out_specs=[row, row],
        out_shape=[jax.ShapeDtypeStruct((m, d), F32), jax.ShapeDtypeStruct((m, d), BF16)],
        scratch_shapes=[pltpu.VMEM((2, MOE_TOPK, ROW_BLOCK, d), F32), pltpu.SemaphoreType.DMA((2,))],
        compiler_params=_params("arbitrary"),
        name="moe_combine_ln",
    )(pos3, pos3, ys, x, gates, g.reshape(1, d), b.reshape(1, d))


def moe_plan(idx):
    e = idx[:N_TOK, :MOE_TOPK].reshape(N_ASSIGN)
    onehot = (e[:, None] == jnp.arange(N_EXPERTS, dtype=I32)[None, :]).astype(I32)
    seg = 64
    assert N_ASSIGN % seg == 0 and seg <= 256
    oh = onehot.astype(F32).reshape(N_ASSIGN // seg, seg, N_EXPERTS)
    tri = lambda n, k: jnp.tril(jnp.ones((n, n), F32), k)
    within = jnp.einsum("ts,rse->rte", tri(seg, -1), oh)
    offset = jnp.einsum("rq,qe->re", tri(N_ASSIGN // seg, -1), jnp.sum(oh, axis=1))
    before = (within + offset[:, None, :]).reshape(N_ASSIGN, N_EXPERTS).astype(I32)
    rank = jnp.sum(before * onehot, axis=1)
    counts = jnp.sum(onehot, axis=0)
    padded = (counts + MOE_ROWS - 1) // MOE_ROWS * MOE_ROWS
    pad_end = jnp.cumsum(padded)
    pad_start = pad_end - padded
    dest = pad_start[e] + rank
    tok = jnp.arange(N_ASSIGN, dtype=I32) // MOE_TOPK
    buf_tok = jnp.zeros((MOE_BUF,), I32).at[dest].set(tok, unique_indices=True)
    blocks = ((pad_start // MOE_ROWS).astype(I32), (padded // MOE_ROWS).astype(I32),
              (pad_end[-1:] // MOE_ROWS).astype(I32))
    pos = jnp.zeros((TOK_PAD * MOE_TOPK,), I32).at[:N_ASSIGN].set(dest)
    return buf_tok, blocks, pos


def _rope_tables():
    half = ROT_DIM // 2
    inv = ROPE_THETA ** (-jnp.arange(half, dtype=F32) * 2.0 / ROT_DIM)
    pos = jnp.concatenate([jnp.arange(SEQ, dtype=I32), jnp.full((TOK_PAD - SEQ,), PAST_LEN, I32)])
    ang = pos.astype(F32)[:, None] * inv[None, :]
    cos, sin = jnp.cos(ang), jnp.sin(ang)
    rest = LANES - ROT_DIM
    cf = jnp.concatenate([cos, cos, jnp.ones((TOK_PAD, rest), F32)], axis=1)
    s1 = jnp.concatenate([-sin, jnp.zeros((TOK_PAD, LANES - half), F32)], axis=1)
    s2 = jnp.concatenate([jnp.zeros((TOK_PAD, half), F32), sin, jnp.zeros((TOK_PAD, rest), F32)], axis=1)
    return cf, s1, s2


def _tokens(prompt, sample):
    pad = jnp.zeros((TOK_PAD - N_TOK, prompt.shape[1]), prompt.dtype)
    return jnp.concatenate([prompt, sample, pad], axis=0)


def _put_sample(buf, rows):
    tail = jnp.zeros((TOK_PAD - SEQ, buf.shape[1]), buf.dtype).at[:rows.shape[0]].set(rows.astype(buf.dtype))
    return lax.dynamic_update_slice(buf, tail, (SEQ, 0))


def kernel(x_prompt, x_sample, state_ssm, state_conv, cache_k, cache_v, page_table, p_prompt, p_sample, ln_g, ln_b, ssm_w_in, ssm_conv_w, ssm_conv_b, ssm_dt_bias, ssm_A_log, ssm_D, ssm_norm_w, ssm_w_out, attn_w_kv, attn_w_q, attn_w_o, ffn_w_gu, ffn_w_down, moe_w_router, moe_b_router, moe_w_gu, moe_w_down, ple_w_gate, ple_b_gate, ple_w_proj):
    x0b = _tokens(x_prompt[0].astype(BF16), x_sample[:, 0].astype(BF16))
    x0_tail = jnp.zeros((TOK_PAD - SEQ, D_MODEL), F32).at[:DEC_BATCH].set(x_sample[:, 0])
    p_tok = [_tokens(p_prompt[i, 0], p_sample[i, :, 0]).astype(BF16) for i in range(2)]

    w_in_t = jnp.swapaxes(ssm_w_in, 1, 2)[0]
    zx_cols = SSM_D_INNER + SSM_CONV_DIM
    zx = matmul(x0b, w_in_t, bn=1024, ncols=zx_cols, w_transposed=True)
    dt_raw = matmul(x0b, w_in_t, bn=LANES, col0=zx_cols, ncols=LANES, w_transposed=True)[:, :SSM_N_HEADS]

    st_conv = state_conv[0]
    x_raw_s = zx[SEQ:N_TOK, SSM_D_INNER:]
    xbc_s = conv_sample(jnp.transpose(st_conv, (1, 0, 2)), x_raw_s, ssm_conv_w[0], ssm_conv_b[0])
    conv_prompt_out = zx[SEQ - (SSM_CONV_W - 1):SEQ, SSM_D_INNER:].reshape(1, 1, SSM_CONV_W - 1, SSM_CONV_DIM)
    conv_sample_out = jnp.concatenate([st_conv[:, 1:], x_raw_s[:, None, :]], axis=1)[None]

    d_lanes = jnp.repeat(ssm_D[0], SSM_HEAD_DIM).reshape(1, SSM_D_INNER)
    gated, h_prompt = ssm_prompt(zx, dt_raw[:SEQ], ssm_conv_w[0], ssm_conv_b[0], ssm_dt_bias[0], ssm_A_log[0],
                                 d_lanes, ssm_norm_w[0])
    xs_s = xbc_s[:DEC_BATCH]
    gn = SSM_N_GROUPS * SSM_D_STATE
    h_sample, y_s = ssd_sample(
        state_ssm[0].reshape(DEC_BATCH, SSM_N_HEADS * SSM_HEAD_DIM, SSM_D_STATE),
        xs_s[:, :SSM_D_INNER],
        xs_s[:, SSM_D_INNER:SSM_D_INNER + gn].reshape(DEC_BATCH, SSM_N_GROUPS, SSM_D_STATE),
        xs_s[:, SSM_D_INNER + gn:].reshape(DEC_BATCH, SSM_N_GROUPS, SSM_D_STATE),
        dt_raw[SEQ:N_TOK], ssm_dt_bias[0], ssm_A_log[0], d_lanes)
    y_tail = jnp.zeros((TOK_PAD - SEQ, SSM_D_INNER), F32).at[:DEC_BATCH].set(y_s.reshape(DEC_BATCH, SSM_D_INNER))
    gated = lax.dynamic_update_slice(gated, gate_norm(y_tail, zx[SEQ:], ssm_norm_w[0]), (SEQ, 0))
    mix = matmul(gated, ssm_w_out[0], bn=512)
    x1, x1b = add_ln(x_prompt[0], mix, ln_g[0, 0], ln_b[0, 0], x_tail=x0_tail)

    hid = matmul_swiglu(x1b, ffn_w_gu[0], bn=512)
    ff = matmul(hid, ffn_w_down[0], bn=512)
    x2, x2b = add_ln(x1, ff, ln_g[0, 1], ln_b[0, 1])
    x3, x3b = ple_mix(x2, x2b, p_tok[0], ple_w_gate, ple_b_gate, ple_w_proj, 0)

    rope = _rope_tables()
    k_all = matmul(x3b, attn_w_kv, bn=KV_COLS, ncols=KV_COLS, rope=rope)
    v_all = matmul(x3b, attn_w_kv, bn=KV_COLS, col0=KV_COLS, ncols=KV_COLS)
    q_all = matmul(x3b, attn_w_q[0], bn=1024, rope=rope)
    att = moba_prompt(q_all, k_all, v_all, block_means(k_all).reshape(SEQ // MOBA_BLOCK, KV_COLS))

    q_s = q_all[SEQ:N_TOK].reshape(DEC_BATCH, ATTN_N_HEADS, ATTN_HEAD_DIM)
    k_s = k_all[SEQ:N_TOK].reshape(DEC_BATCH, ATTN_N_KV_HEADS, ATTN_HEAD_DIM)
    v_s = v_all[SEQ:N_TOK].reshape(DEC_BATCH, ATTN_N_KV_HEADS, ATTN_HEAD_DIM)
    n_pool = cache_k.shape[0]
    sel = sample_select(page_table, q_s, cache_k.reshape(n_pool, PAGE_ROWS, ATTN_HEAD_DIM))[:, :, :MOBA_TOPK]
    att_s = sample_attend(page_table, sel, q_s, k_s, v_s, cache_k, cache_v)
    att = _put_sample(att, att_s.reshape(DEC_BATCH, D_MODEL))
    mix = matmul(att, attn_w_o[0], bn=1024)
    x4, x4b = add_ln(x3, mix, ln_g[1, 0], ln_b[1, 0])

    idx, gates = router(x4b, moe_w_router[0], moe_b_router[0])
    buf_tok, blocks, pos = moe_plan(idx)
    xs = moe_gather(x4, buf_tok)
    hs = moe_expert_matmul(xs, moe_w_gu[0], blocks, swiglu=True, bn=1024, out_dtype=BF16, blocks_per_trip=2)
    ys = moe_expert_matmul(hs, moe_w_down[0], blocks, swiglu=False, bn=512, out_dtype=F32, blocks_per_trip=1)
    x5, x5b = moe_combine_ln(x4, ys, pos, gates, ln_g[1, 1], ln_b[1, 1])
    y_prompt, y_sample = ple_mix(x5, x5b, p_tok[1], ple_w_gate, ple_b_gate, ple_w_proj, 1, final=True)

    state_shape = (1, -1, SSM_N_HEADS, SSM_HEAD_DIM, SSM_D_STATE)
    kv_shape = (-1, ATTN_N_KV_HEADS, ATTN_HEAD_DIM)
    return (y_prompt.reshape(1, SEQ, D_MODEL),
            y_sample.reshape(DEC_BATCH, 1, D_MODEL),
            h_prompt.reshape(state_shape),
            conv_prompt_out,
            k_all[:SEQ].reshape((1,) + (SEQ,) + kv_shape[1:]),
            v_all[:SEQ].reshape((1,) + (SEQ,) + kv_shape[1:]),
            h_sample.reshape(state_shape),
            conv_sample_out,
            k_all[SEQ:N_TOK].reshape((DEC_BATCH, 1) + kv_shape[1:]),
            v_all[SEQ:N_TOK].reshape((DEC_BATCH, 1) + kv_shape[1:]))
```
